```python
import math
import jax, jax.numpy as jnp
from jax import lax
import numpy as np

D_MODEL = 1024
BATCH = 8
SEQ = 8192
DEPTH = 4

SSM_WIDTH = D_MODEL // 2
SSM_GROUP = 16
SSM_GROUPS = SSM_WIDTH // SSM_GROUP
SSM_STATE = 64
MLA_HEADS = 8
QK_NOPE = 64
QK_ROPE = 32
QK_HEAD = QK_NOPE + QK_ROPE
V_HEAD = 64
Q_LORA = 384
KV_LORA = 256
MLA_WIDTH = MLA_HEADS * V_HEAD
ROPE_THETA = 10000.0
Q_BLOCK = 128
D_FF = 4 * D_MODEL
EPS = 1e-6
IN_SIZES = (SSM_WIDTH, Q_LORA, KV_LORA, QK_ROPE, 2 * D_MODEL)
IN_COLS = sum(IN_SIZES)
IN_SPLITS = [int(v) for v in np.cumsum(IN_SIZES)[:-1]]

kernel_name = "hybrid_s5_mla_encoder"


def rms_norm(x, g):
    xf = x.astype(jnp.float32)
    y = xf * lax.rsqrt(jnp.mean(xf * xf, axis=-1, keepdims=True) + EPS)
    return (y * g.astype(jnp.float32)).astype(x.dtype)


def rope_tables(seq, dtype):
    half = QK_ROPE // 2
    inv_freq = ROPE_THETA ** (-jnp.arange(half, dtype=jnp.float32) / half)
    ang = jnp.arange(seq, dtype=jnp.float32)[:, None] * inv_freq[None, :]
    return jnp.cos(ang).astype(dtype), jnp.sin(ang).astype(dtype)


def apply_rope(x, cos, sin):
    x1, x2 = jnp.split(x, 2, axis=-1)
    return jnp.concatenate([x1 * cos - x2 * sin, x1 * sin + x2 * cos], axis=-1)


def _affine_combine(e1, e2):
    a1r, a1i, b1r, b1i = e1
    a2r, a2i, b2r, b2i = e2
    ar = a2r * a1r - a2i * a1i
    ai = a2r * a1i + a2i * a1r
    br = a2r * b1r - a2i * b1i + b2r
    bi = a2r * b1i + a2i * b1r + b2i
    return ar, ai, br, bi


def zoh_discretise(lam_re, lam_im, log_step, b_re, b_im):
    f32 = jnp.float32
    lam_re = lam_re.astype(f32); lam_im = lam_im.astype(f32)
    step = jnp.exp(log_step.astype(f32))[:, None]
    mag = jnp.exp(lam_re * step)
    abar_r = mag * jnp.cos(lam_im * step)
    abar_i = mag * jnp.sin(lam_im * step)
    nr = abar_r - 1.0
    ni = abar_i
    den = lam_re * lam_re + lam_im * lam_im
    fr = (nr * lam_re + ni * lam_im) / den
    fi = (ni * lam_re - nr * lam_im) / den
    b_re = b_re.astype(f32); b_im = b_im.astype(f32)
    bbar_r = fr[..., None] * b_re - fi[..., None] * b_im
    bbar_i = fr[..., None] * b_im + fi[..., None] * b_re
    return abar_r, abar_i, bbar_r, bbar_i


def s5_states(u, lam_re, lam_im, log_step, b_re, b_im, reverse):
    abar_r, abar_i, bbar_r, bbar_i = zoh_discretise(lam_re, lam_im, log_step, b_re, b_im)
    bu_r = jnp.einsum('bsgp,gnp->bsgn', u, bbar_r)
    bu_i = jnp.einsum('bsgp,gnp->bsgn', u, bbar_i)
    a_r = jnp.broadcast_to(abar_r, bu_r.shape)
    a_i = jnp.broadcast_to(abar_i, bu_i.shape)
    _, _, xr, xi = lax.associative_scan(_affine_combine, (a_r, a_i, bu_r, bu_i), reverse=reverse, axis=1)
    return xr, xi


def s5_branch(u, lam_re, lam_im, log_step, b_re, b_im, c_re, c_im, d, w_glu, b_glu):
    bsz, seq, _ = u.shape
    uf = u.astype(jnp.float32).reshape(bsz, seq, SSM_GROUPS, SSM_GROUP)
    xr_f, xi_f = s5_states(uf, lam_re[0], lam_im[0], log_step[0], b_re[0], b_im[0], reverse=False)
    xr_b, xi_b = s5_states(uf, lam_re[1], lam_im[1], log_step[1], b_re[1], b_im[1], reverse=True)
    xr = xr_f + xr_b
    xi = xi_f + xi_b
    y = (jnp.einsum('bsgn,gpn->bsgp', xr, c_re.astype(jnp.float32))
         - jnp.einsum('bsgn,gpn->bsgp', xi, c_im.astype(jnp.float32))
         + d.astype(jnp.float32) * uf)
    y = y.reshape(bsz, seq, SSM_WIDTH).astype(u.dtype)
    y = jax.nn.gelu(y)
    return y * jax.nn.sigmoid(y @ w_glu + b_glu)


def mla_branch(cq, ckv, k_rope, q_norm_g, kv_norm_g, w_q_up, w_kv_up, q_head_g, k_head_g, cos, sin):
    bsz, seq, _ = cq.shape
    q = (rms_norm(cq, q_norm_g) @ w_q_up).reshape(bsz, seq, MLA_HEADS, QK_HEAD)
    kv = (rms_norm(ckv, kv_norm_g) @ w_kv_up).reshape(bsz, seq, MLA_HEADS, QK_NOPE + V_HEAD)
    k_nope, v = kv[..., :QK_NOPE], kv[..., QK_NOPE:]
    k = jnp.concatenate([k_nope, jnp.broadcast_to(k_rope[:, :, None, :], (bsz, seq, MLA_HEADS, QK_ROPE))], axis=-1)
    q = rms_norm(q, q_head_g)
    k = rms_norm(k, k_head_g)
    c4, s4 = cos[None, :, None, :], sin[None, :, None, :]
    q = jnp.concatenate([q[..., :QK_NOPE], apply_rope(q[..., QK_NOPE:], c4, s4)], axis=-1)
    k = jnp.concatenate([k[..., :QK_NOPE], apply_rope(k[..., QK_NOPE:], c4, s4)], axis=-1)
    q = q * (QK_HEAD ** -0.5)
    n_blk = seq // Q_BLOCK
    qb = q.reshape(bsz, n_blk, Q_BLOCK, MLA_HEADS, QK_HEAD).transpose(1, 0, 2, 3, 4)

    def attend(q_blk):
        s = jnp.einsum('bqhd,bkhd->bhqk', q_blk, k, preferred_element_type=jnp.float32)
        p = jax.nn.softmax(s, axis=-1)
        return jnp.einsum('bhqk,bkhd->bqhd', p.astype(v.dtype), v)

    o = lax.map(attend, qb)
    return o.transpose(1, 0, 2, 3, 4).reshape(bsz, seq, MLA_WIDTH)


def _fwd_setup_inputs(seed: int = 0) -> dict:
    key = jax.random.key(seed)
    ks = jax.random.split(key, 32)
    f32 = jnp.float32
    G, N, P = SSM_GROUPS, SSM_STATE, SSM_GROUP

    def nrm(k, shape, scale):
        return jax.random.normal(k, shape, f32) * scale

    def gain(k, shape):
        return 1.0 + 0.05 * jax.random.normal(k, shape, f32)

    n_idx = jnp.arange(N, dtype=f32)
    lam_re = -0.5 + 0.01 * jax.random.normal(ks[4], (DEPTH, 2, G, N), f32)
    lam_im = math.pi * n_idx + 0.01 * jax.random.normal(ks[5], (DEPTH, 2, G, N), f32)
    log_step = jax.random.uniform(ks[6], (DEPTH, 2, G), f32, math.log(1e-3), math.log(1e-1))
    b_scale = (1.0 / math.sqrt(P)) / math.sqrt(2.0)
    c_scale = (1.0 / math.sqrt(N)) / math.sqrt(2.0)
    return {
        "x": jax.random.normal(ks[0], (BATCH, SEQ, D_MODEL), f32),
        "mix_norm_g": gain(ks[1], (DEPTH, D_MODEL)),
        "w_in": nrm(ks[2], (DEPTH, D_MODEL, IN_COLS), D_MODEL ** -0.5),
        "b_gate": nrm(ks[3], (DEPTH, 2, D_MODEL), 0.02),
        "ssm_lam_re": lam_re,
        "ssm_lam_im": lam_im,
        "ssm_log_step": log_step,
        "ssm_b_re": nrm(ks[7], (DEPTH, 2, G, N, P), b_scale),
        "ssm_b_im": nrm(ks[8], (DEPTH, 2, G, N, P), b_scale),
        "ssm_c_re": nrm(ks[9], (DEPTH, G, P, N), c_scale),
        "ssm_c_im": nrm(ks[10], (DEPTH, G, P, N), c_scale),
        "ssm_d": nrm(ks[11], (DEPTH, G, P), 1.0),
        "w_glu": nrm(ks[12], (DEPTH, SSM_WIDTH, SSM_WIDTH), SSM_WIDTH ** -0.5),
        "b_glu": nrm(ks[13], (DEPTH, SSM_WIDTH), 0.02),
        "w_out_ssm": nrm(ks[14], (DEPTH, SSM_WIDTH, D_MODEL), SSM_WIDTH ** -0.5),
        "q_norm_g": gain(ks[15], (DEPTH, Q_LORA)),
        "kv_norm_g": gain(ks[16], (DEPTH, KV_LORA)),
        "w_q_up": nrm(ks[17], (DEPTH, Q_LORA, MLA_HEADS * QK_HEAD), Q_LORA ** -0.5),
        "w_kv_up": nrm(ks[18], (DEPTH, KV_LORA, MLA_HEADS * (QK_NOPE + V_HEAD)), KV_LORA ** -0.5),
        "q_head_g": gain(ks[19], (DEPTH, QK_HEAD)),
        "k_head_g": gain(ks[20], (DEPTH, QK_HEAD)),
        "w_out_mla": nrm(ks[21], (DEPTH, MLA_WIDTH, D_MODEL), MLA_WIDTH ** -0.5),
        "w_o": nrm(ks[22], (DEPTH, D_MODEL, D_MODEL), D_MODEL ** -0.5),
        "ffn_norm_g": gain(ks[23], (DEPTH, D_MODEL)),
        "w_ff1": nrm(ks[24], (DEPTH, D_MODEL, D_FF), D_MODEL ** -0.5),
        "w_ff2": nrm(ks[25], (DEPTH, D_FF, D_MODEL), D_FF ** -0.5),
    }


def _fwd_reference(x, mix_norm_g, w_in, b_gate, ssm_lam_re, ssm_lam_im, ssm_log_step, ssm_b_re, ssm_b_im,
              ssm_c_re, ssm_c_im, ssm_d, w_glu, b_glu, w_out_ssm, q_norm_g, kv_norm_g, w_q_up, w_kv_up,
              q_head_g, k_head_g, w_out_mla, w_o, ffn_norm_g, w_ff1, w_ff2):
    bsz, seq, _ = x.shape
    cos, sin = rope_tables(seq, x.dtype)
    for l in range(DEPTH):
        h = rms_norm(x, mix_norm_g[l])
        proj = h @ w_in[l]
        u_ssm, cq, ckv, k_rope, gate_pre = jnp.split(proj, IN_SPLITS, axis=-1)
        gates = jax.nn.sigmoid(gate_pre.reshape(bsz, seq, 2, D_MODEL) + b_gate[l])
        y_ssm = s5_branch(u_ssm, ssm_lam_re[l], ssm_lam_im[l], ssm_log_step[l], ssm_b_re[l], ssm_b_im[l],
                          ssm_c_re[l], ssm_c_im[l], ssm_d[l], w_glu[l], b_glu[l])
        y_mla = mla_branch(cq, ckv, k_rope, q_norm_g[l], kv_norm_g[l], w_q_up[l], w_kv_up[l],
                           q_head_g[l], k_head_g[l], cos, sin)
        merged = gates[:, :, 0, :] * (y_ssm @ w_out_ssm[l]) + gates[:, :, 1, :] * (y_mla @ w_out_mla[l])
        x = x + merged @ w_o[l]
        h = rms_norm(x, ffn_norm_g[l])
        x = x + jnp.square(jax.nn.relu(h @ w_ff1[l])) @ w_ff2[l]
    return x


import jax as _jax
import jax.numpy as _jnp

TWIN_FORMAT = 'train_step'
FWD_PARAMS = ['x', 'mix_norm_g', 'w_in', 'b_gate', 'ssm_lam_re', 'ssm_lam_im', 'ssm_log_step', 'ssm_b_re', 'ssm_b_im', 'ssm_c_re', 'ssm_c_im', 'ssm_d', 'w_glu', 'b_glu', 'w_out_ssm', 'q_norm_g', 'kv_norm_g', 'w_q_up', 'w_kv_up', 'q_head_g', 'k_head_g', 'w_out_mla', 'w_o', 'ffn_norm_g', 'w_ff1', 'w_ff2']
TWIN_WEIGHTS = ['mix_norm_g', 'w_in', 'b_gate', 'ssm_lam_re', 'ssm_lam_im', 'ssm_log_step', 'ssm_b_re', 'ssm_b_im', 'ssm_c_re', 'ssm_c_im', 'ssm_d', 'w_glu', 'b_glu', 'w_out_ssm', 'q_norm_g', 'kv_norm_g', 'w_q_up', 'w_kv_up', 'q_head_g', 'k_head_g', 'w_out_mla', 'w_o', 'ffn_norm_g', 'w_ff1', 'w_ff2']
TWIN_DIFF_INPUT = 'x'
TWIN_INPUTS = ['x', 'mix_norm_g', 'w_in', 'b_gate', 'ssm_lam_re', 'ssm_lam_im', 'ssm_log_step', 'ssm_b_re', 'ssm_b_im', 'ssm_c_re', 'ssm_c_im', 'ssm_d', 'w_glu', 'b_glu', 'w_out_ssm', 'q_norm_g', 'kv_norm_g', 'w_q_up', 'w_kv_up', 'q_head_g', 'k_head_g', 'w_out_mla', 'w_o', 'ffn_norm_g', 'w_ff1', 'w_ff2', 'loss_target', 'm_mix_norm_g', 'm_w_in', 'm_b_gate', 'm_ssm_lam_re', 'm_ssm_lam_im', 'm_ssm_log_step', 'm_ssm_b_re', 'm_ssm_b_im', 'm_ssm_c_re', 'm_ssm_c_im', 'm_ssm_d', 'm_w_glu', 'm_b_glu', 'm_w_out_ssm', 'm_q_norm_g', 'm_kv_norm_g', 'm_w_q_up', 'm_w_kv_up', 'm_q_head_g', 'm_k_head_g', 'm_w_out_mla', 'm_w_o', 'm_ffn_norm_g', 'm_w_ff1', 'm_w_ff2', 'v_mix_norm_g', 'v_w_in', 'v_b_gate', 'v_ssm_lam_re', 'v_ssm_lam_im', 'v_ssm_log_step', 'v_ssm_b_re', 'v_ssm_b_im', 'v_ssm_c_re', 'v_ssm_c_im', 'v_ssm_d', 'v_w_glu', 'v_b_glu', 'v_w_out_ssm', 'v_q_norm_g', 'v_kv_norm_g', 'v_w_q_up', 'v_w_kv_up', 'v_q_head_g', 'v_k_head_g', 'v_w_out_mla', 'v_w_o', 'v_ffn_norm_g', 'v_w_ff1', 'v_w_ff2']
TWIN_OUTPUTS = ['loss', 'grad_x', 'grad_mix_norm_g', 'grad_w_in', 'grad_b_gate', 'grad_ssm_lam_re', 'grad_ssm_lam_im', 'grad_ssm_log_step', 'grad_ssm_b_re', 'grad_ssm_b_im', 'grad_ssm_c_re', 'grad_ssm_c_im', 'grad_ssm_d', 'grad_w_glu', 'grad_b_glu', 'grad_w_out_ssm', 'grad_q_norm_g', 'grad_kv_norm_g', 'grad_w_q_up', 'grad_w_kv_up', 'grad_q_head_g', 'grad_k_head_g', 'grad_w_out_mla', 'grad_w_o', 'grad_ffn_norm_g', 'grad_w_ff1', 'grad_w_ff2', 'delta_mix_norm_g', 'delta_w_in', 'delta_b_gate', 'delta_ssm_lam_re', 'delta_ssm_lam_im', 'delta_ssm_log_step', 'delta_ssm_b_re', 'delta_ssm_b_im', 'delta_ssm_c_re', 'delta_ssm_c_im', 'delta_ssm_d', 'delta_w_glu', 'delta_b_glu', 'delta_w_out_ssm', 'delta_q_norm_g', 'delta_kv_norm_g', 'delta_w_q_up', 'delta_w_kv_up', 'delta_q_head_g', 'delta_k_head_g', 'delta_w_out_mla', 'delta_w_o', 'delta_ffn_norm_g', 'delta_w_ff1', 'delta_w_ff2', 'new_m_mix_norm_g', 'new_m_w_in', 'new_m_b_gate', 'new_m_ssm_lam_re', 'new_m_ssm_lam_im', 'new_m_ssm_log_step', 'new_m_ssm_b_re', 'new_m_ssm_b_im', 'new_m_ssm_c_re', 'new_m_ssm_c_im', 'new_m_ssm_d', 'new_m_w_glu', 'new_m_b_glu', 'new_m_w_out_ssm', 'new_m_q_norm_g', 'new_m_kv_norm_g', 'new_m_w_q_up', 'new_m_w_kv_up', 'new_m_q_head_g', 'new_m_k_head_g', 'new_m_w_out_mla', 'new_m_w_o', 'new_m_ffn_norm_g', 'new_m_w_ff1', 'new_m_w_ff2', 'new_v_mix_norm_g', 'new_v_w_in', 'new_v_b_gate', 'new_v_ssm_lam_re', 'new_v_ssm_lam_im', 'new_v_ssm_log_step', 'new_v_ssm_b_re', 'new_v_ssm_b_im', 'new_v_ssm_c_re', 'new_v_ssm_c_im', 'new_v_ssm_d', 'new_v_w_glu', 'new_v_b_glu', 'new_v_w_out_ssm', 'new_v_q_norm_g', 'new_v_kv_norm_g', 'new_v_w_q_up', 'new_v_w_kv_up', 'new_v_q_head_g', 'new_v_k_head_g', 'new_v_w_out_mla', 'new_v_w_o', 'new_v_ffn_norm_g', 'new_v_w_ff1', 'new_v_w_ff2']
TWIN_LEAF_KINDS = {'loss': 'loss', 'grad_x': 'grad_x', 'grad_mix_norm_g': 'grad_w', 'grad_w_in': 'grad_w', 'grad_b_gate': 'grad_w', 'grad_ssm_lam_re': 'grad_w', 'grad_ssm_lam_im': 'grad_w', 'grad_ssm_log_step': 'grad_w', 'grad_ssm_b_re': 'grad_w', 'grad_ssm_b_im': 'grad_w', 'grad_ssm_c_re': 'grad_w', 'grad_ssm_c_im': 'grad_w', 'grad_ssm_d': 'grad_w', 'grad_w_glu': 'grad_w', 'grad_b_glu': 'grad_w', 'grad_w_out_ssm': 'grad_w', 'grad_q_norm_g': 'grad_w', 'grad_kv_norm_g': 'grad_w', 'grad_w_q_up': 'grad_w', 'grad_w_kv_up': 'grad_w', 'grad_q_head_g': 'grad_w', 'grad_k_head_g': 'grad_w', 'grad_w_out_mla': 'grad_w', 'grad_w_o': 'grad_w', 'grad_ffn_norm_g': 'grad_w', 'grad_w_ff1': 'grad_w', 'grad_w_ff2': 'grad_w', 'delta_mix_norm_g': 'delta_w', 'delta_w_in': 'delta_w', 'delta_b_gate': 'delta_w', 'delta_ssm_lam_re': 'delta_w', 'delta_ssm_lam_im': 'delta_w', 'delta_ssm_log_step': 'delta_w', 'delta_ssm_b_re': 'delta_w', 'delta_ssm_b_im': 'delta_w', 'delta_ssm_c_re': 'delta_w', 'delta_ssm_c_im': 'delta_w', 'delta_ssm_d': 'delta_w', 'delta_w_glu': 'delta_w', 'delta_b_glu': 'delta_w', 'delta_w_out_ssm': 'delta_w', 'delta_q_norm_g': 'delta_w', 'delta_kv_norm_g': 'delta_w', 'delta_w_q_up': 'delta_w', 'delta_w_kv_up': 'delta_w', 'delta_q_head_g': 'delta_w', 'delta_k_head_g': 'delta_w', 'delta_w_out_mla': 'delta_w', 'delta_w_o': 'delta_w', 'delta_ffn_norm_g': 'delta_w', 'delta_w_ff1': 'delta_w', 'delta_w_ff2': 'delta_w', 'new_m_mix_norm_g': 'new_m', 'new_m_w_in': 'new_m', 'new_m_b_gate': 'new_m', 'new_m_ssm_lam_re': 'new_m', 'new_m_ssm_lam_im': 'new_m', 'new_m_ssm_log_step': 'new_m', 'new_m_ssm_b_re': 'new_m', 'new_m_ssm_b_im': 'new_m', 'new_m_ssm_c_re': 'new_m', 'new_m_ssm_c_im': 'new_m', 'new_m_ssm_d': 'new_m', 'new_m_w_glu': 'new_m', 'new_m_b_glu': 'new_m', 'new_m_w_out_ssm': 'new_m', 'new_m_q_norm_g': 'new_m', 'new_m_kv_norm_g': 'new_m', 'new_m_w_q_up': 'new_m', 'new_m_w_kv_up': 'new_m', 'new_m_q_head_g': 'new_m', 'new_m_k_head_g': 'new_m', 'new_m_w_out_mla': 'new_m', 'new_m_w_o': 'new_m', 'new_m_ffn_norm_g': 'new_m', 'new_m_w_ff1': 'new_m', 'new_m_w_ff2': 'new_m', 'new_v_mix_norm_g': 'new_v', 'new_v_w_in': 'new_v', 'new_v_b_gate': 'new_v', 'new_v_ssm_lam_re': 'new_v', 'new_v_ssm_lam_im': 'new_v', 'new_v_ssm_log_step': 'new_v', 'new_v_ssm_b_re': 'new_v', 'new_v_ssm_b_im': 'new_v', 'new_v_ssm_c_re': 'new_v', 'new_v_ssm_c_im': 'new_v', 'new_v_ssm_d': 'new_v', 'new_v_w_glu': 'new_v', 'new_v_b_glu': 'new_v', 'new_v_w_out_ssm': 'new_v', 'new_v_q_norm_g': 'new_v', 'new_v_kv_norm_g': 'new_v', 'new_v_w_q_up': 'new_v', 'new_v_w_kv_up': 'new_v', 'new_v_q_head_g': 'new_v', 'new_v_k_head_g': 'new_v', 'new_v_w_out_mla': 'new_v', 'new_v_w_o': 'new_v', 'new_v_ffn_norm_g': 'new_v', 'new_v_w_ff1': 'new_v', 'new_v_w_ff2': 'new_v'}


def _forward(args):
    return _fwd_reference(*[args[k] for k in FWD_PARAMS])


def _output_shape():
    def fwd():
        inp = _fwd_setup_inputs(0)
        return _fwd_reference(*[inp[k] for k in FWD_PARAMS])
    out = _jax.eval_shape(fwd)
    return out.shape, out.dtype

N_MICROBATCH = 1
ADAM_LR = 0.001
ADAM_B1 = 0.9
ADAM_B2 = 0.999
ADAM_EPS = 1e-08
ADAM_WD = 0.01
ADAM_STEP = 10
PER_EXAMPLE_BATCH_AXIS = {'x': 0, 'loss_target': 0}
SHARED_INPUTS = []
_WEIGHT_DTYPES = {'mix_norm_g': _jnp.float32, 'w_in': _jnp.float32, 'b_gate': _jnp.float32, 'ssm_lam_re': _jnp.float32, 'ssm_lam_im': _jnp.float32, 'ssm_log_step': _jnp.float32, 'ssm_b_re': _jnp.float32, 'ssm_b_im': _jnp.float32, 'ssm_c_re': _jnp.float32, 'ssm_c_im': _jnp.float32, 'ssm_d': _jnp.float32, 'w_glu': _jnp.float32, 'b_glu': _jnp.float32, 'w_out_ssm': _jnp.float32, 'q_norm_g': _jnp.float32, 'kv_norm_g': _jnp.float32, 'w_q_up': _jnp.float32, 'w_kv_up': _jnp.float32, 'q_head_g': _jnp.float32, 'k_head_g': _jnp.float32, 'w_out_mla': _jnp.float32, 'w_o': _jnp.float32, 'ffn_norm_g': _jnp.float32, 'w_ff1': _jnp.float32, 'w_ff2': _jnp.float32}
MOMENT_SCALE = {'mix_norm_g': 2.424701e+01, 'w_in': 1.332699e+01, 'b_gate': 7.029645e+00, 'ssm_lam_re': 4.451553e-01, 'ssm_lam_im': 6.042963e-01, 'ssm_log_step': 1.866442e+01, 'ssm_b_re': 4.471286e-01, 'ssm_b_im': 3.916870e-01, 'ssm_c_re': 1.401858e+00, 'ssm_c_im': 1.135292e+00, 'ssm_d': 1.992611e+01, 'w_glu': 3.268822e+00, 'b_glu': 8.880802e+00, 'w_out_ssm': 1.278841e+01, 'q_norm_g': 1.495418e+00, 'kv_norm_g': 4.805133e+01, 'w_q_up': 9.424534e-01, 'w_kv_up': 2.074912e+01, 'q_head_g': 2.740745e+00, 'k_head_g': 2.670480e+00, 'w_out_mla': 2.077197e+01, 'w_o': 2.373399e+01, 'ffn_norm_g': 2.046051e+02, 'w_ff1': 1.982888e+01, 'w_ff2': 6.562244e+01}


def _to_microbatches(a, axis):
    t = _jnp.moveaxis(a, axis, 0)
    t = t.reshape((N_MICROBATCH, t.shape[0] // N_MICROBATCH) + t.shape[1:])
    return _jnp.moveaxis(t, 1, axis + 1)


def setup_inputs(seed: int = 0) -> dict:
    inp = _fwd_setup_inputs(seed)
    key = _jax.random.fold_in(_jax.random.key(seed), 7919)
    shape, _ = _output_shape()
    out = dict(inp)
    out["loss_target"] = _jax.random.normal(_jax.random.fold_in(key, 0), shape, _jnp.float32)
    for i, name in enumerate(TWIN_WEIGHTS):
        w = inp[name].astype(_jnp.float32)
        if MOMENT_SCALE is None:
            s = _jnp.sqrt(_jnp.mean(_jnp.square(w)) + 1e-30)
        else:
            s = MOMENT_SCALE[name]
        km, kv = _jax.random.split(_jax.random.fold_in(key, i + 1))
        out[name] = w
        out["m_" + name] = s * _jax.random.normal(km, w.shape, _jnp.float32)
        out["v_" + name] = (s * s) * _jax.random.uniform(kv, w.shape, _jnp.float32, 0.5, 1.5)
    if N_MICROBATCH > 1:
        for name, axis in PER_EXAMPLE_BATCH_AXIS.items():
            out[name] = _to_microbatches(out[name], axis)
    return {'x': out['x'], 'mix_norm_g': out['mix_norm_g'], 'w_in': out['w_in'], 'b_gate': out['b_gate'], 'ssm_lam_re': out['ssm_lam_re'], 'ssm_lam_im': out['ssm_lam_im'], 'ssm_log_step': out['ssm_log_step'], 'ssm_b_re': out['ssm_b_re'], 'ssm_b_im': out['ssm_b_im'], 'ssm_c_re': out['ssm_c_re'], 'ssm_c_im': out['ssm_c_im'], 'ssm_d': out['ssm_d'], 'w_glu': out['w_glu'], 'b_glu': out['b_glu'], 'w_out_ssm': out['w_out_ssm'], 'q_norm_g': out['q_norm_g'], 'kv_norm_g': out['kv_norm_g'], 'w_q_up': out['w_q_up'], 'w_kv_up': out['w_kv_up'], 'q_head_g': out['q_head_g'], 'k_head_g': out['k_head_g'], 'w_out_mla': out['w_out_mla'], 'w_o': out['w_o'], 'ffn_norm_g': out['ffn_norm_g'], 'w_ff1': out['w_ff1'], 'w_ff2': out['w_ff2'], 'loss_target': out['loss_target'], 'm_mix_norm_g': out['m_mix_norm_g'], 'm_w_in': out['m_w_in'], 'm_b_gate': out['m_b_gate'], 'm_ssm_lam_re': out['m_ssm_lam_re'], 'm_ssm_lam_im': out['m_ssm_lam_im'], 'm_ssm_log_step': out['m_ssm_log_step'], 'm_ssm_b_re': out['m_ssm_b_re'], 'm_ssm_b_im': out['m_ssm_b_im'], 'm_ssm_c_re': out['m_ssm_c_re'], 'm_ssm_c_im': out['m_ssm_c_im'], 'm_ssm_d': out['m_ssm_d'], 'm_w_glu': out['m_w_glu'], 'm_b_glu': out['m_b_glu'], 'm_w_out_ssm': out['m_w_out_ssm'], 'm_q_norm_g': out['m_q_norm_g'], 'm_kv_norm_g': out['m_kv_norm_g'], 'm_w_q_up': out['m_w_q_up'], 'm_w_kv_up': out['m_w_kv_up'], 'm_q_head_g': out['m_q_head_g'], 'm_k_head_g': out['m_k_head_g'], 'm_w_out_mla': out['m_w_out_mla'], 'm_w_o': out['m_w_o'], 'm_ffn_norm_g': out['m_ffn_norm_g'], 'm_w_ff1': out['m_w_ff1'], 'm_w_ff2': out['m_w_ff2'], 'v_mix_norm_g': out['v_mix_norm_g'], 'v_w_in': out['v_w_in'], 'v_b_gate': out['v_b_gate'], 'v_ssm_lam_re': out['v_ssm_lam_re'], 'v_ssm_lam_im': out['v_ssm_lam_im'], 'v_ssm_log_step': out['v_ssm_log_step'], 'v_ssm_b_re': out['v_ssm_b_re'], 'v_ssm_b_im': out['v_ssm_b_im'], 'v_ssm_c_re': out['v_ssm_c_re'], 'v_ssm_c_im': out['v_ssm_c_im'], 'v_ssm_d': out['v_ssm_d'], 'v_w_glu': out['v_w_glu'], 'v_b_glu': out['v_b_glu'], 'v_w_out_ssm': out['v_w_out_ssm'], 'v_q_norm_g': out['v_q_norm_g'], 'v_kv_norm_g': out['v_kv_norm_g'], 'v_w_q_up': out['v_w_q_up'], 'v_w_kv_up': out['v_w_kv_up'], 'v_q_head_g': out['v_q_head_g'], 'v_k_head_g': out['v_k_head_g'], 'v_w_out_mla': out['v_w_out_mla'], 'v_w_o': out['v_w_o'], 'v_ffn_norm_g': out['v_ffn_norm_g'], 'v_w_ff1': out['v_w_ff1'], 'v_w_ff2': out['v_w_ff2']}


def _loss(weights, diff, rest, loss_target):
    with _jax.named_scope("forward"):
        args = {**rest, TWIN_DIFF_INPUT: diff, **{k: w.astype(_WEIGHT_DTYPES[k]) for k, w in weights.items()}}
        y = _forward(args)
    with _jax.named_scope("loss_head"):
        err = _jnp.square(y.astype(_jnp.float32) - loss_target)
        return 0.5 * _jnp.sum(_jnp.mean(err, axis=-1)) if err.ndim else 0.5 * err


def _adamw(w, g, m, v):
    m = ADAM_B1 * m + (1.0 - ADAM_B1) * g
    v = ADAM_B2 * v + (1.0 - ADAM_B2) * _jnp.square(g)
    m_hat = m / (1.0 - ADAM_B1 ** ADAM_STEP)
    v_hat = v / (1.0 - ADAM_B2 ** ADAM_STEP)
    delta = -ADAM_LR * (m_hat / (_jnp.sqrt(v_hat) + ADAM_EPS) + ADAM_WD * w)
    return delta, m, v


def reference(x, mix_norm_g, w_in, b_gate, ssm_lam_re, ssm_lam_im, ssm_log_step, ssm_b_re, ssm_b_im, ssm_c_re, ssm_c_im, ssm_d, w_glu, b_glu, w_out_ssm, q_norm_g, kv_norm_g, w_q_up, w_kv_up, q_head_g, k_head_g, w_out_mla, w_o, ffn_norm_g, w_ff1, w_ff2, loss_target, m_mix_norm_g, m_w_in, m_b_gate, m_ssm_lam_re, m_ssm_lam_im, m_ssm_log_step, m_ssm_b_re, m_ssm_b_im, m_ssm_c_re, m_ssm_c_im, m_ssm_d, m_w_glu, m_b_glu, m_w_out_ssm, m_q_norm_g, m_kv_norm_g, m_w_q_up, m_w_kv_up, m_q_head_g, m_k_head_g, m_w_out_mla, m_w_o, m_ffn_norm_g, m_w_ff1, m_w_ff2, v_mix_norm_g, v_w_in, v_b_gate, v_ssm_lam_re, v_ssm_lam_im, v_ssm_log_step, v_ssm_b_re, v_ssm_b_im, v_ssm_c_re, v_ssm_c_im, v_ssm_d, v_w_glu, v_b_glu, v_w_out_ssm, v_q_norm_g, v_kv_norm_g, v_w_q_up, v_w_kv_up, v_q_head_g, v_k_head_g, v_w_out_mla, v_w_o, v_ffn_norm_g, v_w_ff1, v_w_ff2):
    given = dict(x=x, mix_norm_g=mix_norm_g, w_in=w_in, b_gate=b_gate, ssm_lam_re=ssm_lam_re, ssm_lam_im=ssm_lam_im, ssm_log_step=ssm_log_step, ssm_b_re=ssm_b_re, ssm_b_im=ssm_b_im, ssm_c_re=ssm_c_re, ssm_c_im=ssm_c_im, ssm_d=ssm_d, w_glu=w_glu, b_glu=b_glu, w_out_ssm=w_out_ssm, q_norm_g=q_norm_g, kv_norm_g=kv_norm_g, w_q_up=w_q_up, w_kv_up=w_kv_up, q_head_g=q_head_g, k_head_g=k_head_g, w_out_mla=w_out_mla, w_o=w_o, ffn_norm_g=ffn_norm_g, w_ff1=w_ff1, w_ff2=w_ff2, loss_target=loss_target, m_mix_norm_g=m_mix_norm_g, m_w_in=m_w_in, m_b_gate=m_b_gate, m_ssm_lam_re=m_ssm_lam_re, m_ssm_lam_im=m_ssm_lam_im, m_ssm_log_step=m_ssm_log_step, m_ssm_b_re=m_ssm_b_re, m_ssm_b_im=m_ssm_b_im, m_ssm_c_re=m_ssm_c_re, m_ssm_c_im=m_ssm_c_im, m_ssm_d=m_ssm_d, m_w_glu=m_w_glu, m_b_glu=m_b_glu, m_w_out_ssm=m_w_out_ssm, m_q_norm_g=m_q_norm_g, m_kv_norm_g=m_kv_norm_g, m_w_q_up=m_w_q_up, m_w_kv_up=m_w_kv_up, m_q_head_g=m_q_head_g, m_k_head_g=m_k_head_g, m_w_out_mla=m_w_out_mla, m_w_o=m_w_o, m_ffn_norm_g=m_ffn_norm_g, m_w_ff1=m_w_ff1, m_w_ff2=m_w_ff2, v_mix_norm_g=v_mix_norm_g, v_w_in=v_w_in, v_b_gate=v_b_gate, v_ssm_lam_re=v_ssm_lam_re, v_ssm_lam_im=v_ssm_lam_im, v_ssm_log_step=v_ssm_log_step, v_ssm_b_re=v_ssm_b_re, v_ssm_b_im=v_ssm_b_im, v_ssm_c_re=v_ssm_c_re, v_ssm_c_im=v_ssm_c_im, v_ssm_d=v_ssm_d, v_w_glu=v_w_glu, v_b_glu=v_b_glu, v_w_out_ssm=v_w_out_ssm, v_q_norm_g=v_q_norm_g, v_kv_norm_g=v_kv_norm_g, v_w_q_up=v_w_q_up, v_w_kv_up=v_w_kv_up, v_q_head_g=v_q_head_g, v_k_head_g=v_k_head_g, v_w_out_mla=v_w_out_mla, v_w_o=v_w_o, v_ffn_norm_g=v_ffn_norm_g, v_w_ff1=v_w_ff1, v_w_ff2=v_w_ff2)
    weights = {n: given[n] for n in TWIN_WEIGHTS}
    shared = {n: given[n] for n in SHARED_INPUTS}
    per_example = {n: given[n] for n in ['x']}
    grad_fn = _jax.value_and_grad(_loss, argnums=(0, 1))

    def one_microbatch(ex, loss_target):
        ex = dict(ex)
        diff = ex.pop(TWIN_DIFF_INPUT)
        return grad_fn(weights, diff, {**shared, **ex}, loss_target)

    if N_MICROBATCH == 1:
        loss, (grad_w, grad_x) = one_microbatch(per_example, given["loss_target"])
    else:
        def body(carry, xs):
            loss_sum, grad_sum = carry
            l_k, (gw_k, gx_k) = one_microbatch(xs[0], xs[1])
            with _jax.named_scope("update"):
                return (loss_sum + l_k, _jax.tree.map(_jnp.add, grad_sum, gw_k)), gx_k

        init = (_jnp.zeros((), _jnp.float32), _jax.tree.map(_jnp.zeros_like, weights))
        (loss, grad_w), grad_x = _jax.lax.scan(body, init, (per_example, given["loss_target"]))
    with _jax.named_scope("update"):
        delta_w, new_m, new_v = {}, {}, {}
        for n in TWIN_WEIGHTS:
            delta_w[n], new_m[n], new_v[n] = _adamw(weights[n], grad_w[n], given["m_" + n], given["v_" + n])
    return (loss, grad_x, *[grad_w[n] for n in TWIN_WEIGHTS], *[delta_w[n] for n in TWIN_WEIGHTS],
            *[new_m[n] for n in TWIN_WEIGHTS], *[new_v[n] for n in TWIN_WEIGHTS])
```

```python
import functools
import math

import jax
import jax.numpy as jnp
from jax import lax
from jax.experimental import pallas as pl
from jax.experimental.pallas import tpu as pltpu

F32 = jnp.float32
BF16 = jnp.bfloat16
_MXU = jnp.bfloat16

D_MODEL = 1024
DEPTH = 4
SSM_WIDTH = 512
SSM_GROUP = 16
SSM_GROUPS = 32
SSM_STATE = 64
MLA_HEADS = 8
QK_NOPE = 64
QK_ROPE = 32
QK_HEAD = 96
V_HEAD = 64
Q_LORA = 384
KV_LORA = 256
ROPE_THETA = 10000.0
D_FF = 4096
EPS = 1e-6
HEAD_PAD = 128
N_DEV = 8
LANES = 128
SUBLANES = 8
CHUNK_GROUPS = 8
N_CHUNKS = SSM_GROUPS // CHUNK_GROUPS
CHUNK_STATE = CHUNK_GROUPS * SSM_STATE

P_GATE, P_U, P_CKV, P_KR, P_CQ = 0, 2048, 2560, 2816, 3072
P_COLS = 3456
IN_U, IN_CQ, IN_CKV, IN_KR, IN_GATE = 0, 512, 896, 1152, 1184
IN_COLS = 3232

ADAM_LR = 0.001
ADAM_B1 = 0.9
ADAM_B2 = 0.999
ADAM_EPS = 1e-08
ADAM_WD = 0.01
ADAM_STEP = 10

VMEM_LIMIT = 56 * 1024 * 1024

SHARDED = ("w_in", "b_gate", "w_glu", "w_out_ssm", "w_q_up", "w_kv_up", "w_out_mla", "w_o", "w_ff1", "w_ff2")
SHARD_AXIS = {"w_in": 2, "b_gate": 2, "w_glu": 1, "w_out_ssm": 2, "w_q_up": 2, "w_kv_up": 2, "w_out_mla": 2,
              "w_o": 1, "w_ff1": 2, "w_ff2": 1}
REPLICATED = ("mix_norm_g", "ssm_lam_re", "ssm_lam_im", "ssm_log_step", "ssm_b_re", "ssm_b_im", "ssm_c_re",
              "ssm_c_im", "ssm_d", "b_glu", "q_norm_g", "kv_norm_g", "q_head_g", "k_head_g", "ffn_norm_g")
WEIGHTS = ("mix_norm_g", "w_in", "b_gate", "ssm_lam_re", "ssm_lam_im", "ssm_log_step", "ssm_b_re", "ssm_b_im",
           "ssm_c_re", "ssm_c_im", "ssm_d", "w_glu", "b_glu", "w_out_ssm", "q_norm_g", "kv_norm_g", "w_q_up",
           "w_kv_up", "q_head_g", "k_head_g", "w_out_mla", "w_o", "ffn_norm_g", "w_ff1", "w_ff2")


def _cparams(sem):
    return pltpu.CompilerParams(dimension_semantics=sem, vmem_limit_bytes=VMEM_LIMIT)


def _dot(a, b, dims):
    return lax.dot_general(a.astype(_MXU), b.astype(_MXU), (dims, ((), ())), preferred_element_type=F32)


def _dot_nn(a, b):
    return _dot(a, b, ((1,), (0,)))


def _dot_nt(a, b):
    return _dot(a, b, ((1,), (1,)))


def _dot_tn(a, b):
    return _dot(a, b, ((0,), (0,)))


def _rowwise(fn, rows, consts, outs, accs=(), *, tm=512, name):
    n_rows = rows[0][0].shape[0]
    tm = min(tm, n_rows)
    n_in = len(rows) + len(consts)
    n_o, n_a = len(outs), len(accs)

    def body(*refs):
        res = fn(*[r[...] for r in refs[:n_in]])
        if not isinstance(res, (tuple, list)):
            res = (res,)
        orefs = refs[n_in:]
        for k in range(n_o):
            orefs[k][...] = res[k].astype(orefs[k].dtype)
        if n_a:
            @pl.when(pl.program_id(0) == 0)
            def _():
                for k in range(n_a):
                    orefs[n_o + k][...] = jnp.zeros_like(orefs[n_o + k])
            for k in range(n_a):
                orefs[n_o + k][...] += res[n_o + k]

    in_specs = [pl.BlockSpec((tm, w), functools.partial(lambda i, j: (i, j), j=j)) for (_, w, j) in rows]
    in_specs += [pl.BlockSpec(c.shape, functools.partial(lambda i, nd: (0,) * nd, nd=c.ndim)) for c in consts]
    out_specs = [pl.BlockSpec((tm, w), lambda i: (i, 0)) for (w, _) in outs]
    out_specs += [pl.BlockSpec((1, w), lambda i: (0, 0)) for (w, _) in accs]
    out_shape = [jax.ShapeDtypeStruct((n_rows, w), dt) for (w, dt) in outs]
    out_shape += [jax.ShapeDtypeStruct((1, w), dt) for (w, dt) in accs]
    res = pl.pallas_call(
        body, grid=(n_rows // tm,), in_specs=in_specs, out_specs=out_specs, out_shape=out_shape, name=name,
        compiler_params=_cparams(("arbitrary",) if n_a else ("parallel",)),
    )(*[r[0] for r in rows], *consts)
    return res


def _pick(n, cap):
    if n <= cap:
        return n
    best = LANES
    for t in range(LANES, cap + 1, LANES):
        if n % t == 0:
            best = t
    return best


def _mm_nn(a, b, *, add=None, out_dtype=F32, a_cols=None, name):
    m = a.shape[0]
    k, n = b.shape
    aw, aj = (k, 0) if a_cols is None else a_cols
    tm, tn = min(512, m), _pick(n, 1024)

    def body(*refs):
        acc = _dot_nn(refs[0][...], refs[1][...])
        if add is not None:
            acc = acc + refs[2][...]
        refs[-1][...] = acc.astype(out_dtype)

    in_specs = [pl.BlockSpec((tm, aw), lambda j, i: (i, aj)), pl.BlockSpec((k, tn), lambda j, i: (0, j))]
    args = [a, b]
    if add is not None:
        in_specs.append(pl.BlockSpec((tm, tn), lambda j, i: (i, j)))
        args.append(add)
    return pl.pallas_call(
        body, grid=(n // tn, m // tm), in_specs=in_specs, out_specs=pl.BlockSpec((tm, tn), lambda j, i: (i, j)),
        out_shape=jax.ShapeDtypeStruct((m, n), out_dtype), name=name, compiler_params=_cparams(("parallel", "parallel")),
    )(*args)


def _mm_nt(a, b, *, out_dtype=F32, name):
    m, k = a.shape
    n = b.shape[0]
    tm, tn = min(512, m), _pick(n, 1024)

    def body(a_ref, b_ref, o_ref):
        o_ref[...] = _dot_nt(a_ref[...], b_ref[...]).astype(out_dtype)

    return pl.pallas_call(
        body, grid=(n // tn, m // tm),
        in_specs=[pl.BlockSpec((tm, k), lambda j, i: (i, 0)), pl.BlockSpec((tn, k), lambda j, i: (j, 0))],
        out_specs=pl.BlockSpec((tm, tn), lambda j, i: (i, j)),
        out_shape=jax.ShapeDtypeStruct((m, n), out_dtype), name=name, compiler_params=_cparams(("parallel", "parallel")),
    )(a, b)


def _mm_tn(a, b, *, a_cols=None, name):
    s = a.shape[0]
    n = b.shape[1]
    mw, mj = (a.shape[1], 0) if a_cols is None else a_cols
    ts = min(512, s)
    tm, tn = _pick(mw, 1024), _pick(n, 1024)
    n_mb = mw // tm

    def body(a_ref, b_ref, o_ref):
        @pl.when(pl.program_id(2) == 0)
        def _():
            o_ref[...] = jnp.zeros_like(o_ref)
        o_ref[...] += _dot_tn(a_ref[...], b_ref[...])

    return pl.pallas_call(
        body, grid=(n_mb, n // tn, s // ts),
        in_specs=[pl.BlockSpec((ts, tm), lambda i, j, t: (t, mj * n_mb + i)), pl.BlockSpec((ts, tn), lambda i, j, t: (t, j))],
        out_specs=pl.BlockSpec((tm, tn), lambda i, j, t: (i, j)),
        out_shape=jax.ShapeDtypeStruct((mw, n), F32), name=name,
        compiler_params=_cparams(("parallel", "parallel", "arbitrary")),
    )(a, b)


def _rms(x, g, n):
    r = lax.rsqrt(jnp.sum(x * x, axis=-1, keepdims=True) * (1.0 / n) + EPS)
    return x * r * g


def _rms_bwd(x, g, dy, n):
    r = lax.rsqrt(jnp.sum(x * x, axis=-1, keepdims=True) * (1.0 / n) + EPS)
    xr = x * r
    dyg = dy * g
    dx = r * dyg - xr * (r * r) * (jnp.sum(dyg * x, axis=-1, keepdims=True) * (1.0 / n))
    return dx, jnp.sum(dy * xr, axis=0, keepdims=True)


def _gelu(x):
    c = math.sqrt(2.0 / math.pi)
    return 0.5 * x * (1.0 + jnp.tanh(c * (x + 0.044715 * (x * x * x))))


def _gelu_grad(x):
    c = math.sqrt(2.0 / math.pi)
    t = jnp.tanh(c * (x + 0.044715 * (x * x * x)))
    return 0.5 * (1.0 + t) + 0.5 * x * (1.0 - t * t) * (c * (1.0 + 3.0 * 0.044715 * (x * x)))


def _sigmoid(x):
    return 1.0 / (1.0 + jnp.exp(-x))


def _rope(x, cf, sa, sb):
    return x * cf + pltpu.roll(x, HEAD_PAD - QK_ROPE // 2, 1) * sa + pltpu.roll(x, QK_ROPE // 2, 1) * sb


def _rope_t(d, cf, sa, sb):
    return d * cf + pltpu.roll(d * sa, QK_ROPE // 2, 1) + pltpu.roll(d * sb, HEAD_PAD - QK_ROPE // 2, 1)


def _scan_tables(ar, ai, reverse):
    ar = ar.reshape(N_CHUNKS, CHUNK_STATE)
    ai = ai.reshape(N_CHUNKS, CHUNK_STATE)
    pr, pi = [ar], [ai]
    for _ in range(SUBLANES - 1):
        pr, pi = pr + [pr[-1] * ar - pi[-1] * ai], pi + [pr[-1] * ai + pi[-1] * ar]
    row = jnp.arange(SUBLANES)[None, :, None]
    tiles = []
    for k in (1, 2, 4):
        mask = (row <= SUBLANES - 1 - k) if reverse else (row >= k)
        tiles.append(jnp.where(mask, pr[k - 1][:, None, :], 0.0))
        tiles.append(jnp.where(mask, pi[k - 1][:, None, :], 0.0))
    order = list(range(SUBLANES))[::-1] if reverse else list(range(SUBLANES))
    tiles.append(jnp.stack([pr[j] for j in order], axis=1))
    tiles.append(jnp.stack([pi[j] for j in order], axis=1))
    return jnp.stack(tiles, axis=1).astype(F32)


def _slab_scan(xr, xi, coef, carry_r, carry_i, reverse):
    for idx, k in enumerate((1, 2, 4)):
        sh = SUBLANES - k if reverse else k
        sr, si = pltpu.roll(xr, sh, 0), pltpu.roll(xi, sh, 0)
        cr, ci = coef[2 * idx], coef[2 * idx + 1]
        xr, xi = xr + cr * sr - ci * si, xi + cr * si + ci * sr
    pr, pi = coef[6], coef[7]
    xr = xr + pr * carry_r - pi * carry_i
    xi = xi + pr * carry_i + pi * carry_r
    return xr, xi


def _s5_scan_fwd(proj, b_blk, c_blk, coef, reverse, name):
    s = proj.shape[0]
    t_blk = min(512, s)
    n_t = s // t_blk
    n_slab = t_blk // SUBLANES
    last = 0 if reverse else SUBLANES - 1

    def tmap(t):
        return n_t - 1 - t if reverse else t

    def body(u_ref, b_ref, c_ref, coef_ref, y_ref, xr_ref, xi_ref, carry_ref):
        @pl.when(pl.program_id(1) == 0)
        def _():
            carry_ref[...] = jnp.zeros_like(carry_ref)
        bu = _dot_nn(u_ref[...], b_ref[0])
        xr_ref[...] = bu[:, :CHUNK_STATE]
        xi_ref[...] = bu[:, CHUNK_STATE:]
        coef_v = [coef_ref[0, k] for k in range(8)]

        def slab(i, carry):
            sl = (n_slab - 1 - i) if reverse else i
            rows = pl.ds(pl.multiple_of(sl * SUBLANES, SUBLANES), SUBLANES)
            xr, xi = _slab_scan(xr_ref[rows, :], xi_ref[rows, :], coef_v, carry[0], carry[1], reverse)
            xr_ref[rows, :] = xr
            xi_ref[rows, :] = xi
            return (jnp.broadcast_to(xr[last:last + 1, :], xr.shape), jnp.broadcast_to(xi[last:last + 1, :], xi.shape))

        cr, ci = lax.fori_loop(0, n_slab, slab, (carry_ref[0], carry_ref[1]))
        carry_ref[0] = cr
        carry_ref[1] = ci
        y_ref[...] = _dot_nn(xr_ref[...], c_ref[0, :CHUNK_STATE, :]) + _dot_nn(xi_ref[...], c_ref[0, CHUNK_STATE:, :])

    u_blk0 = P_U // LANES
    return pl.pallas_call(
        body, grid=(N_CHUNKS, n_t),
        in_specs=[pl.BlockSpec((t_blk, LANES), lambda c, t: (tmap(t), u_blk0 + c)),
                  pl.BlockSpec((1, LANES, 2 * CHUNK_STATE), lambda c, t: (c, 0, 0)),
                  pl.BlockSpec((1, 2 * CHUNK_STATE, LANES), lambda c, t: (c, 0, 0)),
                  pl.BlockSpec((1, 8, SUBLANES, CHUNK_STATE), lambda c, t: (c, 0, 0, 0))],
        out_specs=[pl.BlockSpec((t_blk, LANES), lambda c, t: (tmap(t), c)),
                   pl.BlockSpec((t_blk, CHUNK_STATE), lambda c, t: (tmap(t), c)),
                   pl.BlockSpec((t_blk, CHUNK_STATE), lambda c, t: (tmap(t), c))],
        out_shape=[jax.ShapeDtypeStruct((s, SSM_WIDTH), F32),
                   jax.ShapeDtypeStruct((s, N_CHUNKS * CHUNK_STATE), F32),
                   jax.ShapeDtypeStruct((s, N_CHUNKS * CHUNK_STATE), F32)],
        scratch_shapes=[pltpu.VMEM((2, SUBLANES, CHUNK_STATE), F32)],
        name=name, compiler_params=_cparams(("parallel", "arbitrary")),
    )(proj, b_blk, c_blk, coef)


def _s5_scan_bwd(dy, proj, x_re, x_im, b_blk, c_blk, coef, reverse, name):
    s = dy.shape[0]
    t_blk = min(512, s)
    n_t = s // t_blk
    n_slab = t_blk // SUBLANES
    last = 0 if reverse else SUBLANES - 1
    first = SUBLANES - 1 if reverse else 0

    def tmap(t):
        return n_t - 1 - t if reverse else t

    def body(dy_ref, u_ref, xr_ref, xi_ref, b_ref, c_ref, coef_ref, du_ref, da_ref, db_ref, dc_ref,
             carry_ref, lr_ref, li_ref):
        @pl.when(pl.program_id(1) == 0)
        def _():
            carry_ref[...] = jnp.zeros_like(carry_ref)
            da_ref[...] = jnp.zeros_like(da_ref)
            db_ref[...] = jnp.zeros_like(db_ref)
            dc_ref[...] = jnp.zeros_like(dc_ref)
        g = _dot_nt(dy_ref[...], c_ref[0])
        lr_ref[...] = g[:, :CHUNK_STATE]
        li_ref[...] = g[:, CHUNK_STATE:]
        coef_v = [coef_ref[0, k] for k in range(8)]
        row = lax.broadcasted_iota(jnp.int32, (SUBLANES, CHUNK_STATE), 0)
        sh_prev = SUBLANES - 1 if reverse else 1

        def slab(i, carry):
            cr, ci, ar_acc, ai_acc = carry
            sl = (n_slab - 1 - i) if reverse else i
            rows = pl.ds(pl.multiple_of(sl * SUBLANES, SUBLANES), SUBLANES)
            lr, li = _slab_scan(lr_ref[rows, :], li_ref[rows, :], coef_v, cr, ci, reverse)
            lr_ref[rows, :] = lr
            li_ref[rows, :] = li
            pr = jnp.where(row == first, cr, pltpu.roll(lr, sh_prev, 0))
            pi = jnp.where(row == first, ci, pltpu.roll(li, sh_prev, 0))
            xr, xi = xr_ref[rows, :], xi_ref[rows, :]
            ar_acc = ar_acc + xr * pr + xi * pi
            ai_acc = ai_acc + xr * pi - xi * pr
            return (jnp.broadcast_to(lr[last:last + 1, :], lr.shape), jnp.broadcast_to(li[last:last + 1, :], li.shape),
                    ar_acc, ai_acc)

        zero = jnp.zeros((SUBLANES, CHUNK_STATE), F32)
        cr, ci, ar_acc, ai_acc = lax.fori_loop(0, n_slab, slab, (carry_ref[0], carry_ref[1], zero, zero))
        carry_ref[0] = cr
        carry_ref[1] = ci
        da_ref[0, :, :CHUNK_STATE] += ar_acc
        da_ref[0, :, CHUNK_STATE:] += ai_acc
        lam_r, lam_i = lr_ref[...], li_ref[...]
        u = u_ref[...]
        du_ref[...] = _dot_nt(lam_r, b_ref[0, :, :CHUNK_STATE]) + _dot_nt(lam_i, b_ref[0, :, CHUNK_STATE:])
        db_ref[0, :, :CHUNK_STATE] += _dot_tn(u, lam_r)
        db_ref[0, :, CHUNK_STATE:] += _dot_tn(u, lam_i)
        dyv = dy_ref[...]
        dc_ref[0, :CHUNK_STATE, :] += _dot_tn(xr_ref[...], dyv)
        dc_ref[0, CHUNK_STATE:, :] += _dot_tn(xi_ref[...], dyv)

    u_blk0 = P_U // LANES
    return pl.pallas_call(
        body, grid=(N_CHUNKS, n_t),
        in_specs=[pl.BlockSpec((t_blk, LANES), lambda c, t: (tmap(t), c)),
                  pl.BlockSpec((t_blk, LANES), lambda c, t: (tmap(t), u_blk0 + c)),
                  pl.BlockSpec((t_blk, CHUNK_STATE), lambda c, t: (tmap(t), c)),
                  pl.BlockSpec((t_blk, CHUNK_STATE), lambda c, t: (tmap(t), c)),
                  pl.BlockSpec((1, LANES, 2 * CHUNK_STATE), lambda c, t: (c, 0, 0)),
                  pl.BlockSpec((1, 2 * CHUNK_STATE, LANES), lambda c, t: (c, 0, 0)),
                  pl.BlockSpec((1, 8, SUBLANES, CHUNK_STATE), lambda c, t: (c, 0, 0, 0))],
        out_specs=[pl.BlockSpec((t_blk, LANES), lambda c, t: (tmap(t), c)),
                   pl.BlockSpec((1, SUBLANES, 2 * CHUNK_STATE), lambda c, t: (c, 0, 0)),
                   pl.BlockSpec((1, LANES, 2 * CHUNK_STATE), lambda c, t: (c, 0, 0)),
                   pl.BlockSpec((1, 2 * CHUNK_STATE, LANES), lambda c, t: (c, 0, 0))],
        out_shape=[jax.ShapeDtypeStruct((s, SSM_WIDTH), F32),
                   jax.ShapeDtypeStruct((N_CHUNKS, SUBLANES, 2 * CHUNK_STATE), F32),
                   jax.ShapeDtypeStruct((N_CHUNKS, LANES, 2 * CHUNK_STATE), F32),
                   jax.ShapeDtypeStruct((N_CHUNKS, 2 * CHUNK_STATE, LANES), F32)],
        scratch_shapes=[pltpu.VMEM((2, SUBLANES, CHUNK_STATE), F32), pltpu.VMEM((t_blk, CHUNK_STATE), F32),
                        pltpu.VMEM((t_blk, CHUNK_STATE), F32)],
        name=name, compiler_params=_cparams(("parallel", "arbitrary")),
    )(dy, proj, x_re, x_im, b_blk, c_blk, coef)


def _zoh(lam_re, lam_im, log_step, b_re, b_im):
    step = jnp.exp(log_step)[:, None]
    mag = jnp.exp(lam_re * step)
    abar_r = mag * jnp.cos(lam_im * step)
    abar_i = mag * jnp.sin(lam_im * step)
    nr = abar_r - 1.0
    ni = abar_i
    den = lam_re * lam_re + lam_im * lam_im
    fr = (nr * lam_re + ni * lam_im) / den
    fi = (ni * lam_re - nr * lam_im) / den
    bbar_r = fr[..., None] * b_re - fi[..., None] * b_im
    bbar_i = fr[..., None] * b_im + fi[..., None] * b_re
    return abar_r, abar_i, bbar_r, bbar_i


def _b_block(bbar_r, bbar_i):
    eye = jnp.eye(CHUNK_GROUPS, dtype=F32)

    def one(b):
        b = b.reshape(N_CHUNKS, CHUNK_GROUPS, SSM_STATE, SSM_GROUP)
        return jnp.einsum("cgnp,gh->cgphn", b, eye).reshape(N_CHUNKS, LANES, CHUNK_STATE)

    return jnp.concatenate([one(bbar_r), one(bbar_i)], axis=2)


def _b_unblock(db):
    eye = jnp.eye(CHUNK_GROUPS, dtype=F32)

    def one(d):
        d = d.reshape(N_CHUNKS, CHUNK_GROUPS, SSM_GROUP, CHUNK_GROUPS, SSM_STATE)
        return jnp.einsum("cgphn,gh->cgnp", d, eye).reshape(SSM_GROUPS, SSM_STATE, SSM_GROUP)

    return one(db[:, :, :CHUNK_STATE]), one(db[:, :, CHUNK_STATE:])


def _c_block(c_re, c_im):
    eye = jnp.eye(CHUNK_GROUPS, dtype=F32)

    def one(c):
        c = c.reshape(N_CHUNKS, CHUNK_GROUPS, SSM_GROUP, SSM_STATE)
        return jnp.einsum("cgpn,gh->cgnhp", c, eye).reshape(N_CHUNKS, CHUNK_STATE, LANES)

    return jnp.concatenate([one(c_re), -one(c_im)], axis=1)


def _c_unblock(dc):
    eye = jnp.eye(CHUNK_GROUPS, dtype=F32)

    def one(d):
        d = d.reshape(N_CHUNKS, CHUNK_GROUPS, SSM_STATE, CHUNK_GROUPS, SSM_GROUP)
        return jnp.einsum("cgnhp,gh->cgpn", d, eye).reshape(SSM_GROUPS, SSM_GROUP, SSM_STATE)

    return one(dc[:, :CHUNK_STATE, :]), -one(dc[:, CHUNK_STATE:, :])


def _attn_fwd(q, k, v, name):
    s = q.shape[0]
    tq = min(1024, s)
    tk = min(1024, s)
    n_k = s // tk

    def body(q_ref, k_ref, v_ref, o_ref, lse_ref, m_ref, l_ref, acc_ref):
        m_ref[...] = jnp.full_like(m_ref, -jnp.inf)
        l_ref[...] = jnp.zeros_like(l_ref)
        acc_ref[...] = jnp.zeros_like(acc_ref)
        qv = q_ref[...]

        def step(j, _):
            rows = pl.ds(pl.multiple_of(j * tk, tk), tk)
            sc = _dot_nt(qv, k_ref[rows, :])
            m_old = m_ref[...]
            m_new = jnp.maximum(m_old, jnp.max(sc, axis=1, keepdims=True))
            p = jnp.exp(sc - m_new)
            alpha = jnp.exp(m_old - m_new)
            l_ref[...] = alpha * l_ref[...] + jnp.sum(p, axis=1, keepdims=True)
            acc_ref[...] = alpha * acc_ref[...] + _dot_nn(p, v_ref[rows, :])
            m_ref[...] = m_new
            return 0

        lax.fori_loop(0, n_k, step, 0)
        l = l_ref[...]
        o_ref[...] = acc_ref[...] / l
        lse = m_ref[...] + jnp.log(l)
        lse_ref[0] = jnp.broadcast_to(lse, (tq, LANES)).T[:SUBLANES, :]

    return pl.pallas_call(
        body, grid=(MLA_HEADS, s // tq),
        in_specs=[pl.BlockSpec((tq, HEAD_PAD), lambda h, i: (i, h)),
                  pl.BlockSpec((s, HEAD_PAD), lambda h, i: (0, h)),
                  pl.BlockSpec((s, HEAD_PAD), lambda h, i: (0, h))],
        out_specs=[pl.BlockSpec((tq, HEAD_PAD), lambda h, i: (i, h)),
                   pl.BlockSpec((1, SUBLANES, tq), lambda h, i: (h, 0, i))],
        out_shape=[jax.ShapeDtypeStruct((s, MLA_HEADS * HEAD_PAD), F32),
                   jax.ShapeDtypeStruct((MLA_HEADS, SUBLANES, s), F32)],
        scratch_shapes=[pltpu.VMEM((tq, 1), F32), pltpu.VMEM((tq, 1), F32), pltpu.VMEM((tq, HEAD_PAD), F32)],
        name=name, compiler_params=_cparams(("parallel", "parallel")),
    )(q, k, v)


def _attn_bwd(q, k, v, o, do, lse, name):
    s = q.shape[0]
    tq = min(1024, s)
    tk = min(1024, s)

    def body(q_ref, k_ref, v_ref, o_ref, do_ref, lse_ref, dq_ref, dk_ref, dv_ref):
        j, i = pl.program_id(1), pl.program_id(2)

        @pl.when(jnp.logical_and(j == 0, i == 0))
        def _():
            dq_ref[...] = jnp.zeros_like(dq_ref)

        @pl.when(i == 0)
        def _():
            dk_ref[...] = jnp.zeros_like(dk_ref)
            dv_ref[...] = jnp.zeros_like(dv_ref)

        qv, kv, vv, dov = q_ref[...], k_ref[...], v_ref[...], do_ref[...]
        delta_col = jnp.sum(dov * o_ref[...], axis=1, keepdims=True)
        delta = jnp.broadcast_to(delta_col, (tq, LANES)).T[:1, :]
        st = _dot_nt(kv, qv)
        pt = jnp.exp(st - lse_ref[0, :1, :])
        dv_ref[...] += _dot_nn(pt, dov)
        dpt = _dot_nt(vv, dov)
        dst = pt * (dpt - delta)
        dk_ref[...] += _dot_nn(dst, qv)
        rows = pl.ds(pl.multiple_of(i * tq, tq), tq)
        dq_ref[rows, :] += _dot_tn(dst, kv)

    return pl.pallas_call(
        body, grid=(MLA_HEADS, s // tk, s // tq),
        in_specs=[pl.BlockSpec((tq, HEAD_PAD), lambda h, j, i: (i, h)),
                  pl.BlockSpec((tk, HEAD_PAD), lambda h, j, i: (j, h)),
                  pl.BlockSpec((tk, HEAD_PAD), lambda h, j, i: (j, h)),
                  pl.BlockSpec((tq, HEAD_PAD), lambda h, j, i: (i, h)),
                  pl.BlockSpec((tq, HEAD_PAD), lambda h, j, i: (i, h)),
                  pl.BlockSpec((1, SUBLANES, tq), lambda h, j, i: (h, 0, i))],
        out_specs=[pl.BlockSpec((s, HEAD_PAD), lambda h, j, i: (0, h)),
                   pl.BlockSpec((tk, HEAD_PAD), lambda h, j, i: (j, h)),
                   pl.BlockSpec((tk, HEAD_PAD), lambda h, j, i: (j, h))],
        out_shape=[jax.ShapeDtypeStruct((s, MLA_HEADS * HEAD_PAD), F32)] * 3,
        name=name, compiler_params=_cparams(("parallel", "arbitrary", "arbitrary")),
    )(q, k, v, o, do, lse)


def _pad_w_in(w):
    z = functools.partial(jnp.zeros, dtype=w.dtype)
    return jnp.concatenate([
        w[:, IN_GATE:IN_COLS], w[:, IN_U:IN_CQ], w[:, IN_CKV:IN_KR],
        z((D_MODEL, QK_NOPE)), w[:, IN_KR:IN_GATE], z((D_MODEL, HEAD_PAD - QK_HEAD)),
        z((D_MODEL, P_CQ - P_KR - HEAD_PAD)), w[:, IN_CQ:IN_CKV]], axis=1)


def _unpad_w_in(d):
    return jnp.concatenate([d[:, P_U:P_CKV], d[:, P_CQ:P_COLS], d[:, P_CKV:P_KR],
                            d[:, P_KR + QK_NOPE:P_KR + QK_HEAD], d[:, P_GATE:P_U]], axis=1)


def _pad_heads_cols(w, real):
    k = w.shape[0]
    w = w.reshape(k, MLA_HEADS, real)
    return jnp.pad(w, ((0, 0), (0, 0), (0, HEAD_PAD - real))).reshape(k, MLA_HEADS * HEAD_PAD)


def _unpad_heads_cols(d, real):
    k = d.shape[0]
    return d.reshape(k, MLA_HEADS, HEAD_PAD)[:, :, :real].reshape(k, MLA_HEADS * real)


def _pad_kv(w):
    w = w.reshape(KV_LORA, MLA_HEADS, QK_NOPE + V_HEAD)
    kn = jnp.pad(w[:, :, :QK_NOPE], ((0, 0), (0, 0), (0, HEAD_PAD - QK_NOPE)))
    vv = jnp.pad(w[:, :, QK_NOPE:], ((0, 0), (0, 0), (0, HEAD_PAD - V_HEAD)))
    return jnp.concatenate([kn.reshape(KV_LORA, -1), vv.reshape(KV_LORA, -1)], axis=1)


def _unpad_kv(d):
    n = MLA_HEADS * HEAD_PAD
    kn = d[:, :n].reshape(KV_LORA, MLA_HEADS, HEAD_PAD)[:, :, :QK_NOPE]
    vv = d[:, n:].reshape(KV_LORA, MLA_HEADS, HEAD_PAD)[:, :, :V_HEAD]
    return jnp.concatenate([kn, vv], axis=2).reshape(KV_LORA, MLA_HEADS * (QK_NOPE + V_HEAD))


def _pad_out_mla(w):
    w = w.reshape(MLA_HEADS, V_HEAD, D_MODEL)
    return jnp.pad(w, ((0, 0), (0, HEAD_PAD - V_HEAD), (0, 0))).reshape(MLA_HEADS * HEAD_PAD, D_MODEL)


def _unpad_out_mla(d):
    return d.reshape(MLA_HEADS, HEAD_PAD, D_MODEL)[:, :V_HEAD, :].reshape(MLA_HEADS * V_HEAD, D_MODEL)


def _rope_tables(seq):
    half = QK_ROPE // 2
    inv_freq = ROPE_THETA ** (-jnp.arange(half, dtype=F32) / half)
    ang = jnp.arange(seq, dtype=F32)[:, None] * inv_freq[None, :]
    cos, sin = jnp.cos(ang), jnp.sin(ang)
    one, zero = jnp.ones((seq, QK_NOPE), F32), jnp.zeros((seq, half), F32)
    tail1, tail0 = jnp.ones((seq, HEAD_PAD - QK_HEAD), F32), jnp.zeros((seq, HEAD_PAD - QK_HEAD), F32)
    cf = jnp.concatenate([one, cos, cos, tail1], axis=1)
    sa = jnp.concatenate([0.0 * one, -sin, zero, tail0], axis=1)
    sb = jnp.concatenate([0.0 * one, zero, sin, tail0], axis=1)
    return cf, sa, sb


def _prep_layer(w):
    p = {}
    p["w_in_p"] = _pad_w_in(w["w_in"])
    p["w_glu"] = w["w_glu"]
    p["w_out_ssm"] = w["w_out_ssm"]
    p["w_q_p"] = _pad_heads_cols(w["w_q_up"], QK_HEAD)
    p["w_kv_p"] = _pad_kv(w["w_kv_up"])
    p["w_out_mla_p"] = _pad_out_mla(w["w_out_mla"])
    p["w_o"] = w["w_o"]
    p["w_ff1"] = w["w_ff1"]
    p["w_ff2"] = w["w_ff2"]
    p["mix_g"] = w["mix_norm_g"].reshape(1, D_MODEL)
    p["ffn_g"] = w["ffn_norm_g"].reshape(1, D_MODEL)
    p["b_gate"] = w["b_gate"]
    p["b_glu"] = w["b_glu"].reshape(1, SSM_WIDTH)
    p["d"] = w["ssm_d"].reshape(1, SSM_WIDTH)
    p["q_g"] = w["q_norm_g"].reshape(1, Q_LORA)
    p["kv_g"] = w["kv_norm_g"].reshape(1, KV_LORA)
    p["qh_g"] = jnp.pad(w["q_head_g"], (0, HEAD_PAD - QK_HEAD)).reshape(1, HEAD_PAD)
    p["kh_g"] = jnp.pad(w["k_head_g"], (0, HEAD_PAD - QK_HEAD)).reshape(1, HEAD_PAD)
    p["c_blk"] = _c_block(w["ssm_c_re"], w["ssm_c_im"]).astype(BF16)
    zoh, p["zoh_vjp"] = [], []
    for dr in range(2):
        out, vjp = jax.vjp(_zoh, w["ssm_lam_re"][dr], w["ssm_lam_im"][dr], w["ssm_log_step"][dr],
                           w["ssm_b_re"][dr], w["ssm_b_im"][dr])
        zoh.append(out)
        p["zoh_vjp"].append(vjp)
    p["b_blk"] = [_b_block(z[2], z[3]).astype(BF16) for z in zoh]
    p["coef_fwd"] = [_scan_tables(zoh[0][0], zoh[0][1], False), _scan_tables(zoh[1][0], zoh[1][1], True)]
    p["coef_adj"] = [_scan_tables(zoh[0][0], -zoh[0][1], True), _scan_tables(zoh[1][0], -zoh[1][1], False)]
    return p


def _head_prep_fwd(q_raw, kv_raw, proj, tabs, p, li):
    cf, sa, sb = tabs
    scale = QK_HEAD ** -0.5

    def fn(qr, kn, vv, kr, cfv, sav, sbv, gq, gk):
        qo, ko = [], []
        for h in range(MLA_HEADS):
            sl = slice(h * HEAD_PAD, (h + 1) * HEAD_PAD)
            qo.append(_rope(_rms(qr[:, sl], gq, QK_HEAD), cfv, sav, sbv) * scale)
            ko.append(_rope(_rms(kn[:, sl] + kr, gk, QK_HEAD), cfv, sav, sbv))
        return jnp.concatenate(qo, axis=1), jnp.concatenate(ko, axis=1), vv

    n = MLA_HEADS * HEAD_PAD
    return _rowwise(fn, [(q_raw, n, 0), (kv_raw, n, 0), (kv_raw, n, 1), (proj, HEAD_PAD, P_KR // HEAD_PAD),
                         (cf, HEAD_PAD, 0), (sa, HEAD_PAD, 0), (sb, HEAD_PAD, 0)], [p["qh_g"], p["kh_g"]],
                    [(n, BF16), (n, BF16), (n, BF16)], tm=256, name=f"head_prep_fwd_{li}")


def _head_prep_bwd(dq, dk, dv, q_raw, kv_raw, proj, tabs, p, li):
    cf, sa, sb = tabs
    scale = QK_HEAD ** -0.5

    def fn(dqv, dkv, dvv, qr, kn, kr, cfv, sav, sbv, gq, gk):
        dqo, dko = [], []
        dkr = jnp.zeros_like(kr)
        dgq = jnp.zeros((1, HEAD_PAD), F32)
        dgk = jnp.zeros((1, HEAD_PAD), F32)
        for h in range(MLA_HEADS):
            sl = slice(h * HEAD_PAD, (h + 1) * HEAD_PAD)
            dx, dg = _rms_bwd(qr[:, sl], gq, _rope_t(dqv[:, sl] * scale, cfv, sav, sbv), QK_HEAD)
            dqo.append(dx)
            dgq = dgq + dg
            dx, dg = _rms_bwd(kn[:, sl] + kr, gk, _rope_t(dkv[:, sl], cfv, sav, sbv), QK_HEAD)
            dko.append(dx)
            dkr = dkr + dx
            dgk = dgk + dg
        return jnp.concatenate(dqo, axis=1), jnp.concatenate(dko + [dvv], axis=1), dkr, dgq, dgk

    n = MLA_HEADS * HEAD_PAD
    return _rowwise(fn, [(dq, n, 0), (dk, n, 0), (dv, n, 0), (q_raw, n, 0), (kv_raw, n, 0),
                         (proj, HEAD_PAD, P_KR // HEAD_PAD), (cf, HEAD_PAD, 0), (sa, HEAD_PAD, 0), (sb, HEAD_PAD, 0)],
                    [p["qh_g"], p["kh_g"]], [(n, BF16), (2 * n, BF16), (HEAD_PAD, BF16)],
                    [(HEAD_PAD, F32), (HEAD_PAD, F32)], tm=256, name=f"head_prep_bwd_{li}")


def _layer_fwd(x, p, tabs, li):
    sv = {"x": x}
    h = _rowwise(lambda xv, g: _rms(xv, g, D_MODEL), [(x, D_MODEL, 0)], [p["mix_g"]], [(D_MODEL, BF16)],
                 name=f"mix_norm_{li}")[0]
    proj = _mm_nn(h, p["w_in_p"], name=f"in_proj_{li}")
    sv["h"], sv["proj"] = h, proj
    y_f, xr_f, xi_f = _s5_scan_fwd(proj, p["b_blk"][0], p["c_blk"], p["coef_fwd"][0], False, f"s5_fwd_f_{li}")
    y_b, xr_b, xi_b = _s5_scan_fwd(proj, p["b_blk"][1], p["c_blk"], p["coef_fwd"][1], True, f"s5_fwd_b_{li}")
    sv["states"] = [(xr_f, xi_f), (xr_b, xi_b)]
    y_raw, yg = _rowwise(lambda a, b, u, d: (a + b + d * u, _gelu(a + b + d * u)),
                         [(y_f, SSM_WIDTH, 0), (y_b, SSM_WIDTH, 0), (proj, SSM_WIDTH, P_U // SSM_WIDTH)], [p["d"]],
                         [(SSM_WIDTH, F32), (SSM_WIDTH, BF16)], name=f"s5_gelu_{li}")
    z = _mm_nn(yg, p["w_glu"], name=f"glu_proj_{li}")
    y_ssm = _rowwise(lambda yr, zv, b: _gelu(yr) * _sigmoid(zv + b), [(y_raw, SSM_WIDTH, 0), (z, SSM_WIDTH, 0)],
                     [p["b_glu"]], [(SSM_WIDTH, BF16)], name=f"glu_{li}")[0]
    sv.update(y_raw=y_raw, yg=yg, z=z, y_ssm=y_ssm)
    cqn, ckvn = _rowwise(lambda cq, ckv, gq, gkv: (_rms(cq, gq, Q_LORA), _rms(ckv, gkv, KV_LORA)),
                         [(proj, Q_LORA, P_CQ // Q_LORA), (proj, KV_LORA, P_CKV // KV_LORA)], [p["q_g"], p["kv_g"]],
                         [(Q_LORA, BF16), (KV_LORA, BF16)], name=f"lora_norm_{li}")
    q_raw = _mm_nn(cqn, p["w_q_p"], name=f"q_up_{li}")
    kv_raw = _mm_nn(ckvn, p["w_kv_p"], name=f"kv_up_{li}")
    q, k, v = _head_prep_fwd(q_raw, kv_raw, proj, tabs, p, li)
    o, lse = _attn_fwd(q, k, v, f"attn_fwd_{li}")
    sv.update(cqn=cqn, ckvn=ckvn, q_raw=q_raw, kv_raw=kv_raw, q=q, k=k, v=v, o=o, lse=lse)
    t_ssm = _mm_nn(y_ssm, p["w_out_ssm"], name=f"out_ssm_{li}")
    t_mla = _mm_nn(o, p["w_out_mla_p"], name=f"out_mla_{li}")
    merged = _rowwise(lambda g0, g1, ts, tmv, b: _sigmoid(g0 + b[0:1]) * ts + _sigmoid(g1 + b[1:2]) * tmv,
                      [(proj, D_MODEL, 0), (proj, D_MODEL, 1), (t_ssm, D_MODEL, 0), (t_mla, D_MODEL, 0)],
                      [p["b_gate"]], [(D_MODEL, BF16)], name=f"merge_{li}")[0]
    x1 = _mm_nn(merged, p["w_o"], add=x, name=f"o_proj_{li}")
    sv.update(t_ssm=t_ssm, t_mla=t_mla, merged=merged, x1=x1)
    h2 = _rowwise(lambda xv, g: _rms(xv, g, D_MODEL), [(x1, D_MODEL, 0)], [p["ffn_g"]], [(D_MODEL, BF16)],
                  name=f"ffn_norm_{li}")[0]
    a = _mm_nn(h2, p["w_ff1"], name=f"ff1_{li}")
    r = _rowwise(lambda av: jnp.square(jnp.maximum(av, 0.0)), [(a, D_FF, 0)], [], [(D_FF, BF16)], tm=256,
                 name=f"relu2_{li}")[0]
    x2 = _mm_nn(r, p["w_ff2"], add=x1, name=f"ff2_{li}")
    sv.update(h2=h2, a=a, r=r)
    return x2, sv


def _layer_bwd(dx2, sv, p, tabs, li):
    g = {}
    dx2_b = dx2.astype(BF16)
    dr = _mm_nt(dx2_b, p["w_ff2"], name=f"d_ff2_x_{li}")
    g["w_ff2"] = _mm_tn(sv["r"], dx2_b, name=f"d_ff2_w_{li}")
    da = _rowwise(lambda drv, av: drv * (2.0 * jnp.maximum(av, 0.0)), [(dr, D_FF, 0), (sv["a"], D_FF, 0)], [],
                  [(D_FF, BF16)], tm=256, name=f"d_relu2_{li}")[0]
    dh2 = _mm_nt(da, p["w_ff1"], name=f"d_ff1_x_{li}")
    g["w_ff1"] = _mm_tn(sv["h2"], da, name=f"d_ff1_w_{li}")

    def norm_bwd(xv, dyv, dres, gg):
        dx, dg = _rms_bwd(xv, gg, dyv, D_MODEL)
        return dres + dx, dg

    dx1, dg = _rowwise(norm_bwd, [(sv["x1"], D_MODEL, 0), (dh2, D_MODEL, 0), (dx2, D_MODEL, 0)], [p["ffn_g"]],
                       [(D_MODEL, F32)], [(D_MODEL, F32)], name=f"d_ffn_norm_{li}")
    g["ffn_norm_g"] = dg.reshape(D_MODEL)
    dx1_b = dx1.astype(BF16)
    dmerged = _mm_nt(dx1_b, p["w_o"], name=f"d_o_x_{li}")
    g["w_o"] = _mm_tn(sv["merged"], dx1_b, name=f"d_o_w_{li}")

    def merge_bwd(dm, g0, g1, ts, tmv, b):
        s0, s1 = _sigmoid(g0 + b[0:1]), _sigmoid(g1 + b[1:2])
        dg0, dg1 = dm * ts * s0 * (1.0 - s0), dm * tmv * s1 * (1.0 - s1)
        return (dm * s0, dm * s1, jnp.concatenate([dg0, dg1], axis=1),
                jnp.sum(dg0, axis=0, keepdims=True), jnp.sum(dg1, axis=0, keepdims=True))

    proj = sv["proj"]
    dt_ssm, dt_mla, dgate, db0, db1 = _rowwise(
        merge_bwd, [(dmerged, D_MODEL, 0), (proj, D_MODEL, 0), (proj, D_MODEL, 1), (sv["t_ssm"], D_MODEL, 0),
                    (sv["t_mla"], D_MODEL, 0)], [p["b_gate"]],
        [(D_MODEL, BF16), (D_MODEL, BF16), (2 * D_MODEL, BF16)], [(D_MODEL, F32), (D_MODEL, F32)], tm=256,
        name=f"d_merge_{li}")
    g["b_gate"] = jnp.concatenate([db0, db1], axis=0)
    dy_ssm = _mm_nt(dt_ssm, p["w_out_ssm"], name=f"d_out_ssm_x_{li}")
    g["w_out_ssm"] = _mm_tn(sv["y_ssm"], dt_ssm, name=f"d_out_ssm_w_{li}")
    do = _mm_nt(dt_mla, p["w_out_mla_p"], name=f"d_out_mla_x_{li}")
    g["w_out_mla"] = _unpad_out_mla(_mm_tn(sv["o"], dt_mla, name=f"d_out_mla_w_{li}"))

    def glu_bwd(dyv, yr, zv, b):
        yg = _gelu(yr)
        sg = _sigmoid(zv + b)
        dz = dyv * yg * sg * (1.0 - sg)
        return dz, dyv * sg, jnp.sum(dz, axis=0, keepdims=True)

    dz, dyg_direct, dbglu = _rowwise(glu_bwd, [(dy_ssm, SSM_WIDTH, 0), (sv["y_raw"], SSM_WIDTH, 0), (sv["z"], SSM_WIDTH, 0)],
                                     [p["b_glu"]], [(SSM_WIDTH, BF16), (SSM_WIDTH, F32)], [(SSM_WIDTH, F32)],
                                     name=f"d_glu_{li}")
    g["b_glu"] = dbglu.reshape(SSM_WIDTH)
    dyg_mm = _mm_nt(dz, p["w_glu"], name=f"d_glu_x_{li}")
    g["w_glu"] = _mm_tn(sv["yg"], dz, name=f"d_glu_w_{li}")

    def gelu_bwd(d1, d2, yr, u, d):
        dyr = (d1 + d2) * _gelu_grad(yr)
        return dyr, dyr * d, jnp.sum(dyr * u, axis=0, keepdims=True)

    dy_raw, du_d, dd = _rowwise(gelu_bwd, [(dyg_direct, SSM_WIDTH, 0), (dyg_mm, SSM_WIDTH, 0), (sv["y_raw"], SSM_WIDTH, 0),
                                           (proj, SSM_WIDTH, P_U // SSM_WIDTH)], [p["d"]],
                                [(SSM_WIDTH, BF16), (SSM_WIDTH, F32)], [(SSM_WIDTH, F32)], name=f"d_gelu_{li}")
    g["ssm_d"] = dd.reshape(SSM_GROUPS, SSM_GROUP)
    du_parts, dc_sum = [du_d], None
    zoh_grads = []
    for dr_i in range(2):
        xr, xi = sv["states"][dr_i]
        du_i, da_i, db_i, dc_i = _s5_scan_bwd(dy_raw, proj, xr, xi, p["b_blk"][dr_i], p["c_blk"], p["coef_adj"][dr_i],
                                              dr_i == 0, f"s5_bwd_{'fb'[dr_i]}_{li}")
        du_parts.append(du_i)
        dc_sum = dc_i if dc_sum is None else dc_sum + dc_i
        da_i = jnp.sum(da_i, axis=1)
        dar = da_i[:, :CHUNK_STATE].reshape(SSM_GROUPS, SSM_STATE)
        dai = da_i[:, CHUNK_STATE:].reshape(SSM_GROUPS, SSM_STATE)
        dbr, dbi = _b_unblock(db_i)
        zoh_grads.append(p["zoh_vjp"][dr_i]((dar, dai, dbr, dbi)))
    for k_i, nm in enumerate(("ssm_lam_re", "ssm_lam_im", "ssm_log_step", "ssm_b_re", "ssm_b_im")):
        g[nm] = jnp.stack([zoh_grads[0][k_i], zoh_grads[1][k_i]], axis=0)
    g["ssm_c_re"], g["ssm_c_im"] = _c_unblock(dc_sum)
    du = _rowwise(lambda a, b, c: a + b + c, [(d_, SSM_WIDTH, 0) for d_ in du_parts], [], [(SSM_WIDTH, BF16)],
                  name=f"d_u_sum_{li}")[0]
    dq, dk, dv = _attn_bwd(sv["q"], sv["k"], sv["v"], sv["o"], do, sv["lse"], f"attn_bwd_{li}")
    dq_raw, dkv_raw, dkr, dgq, dgk = _head_prep_bwd(dq, dk, dv, sv["q_raw"], sv["kv_raw"], proj, tabs, p, li)
    g["q_head_g"] = dgq.reshape(HEAD_PAD)[:QK_HEAD]
    g["k_head_g"] = dgk.reshape(HEAD_PAD)[:QK_HEAD]
    dcqn = _mm_nt(dq_raw, p["w_q_p"], name=f"d_q_up_x_{li}")
    g["w_q_up"] = _unpad_heads_cols(_mm_tn(sv["cqn"], dq_raw, name=f"d_q_up_w_{li}"), QK_HEAD)
    dckvn = _mm_nt(dkv_raw, p["w_kv_p"], name=f"d_kv_up_x_{li}")
    g["w_kv_up"] = _unpad_kv(_mm_tn(sv["ckvn"], dkv_raw, name=f"d_kv_up_w_{li}"))

    def lora_bwd(cq, ckv, d1, d2, gq, gkv):
        dx1_, dg1 = _rms_bwd(cq, gq, d1, Q_LORA)
        dx2_, dg2 = _rms_bwd(ckv, gkv, d2, KV_LORA)
        return dx1_, dx2_, dg1, dg2

    dcq, dckv, dgqn, dgkvn = _rowwise(
        lora_bwd, [(proj, Q_LORA, P_CQ // Q_LORA), (proj, KV_LORA, P_CKV // KV_LORA), (dcqn, Q_LORA, 0), (dckvn, KV_LORA, 0)],
        [p["q_g"], p["kv_g"]], [(Q_LORA, BF16), (KV_LORA, BF16)], [(Q_LORA, F32), (KV_LORA, F32)], name=f"d_lora_norm_{li}")
    g["q_norm_g"], g["kv_norm_g"] = dgqn.reshape(Q_LORA), dgkvn.reshape(KV_LORA)
    gap = jnp.zeros((dx2.shape[0], P_CQ - P_KR - HEAD_PAD), BF16)
    dproj = jnp.concatenate([dgate, du, dckv, dkr, gap, dcq], axis=1)
    dh = _mm_nt(dproj, p["w_in_p"], name=f"d_in_x_{li}")
    g["w_in"] = _unpad_w_in(_mm_tn(sv["h"], dproj, name=f"d_in_w_{li}"))
    dx, dg = _rowwise(norm_bwd, [(sv["x"], D_MODEL, 0), (dh, D_MODEL, 0), (dx1, D_MODEL, 0)], [p["mix_g"]],
                      [(D_MODEL, F32)], [(D_MODEL, F32)], name=f"d_mix_norm_{li}")
    g["mix_norm_g"] = dg.reshape(D_MODEL)
    return dx, g


def _local_step(x, target, full):
    tabs = _rope_tables(x.shape[0])
    saved, preps = [], []
    for li in range(DEPTH):
        p = _prep_layer({k: v[li] for k, v in full.items()})
        x, sv = _layer_fwd(x, p, tabs, li)
        saved.append(sv)
        preps.append(p)

    def loss_fn(y, t):
        err = y - t
        return err * (1.0 / D_MODEL), jnp.sum(jnp.sum(err * err, axis=1, keepdims=True), axis=0, keepdims=True) * jnp.ones((1, LANES), F32)

    dx, lsum = _rowwise(loss_fn, [(x, D_MODEL, 0), (target, D_MODEL, 0)], [], [(D_MODEL, F32)], [(LANES, F32)], name="loss")
    loss = 0.5 * lsum[0, 0] * (1.0 / D_MODEL)
    grads = [None] * DEPTH
    for li in reversed(range(DEPTH)):
        dx, grads[li] = _layer_bwd(dx, saved[li], preps[li], tabs, li)
    stacked = {k: jnp.stack([grads[li][k] for li in range(DEPTH)], axis=0) for k in WEIGHTS}
    return loss, dx, stacked


def _exchange(buf, scatter, name):
    shape = buf.shape[1:] if scatter else buf.shape

    def body(buf_ref, out_ref, send_sems, recv_sems, local_sem):
        x, y, c = lax.axis_index("x"), lax.axis_index("y"), lax.axis_index("c")
        me = 4 * x + 2 * y + c
        mine = pltpu.make_async_copy(buf_ref.at[me] if scatter else buf_ref, out_ref.at[me], local_sem)
        mine.start()
        copies = []
        for k in range(1, N_DEV):
            kx, ky, kc = (k >> 2) & 1, (k >> 1) & 1, k & 1
            peer = (x ^ kx, y ^ ky, c ^ kc)
            peer_idx = 4 * peer[0] + 2 * peer[1] + peer[2]
            cp = pltpu.make_async_remote_copy(
                src_ref=buf_ref.at[peer_idx] if scatter else buf_ref, dst_ref=out_ref.at[me],
                send_sem=send_sems.at[k], recv_sem=recv_sems.at[k], device_id=peer, device_id_type=pl.DeviceIdType.MESH)
            cp.start()
            copies.append(cp)
        for cp in copies:
            cp.wait()
        mine.wait()

    return pl.pallas_call(
        body, out_shape=jax.ShapeDtypeStruct((N_DEV,) + tuple(shape), buf.dtype),
        in_specs=[pl.BlockSpec(memory_space=pl.ANY)], out_specs=pl.BlockSpec(memory_space=pl.ANY),
        scratch_shapes=[pltpu.SemaphoreType.DMA((N_DEV,)), pltpu.SemaphoreType.DMA((N_DEV,)), pltpu.SemaphoreType.DMA],
        name=name, compiler_params=pltpu.CompilerParams(has_side_effects=True),
    )(buf)


def _adamw(parts, w, m, v, name):
    r = w.shape[0]
    tm = _pick_rows(r)

    def body(p_ref, w_ref, m_ref, v_ref, g_ref, d_ref, nm_ref, nv_ref):
        g = p_ref[0]
        for j in range(1, N_DEV):
            g = g + p_ref[j]
        m_new = ADAM_B1 * m_ref[...] + (1.0 - ADAM_B1) * g
        v_new = ADAM_B2 * v_ref[...] + (1.0 - ADAM_B2) * (g * g)
        m_hat = m_new / (1.0 - ADAM_B1 ** ADAM_STEP)
        v_hat = v_new / (1.0 - ADAM_B2 ** ADAM_STEP)
        g_ref[...] = g
        d_ref[...] = -ADAM_LR * (m_hat / (jnp.sqrt(v_hat) + ADAM_EPS) + ADAM_WD * w_ref[...])
        nm_ref[...] = m_new
        nv_ref[...] = v_new

    spec = pl.BlockSpec((tm, D_MODEL), lambda i: (i, 0))
    return pl.pallas_call(
        body, grid=(r // tm,), in_specs=[pl.BlockSpec((N_DEV, tm, D_MODEL), lambda i: (0, i, 0)), spec, spec, spec],
        out_specs=[spec] * 4, out_shape=[jax.ShapeDtypeStruct((r, D_MODEL), F32)] * 4, name=name,
        compiler_params=_cparams(("parallel",)),
    )(parts, w, m, v)


def _pick_rows(r):
    for t in (512, 256, 128, 64, 32, 16, 8):
        if r % t == 0:
            return t
    return r


def _pack(arrs, dtype, row_mult):
    flat = jnp.concatenate([a.reshape(-1).astype(dtype) for a in arrs])
    n = flat.shape[0]
    per = row_mult * D_MODEL
    total = -(-n // per) * per
    return jnp.pad(flat, (0, total - n)).reshape(total // D_MODEL, D_MODEL)


def _unpack(flat, shapes):
    lead = flat.shape[:-2]
    flat = flat.reshape(lead + (-1,))
    out, off = [], 0
    for shp in shapes:
        n = math.prod(shp)
        out.append(flat[..., off:off + n].reshape(lead + tuple(shp)))
        off += n
    return out


def _to_shards(gfull, axis):
    shp = gfull.shape
    gfull = gfull.reshape(shp[:axis] + (N_DEV, shp[axis] // N_DEV) + shp[axis + 1:])
    return jnp.moveaxis(gfull, axis, 0)


def _from_shards(parts, axis):
    parts = jnp.moveaxis(parts, 0, axis)
    shp = parts.shape
    return parts.reshape(shp[:axis] + (shp[axis] * shp[axis + 1],) + shp[axis + 2:])


def kernel(x, mix_norm_g, w_in, b_gate, ssm_lam_re, ssm_lam_im, ssm_log_step, ssm_b_re, ssm_b_im, ssm_c_re, ssm_c_im, ssm_d, w_glu, b_glu, w_out_ssm, q_norm_g, kv_norm_g, w_q_up, w_kv_up, q_head_g, k_head_g, w_out_mla, w_o, ffn_norm_g, w_ff1, w_ff2, loss_target, m_mix_norm_g, m_w_in, m_b_gate, m_ssm_lam_re, m_ssm_lam_im, m_ssm_log_step, m_ssm_b_re, m_ssm_b_im, m_ssm_c_re, m_ssm_c_im, m_ssm_d, m_w_glu, m_b_glu, m_w_out_ssm, m_q_norm_g, m_kv_norm_g, m_w_q_up, m_w_kv_up, m_q_head_g, m_k_head_g, m_w_out_mla, m_w_o, m_ffn_norm_g, m_w_ff1, m_w_ff2, v_mix_norm_g, v_w_in, v_b_gate, v_ssm_lam_re, v_ssm_lam_im, v_ssm_log_step, v_ssm_b_re, v_ssm_b_im, v_ssm_c_re, v_ssm_c_im, v_ssm_d, v_w_glu, v_b_glu, v_w_out_ssm, v_q_norm_g, v_kv_norm_g, v_w_q_up, v_w_kv_up, v_q_head_g, v_k_head_g, v_w_out_mla, v_w_o, v_ffn_norm_g, v_w_ff1, v_w_ff2):
    args = dict(locals())
    w = {n: args[n] for n in WEIGHTS}
    m = {n: args["m_" + n] for n in WEIGHTS}
    v = {n: args["v_" + n] for n in WEIGHTS}

    mats = [n for n in SHARDED if n != "b_gate"]
    bias_bits = lax.bitcast_convert_type(w["b_gate"], BF16)
    send = _pack([w[n] for n in mats] + [bias_bits], BF16, 16)
    gathered = _exchange(send, False, "weight_all_gather")
    parts = _unpack(gathered, [w[n].shape for n in mats] + [bias_bits.shape])
    full = {n: _from_shards(pt, SHARD_AXIS[n] + 0) for n, pt in zip(mats, parts[:-1])}
    full["b_gate"] = _from_shards(lax.bitcast_convert_type(parts[-1], F32), SHARD_AXIS["b_gate"])
    for n in REPLICATED:
        full[n] = w[n]

    loss_part, dx, grads = _local_step(x[0], loss_target[0], full)
    loss = lax.psum(loss_part, ("x", "y", "c"))

    g_send = _pack8([_to_shards(grads[n], SHARD_AXIS[n]) for n in SHARDED])
    g_parts = _exchange(g_send, True, "grad_reduce_scatter")
    r_send = _pack([grads[n] for n in REPLICATED], F32, 8)
    r_parts = _exchange(r_send, False, "grad_all_gather")

    outs = {}
    shard_shapes = [w[n].shape for n in SHARDED]
    res = _adamw(g_parts, _pack([w[n] for n in SHARDED], F32, 8), _pack([m[n] for n in SHARDED], F32, 8),
                 _pack([v[n] for n in SHARDED], F32, 8), "adamw_sharded")
    for kind, flat in zip(("grad", "delta", "new_m", "new_v"), res):
        for n, a in zip(SHARDED, _unpack(flat, shard_shapes)):
            outs[kind + "_" + n] = a
    rep_shapes = [w[n].shape for n in REPLICATED]
    res = _adamw(r_parts, _pack([w[n] for n in REPLICATED], F32, 8), _pack([m[n] for n in REPLICATED], F32, 8),
                 _pack([v[n] for n in REPLICATED], F32, 8), "adamw_replicated")
    for kind, flat in zip(("grad", "delta", "new_m", "new_v"), res):
        for n, a in zip(REPLICATED, _unpack(flat, rep_shapes)):
            outs[kind + "_" + n] = a
    return (loss, dx[None], *[outs[k + "_" + n] for k in ("grad", "delta", "new_m", "new_v") for n in WEIGHTS])


def _pack8(arrs):
    flat = jnp.concatenate([a.reshape(N_DEV, -1).astype(F32) for a in arrs], axis=1)
    n = flat.shape[1]
    per = SUBLANES * D_MODEL
    total = -(-n // per) * per
    return jnp.pad(flat, ((0, 0), (0, total - n))).reshape(N_DEV, total // D_MODEL, D_MODEL)
```

```python
import functools
import math

import jax
import jax.numpy as jnp
from jax import lax
from jax.experimental import pallas as pl
from jax.experimental.pallas import tpu as pltpu

F32 = jnp.float32
BF16 = jnp.bfloat16
_MXU = jnp.bfloat16

D_MODEL = 1024
DEPTH = 4
SSM_WIDTH = 512
SSM_GROUP = 16
SSM_GROUPS = 32
SSM_STATE = 64
MLA_HEADS = 8
QK_NOPE = 64
QK_ROPE = 32
QK_HEAD = 96
V_HEAD = 64
Q_LORA = 384
KV_LORA = 256
ROPE_THETA = 10000.0
D_FF = 4096
EPS = 1e-6
HEAD_PAD = 128
N_DEV = 8
LANES = 128
SUBLANES = 8
CHUNK_GROUPS = 8
N_CHUNKS = SSM_GROUPS // CHUNK_GROUPS
CHUNK_STATE = CHUNK_GROUPS * SSM_STATE

P_GATE, P_U, P_CKV, P_KR, P_CQ = 0, 2048, 2560, 2816, 3072
P_COLS = 3584
IN_U, IN_CQ, IN_CKV, IN_KR, IN_GATE = 0, 512, 896, 1152, 1184
IN_COLS = 3232

ADAM_LR = 0.001
ADAM_B1 = 0.9
ADAM_B2 = 0.999
ADAM_EPS = 1e-08
ADAM_WD = 0.01
ADAM_STEP = 10

VMEM_LIMIT = 56 * 1024 * 1024
TN_OUT_BLOCK_BYTES = 8 * 1024 * 1024
MXU_WIDTH = 256

SHARDED = ("w_in", "b_gate", "w_glu", "w_out_ssm", "w_q_up", "w_kv_up", "w_out_mla", "w_o", "w_ff1", "w_ff2")
SHARD_AXIS = {"w_in": 2, "b_gate": 2, "w_glu": 1, "w_out_ssm": 2, "w_q_up": 2, "w_kv_up": 2, "w_out_mla": 2,
              "w_o": 1, "w_ff1": 2, "w_ff2": 1}
REPLICATED = ("mix_norm_g", "ssm_lam_re", "ssm_lam_im", "ssm_log_step", "ssm_b_re", "ssm_b_im", "ssm_c_re",
              "ssm_c_im", "ssm_d", "b_glu", "q_norm_g", "kv_norm_g", "q_head_g", "k_head_g", "ffn_norm_g")
WEIGHTS = ("mix_norm_g", "w_in", "b_gate", "ssm_lam_re", "ssm_lam_im", "ssm_log_step", "ssm_b_re", "ssm_b_im",
           "ssm_c_re", "ssm_c_im", "ssm_d", "w_glu", "b_glu", "w_out_ssm", "q_norm_g", "kv_norm_g", "w_q_up",
           "w_kv_up", "q_head_g", "k_head_g", "w_out_mla", "w_o", "ffn_norm_g", "w_ff1", "w_ff2")


def _cparams(sem):
    return pltpu.CompilerParams(dimension_semantics=sem, vmem_limit_bytes=VMEM_LIMIT)


def _dot(a, b, dims):
    return lax.dot_general(a.astype(_MXU), b.astype(_MXU), (dims, ((), ())), preferred_element_type=F32)


def _dot_nn(a, b):
    return _dot(a, b, ((1,), (0,)))


def _dot_nt(a, b):
    return _dot(a, b, ((1,), (1,)))


def _dot_tn(a, b):
    return _dot(a, b, ((0,), (0,)))


def _rowwise(fn, rows, consts, outs, accs=(), *, tm=512, name):
    n_rows = rows[0][0].shape[0]
    tm = min(tm, n_rows)
    n_in = len(rows) + len(consts)
    n_o, n_a = len(outs), len(accs)

    def body(*refs):
        res = fn(*[r[...] for r in refs[:n_in]])
        if not isinstance(res, (tuple, list)):
            res = (res,)
        orefs = refs[n_in:]
        for k in range(n_o):
            orefs[k][...] = res[k].astype(orefs[k].dtype)
        if n_a:
            @pl.when(pl.program_id(0) == 0)
            def _():
                for k in range(n_a):
                    orefs[n_o + k][...] = jnp.zeros_like(orefs[n_o + k])
            for k in range(n_a):
                orefs[n_o + k][...] += res[n_o + k]

    in_specs = [pl.BlockSpec((tm, w), functools.partial(lambda i, j: (i, j), j=j)) for (_, w, j) in rows]
    in_specs += [pl.BlockSpec(c.shape, functools.partial(lambda i, nd: (0,) * nd, nd=c.ndim)) for c in consts]
    out_specs = [pl.BlockSpec((tm, w), lambda i: (i, 0)) for (w, _) in outs]
    out_specs += [pl.BlockSpec((1, w), lambda i: (0, 0)) for (w, _) in accs]
    out_shape = [jax.ShapeDtypeStruct((n_rows, w), dt) for (w, dt) in outs]
    out_shape += [jax.ShapeDtypeStruct((1, w), dt) for (w, dt) in accs]
    res = pl.pallas_call(
        body, grid=(n_rows // tm,), in_specs=in_specs, out_specs=out_specs, out_shape=out_shape, name=name,
        compiler_params=_cparams(("arbitrary",) if n_a else ("parallel",)),
    )(*[r[0] for r in rows], *consts)
    return res


def _pick(n, cap):
    if n <= cap:
        return n
    for unit in (MXU_WIDTH, LANES):
        best = 0
        for t in range(unit, cap + 1, unit):
            if n % t == 0:
                best = t
        if best:
            return best
    return n


def _mm_nn(a, b, *, add=None, out_dtype=F32, a_cols=None, name):
    m = a.shape[0]
    k, n = b.shape
    aw, aj = (k, 0) if a_cols is None else a_cols
    tm, tn = min(512, m), _pick(n, 1024)

    def body(*refs):
        acc = _dot_nn(refs[0][...], refs[1][...])
        if add is not None:
            acc = acc + refs[2][...]
        refs[-1][...] = acc.astype(out_dtype)

    in_specs = [pl.BlockSpec((tm, aw), lambda j, i: (i, aj)), pl.BlockSpec((k, tn), lambda j, i: (0, j))]
    args = [a, b]
    if add is not None:
        in_specs.append(pl.BlockSpec((tm, tn), lambda j, i: (i, j)))
        args.append(add)
    return pl.pallas_call(
        body, grid=(n // tn, m // tm), in_specs=in_specs, out_specs=pl.BlockSpec((tm, tn), lambda j, i: (i, j)),
        out_shape=jax.ShapeDtypeStruct((m, n), out_dtype), name=name, compiler_params=_cparams(("parallel", "parallel")),
    )(*args)


def _mm_nt(a, b, *, out_dtype=F32, name):
    m, k = a.shape
    n = b.shape[0]
    tm, tn = min(512, m), _pick(n, 1024)

    def body(a_ref, b_ref, o_ref):
        o_ref[...] = _dot_nt(a_ref[...], b_ref[...]).astype(out_dtype)

    return pl.pallas_call(
        body, grid=(n // tn, m // tm),
        in_specs=[pl.BlockSpec((tm, k), lambda j, i: (i, 0)), pl.BlockSpec((tn, k), lambda j, i: (j, 0))],
        out_specs=pl.BlockSpec((tm, tn), lambda j, i: (i, j)),
        out_shape=jax.ShapeDtypeStruct((m, n), out_dtype), name=name, compiler_params=_cparams(("parallel", "parallel")),
    )(a, b)


def _mm_tn(a, b, *, a_cols=None, name):
    s = a.shape[0]
    n = b.shape[1]
    mw, mj = (a.shape[1], 0) if a_cols is None else a_cols
    ts = min(512, s)
    tm = _pick(mw, 1024)
    tn = _pick(n, max(1024, TN_OUT_BLOCK_BYTES // (4 * tm)))
    n_mb = mw // tm

    def body(a_ref, b_ref, o_ref):
        @pl.when(pl.program_id(2) == 0)
        def _():
            o_ref[...] = jnp.zeros_like(o_ref)
        o_ref[...] += _dot_tn(a_ref[...], b_ref[...])

    return pl.pallas_call(
        body, grid=(n_mb, n // tn, s // ts),
        in_specs=[pl.BlockSpec((ts, tm), lambda i, j, t: (t, mj * n_mb + i)), pl.BlockSpec((ts, tn), lambda i, j, t: (t, j))],
        out_specs=pl.BlockSpec((tm, tn), lambda i, j, t: (i, j)),
        out_shape=jax.ShapeDtypeStruct((mw, n), F32), name=name,
        compiler_params=_cparams(("parallel", "parallel", "arbitrary")),
    )(a, b)


def _rms(x, g, n):
    r = lax.rsqrt(jnp.sum(x * x, axis=-1, keepdims=True) * (1.0 / n) + EPS)
    return x * r * g


def _rms_bwd(x, g, dy, n):
    r = lax.rsqrt(jnp.sum(x * x, axis=-1, keepdims=True) * (1.0 / n) + EPS)
    xr = x * r
    dyg = dy * g
    dx = r * dyg - xr * (r * r) * (jnp.sum(dyg * x, axis=-1, keepdims=True) * (1.0 / n))
    return dx, jnp.sum(dy * xr, axis=0, keepdims=True)


def _gelu(x):
    c = math.sqrt(2.0 / math.pi)
    return 0.5 * x * (1.0 + jnp.tanh(c * (x + 0.044715 * (x * x * x))))


def _gelu_grad(x):
    c = math.sqrt(2.0 / math.pi)
    t = jnp.tanh(c * (x + 0.044715 * (x * x * x)))
    return 0.5 * (1.0 + t) + 0.5 * x * (1.0 - t * t) * (c * (1.0 + 3.0 * 0.044715 * (x * x)))


def _sigmoid(x):
    return 1.0 / (1.0 + jnp.exp(-x))


def _rope(x, cf, sa, sb):
    return x * cf + pltpu.roll(x, HEAD_PAD - QK_ROPE // 2, 1) * sa + pltpu.roll(x, QK_ROPE // 2, 1) * sb


def _rope_t(d, cf, sa, sb):
    return d * cf + pltpu.roll(d * sa, QK_ROPE // 2, 1) + pltpu.roll(d * sb, HEAD_PAD - QK_ROPE // 2, 1)


def _scan_tables(ar, ai, reverse):
    ar = ar.reshape(N_CHUNKS, CHUNK_STATE)
    ai = ai.reshape(N_CHUNKS, CHUNK_STATE)
    pr, pi = [ar], [ai]
    for _ in range(SUBLANES - 1):
        pr, pi = pr + [pr[-1] * ar - pi[-1] * ai], pi + [pr[-1] * ai + pi[-1] * ar]
    row = jnp.arange(SUBLANES)[None, :, None]
    tiles = []
    for k in (1, 2, 4):
        mask = (row <= SUBLANES - 1 - k) if reverse else (row >= k)
        tiles.append(jnp.where(mask, pr[k - 1][:, None, :], 0.0))
        tiles.append(jnp.where(mask, pi[k - 1][:, None, :], 0.0))
    order = list(range(SUBLANES))[::-1] if reverse else list(range(SUBLANES))
    tiles.append(jnp.stack([pr[j] for j in order], axis=1))
    tiles.append(jnp.stack([pi[j] for j in order], axis=1))
    return jnp.stack(tiles, axis=1).astype(F32)


def _slab_scan(xr, xi, coef, carry_r, carry_i, reverse):
    for idx, k in enumerate((1, 2, 4)):
        sh = SUBLANES - k if reverse else k
        sr, si = pltpu.roll(xr, sh, 0), pltpu.roll(xi, sh, 0)
        cr, ci = coef[2 * idx], coef[2 * idx + 1]
        xr, xi = xr + cr * sr - ci * si, xi + cr * si + ci * sr
    pr, pi = coef[6], coef[7]
    xr = xr + pr * carry_r - pi * carry_i
    xi = xi + pr * carry_i + pi * carry_r
    return xr, xi


def _s5_scan_fwd(proj, b_blk, c_blk, coef, reverse, name):
    s = proj.shape[0]
    t_blk = min(512, s)
    n_t = s // t_blk
    n_slab = t_blk // SUBLANES
    last = 0 if reverse else SUBLANES - 1

    def tmap(t):
        return n_t - 1 - t if reverse else t

    def body(u_ref, b_ref, c_ref, coef_ref, y_ref, xr_ref, xi_ref, carry_ref):
        @pl.when(pl.program_id(1) == 0)
        def _():
            carry_ref[...] = jnp.zeros_like(carry_ref)
        bu = _dot_nn(u_ref[...], b_ref[0])
        xr_ref[...] = bu[:, :CHUNK_STATE]
        xi_ref[...] = bu[:, CHUNK_STATE:]
        coef_v = [coef_ref[0, k] for k in range(8)]

        def slab(i, carry):
            sl = (n_slab - 1 - i) if reverse else i
            rows = pl.ds(pl.multiple_of(sl * SUBLANES, SUBLANES), SUBLANES)
            xr, xi = _slab_scan(xr_ref[rows, :], xi_ref[rows, :], coef_v, carry[0], carry[1], reverse)
            xr_ref[rows, :] = xr
            xi_ref[rows, :] = xi
            return (jnp.broadcast_to(xr[last:last + 1, :], xr.shape), jnp.broadcast_to(xi[last:last + 1, :], xi.shape))

        cr, ci = lax.fori_loop(0, n_slab, slab, (carry_ref[0], carry_ref[1]))
        carry_ref[0] = cr
        carry_ref[1] = ci
        y_ref[...] = _dot_nn(xr_ref[...], c_ref[0, :CHUNK_STATE, :]) + _dot_nn(xi_ref[...], c_ref[0, CHUNK_STATE:, :])

    u_blk0 = P_U // LANES
    return pl.pallas_call(
        body, grid=(N_CHUNKS, n_t),
        in_specs=[pl.BlockSpec((t_blk, LANES), lambda c, t: (tmap(t), u_blk0 + c)),
                  pl.BlockSpec((1, LANES, 2 * CHUNK_STATE), lambda c, t: (c, 0, 0)),
                  pl.BlockSpec((1, 2 * CHUNK_STATE, LANES), lambda c, t: (c, 0, 0)),
                  pl.BlockSpec((1, 8, SUBLANES, CHUNK_STATE), lambda c, t: (c, 0, 0, 0))],
        out_specs=[pl.BlockSpec((t_blk, LANES), lambda c, t: (tmap(t), c)),
                   pl.BlockSpec((t_blk, CHUNK_STATE), lambda c, t: (tmap(t), c)),
                   pl.BlockSpec((t_blk, CHUNK_STATE), lambda c, t: (tmap(t), c))],
        out_shape=[jax.ShapeDtypeStruct((s, SSM_WIDTH), F32),
                   jax.ShapeDtypeStruct((s, N_CHUNKS * CHUNK_STATE), F32),
                   jax.ShapeDtypeStruct((s, N_CHUNKS * CHUNK_STATE), F32)],
        scratch_shapes=[pltpu.VMEM((2, SUBLANES, CHUNK_STATE), F32)],
        name=name, compiler_params=_cparams(("parallel", "arbitrary")),
    )(proj, b_blk, c_blk, coef)


def _s5_scan_bwd(dy, proj, x_re, x_im, b_blk, c_blk, coef, reverse, name):
    s = dy.shape[0]
    t_blk = min(512, s)
    n_t = s // t_blk
    n_slab = t_blk // SUBLANES
    last = 0 if reverse else SUBLANES - 1
    first = SUBLANES - 1 if reverse else 0

    def tmap(t):
        return n_t - 1 - t if reverse else t

    def body(dy_ref, u_ref, xr_ref, xi_ref, b_ref, c_ref, coef_ref, du_ref, da_ref, db_ref, dc_ref,
             carry_ref, lr_ref, li_ref):
        @pl.when(pl.program_id(1) == 0)
        def _():
            carry_ref[...] = jnp.zeros_like(carry_ref)
            da_ref[...] = jnp.zeros_like(da_ref)
            db_ref[...] = jnp.zeros_like(db_ref)
            dc_ref[...] = jnp.zeros_like(dc_ref)
        g = _dot_nt(dy_ref[...], c_ref[0])
        lr_ref[...] = g[:, :CHUNK_STATE]
        li_ref[...] = g[:, CHUNK_STATE:]
        coef_v = [coef_ref[0, k] for k in range(8)]
        row = lax.broadcasted_iota(jnp.int32, (SUBLANES, CHUNK_STATE), 0)
        sh_prev = SUBLANES - 1 if reverse else 1

        def slab(i, carry):
            cr, ci, ar_acc, ai_acc = carry
            sl = (n_slab - 1 - i) if reverse else i
            rows = pl.ds(pl.multiple_of(sl * SUBLANES, SUBLANES), SUBLANES)
            lr, li = _slab_scan(lr_ref[rows, :], li_ref[rows, :], coef_v, cr, ci, reverse)
            lr_ref[rows, :] = lr
            li_ref[rows, :] = li
            pr = jnp.where(row == first, cr, pltpu.roll(lr, sh_prev, 0))
            pi = jnp.where(row == first, ci, pltpu.roll(li, sh_prev, 0))
            xr, xi = xr_ref[rows, :], xi_ref[rows, :]
            ar_acc = ar_acc + xr * pr + xi * pi
            ai_acc = ai_acc + xr * pi - xi * pr
            return (jnp.broadcast_to(lr[last:last + 1, :], lr.shape), jnp.broadcast_to(li[last:last + 1, :], li.shape),
                    ar_acc, ai_acc)

        zero = jnp.zeros((SUBLANES, CHUNK_STATE), F32)
        cr, ci, ar_acc, ai_acc = lax.fori_loop(0, n_slab, slab, (carry_ref[0], carry_ref[1], zero, zero))
        carry_ref[0] = cr
        carry_ref[1] = ci
        da_ref[0, :, :CHUNK_STATE] += ar_acc
        da_ref[0, :, CHUNK_STATE:] += ai_acc
        lam_r, lam_i = lr_ref[...], li_ref[...]
        u = u_ref[...]
        du_ref[...] = _dot_nt(lam_r, b_ref[0, :, :CHUNK_STATE]) + _dot_nt(lam_i, b_ref[0, :, CHUNK_STATE:])
        db_ref[0, :, :CHUNK_STATE] += _dot_tn(u, lam_r)
        db_ref[0, :, CHUNK_STATE:] += _dot_tn(u, lam_i)
        dyv = dy_ref[...]
        dc_ref[0, :CHUNK_STATE, :] += _dot_tn(xr_ref[...], dyv)
        dc_ref[0, CHUNK_STATE:, :] += _dot_tn(xi_ref[...], dyv)

    u_blk0 = P_U // LANES
    return pl.pallas_call(
        body, grid=(N_CHUNKS, n_t),
        in_specs=[pl.BlockSpec((t_blk, LANES), lambda c, t: (tmap(t), c)),
                  pl.BlockSpec((t_blk, LANES), lambda c, t: (tmap(t), u_blk0 + c)),
                  pl.BlockSpec((t_blk, CHUNK_STATE), lambda c, t: (tmap(t), c)),
                  pl.BlockSpec((t_blk, CHUNK_STATE), lambda c, t: (tmap(t), c)),
                  pl.BlockSpec((1, LANES, 2 * CHUNK_STATE), lambda c, t: (c, 0, 0)),
                  pl.BlockSpec((1, 2 * CHUNK_STATE, LANES), lambda c, t: (c, 0, 0)),
                  pl.BlockSpec((1, 8, SUBLANES, CHUNK_STATE), lambda c, t: (c, 0, 0, 0))],
        out_specs=[pl.BlockSpec((t_blk, LANES), lambda c, t: (tmap(t), c)),
                   pl.BlockSpec((1, SUBLANES, 2 * CHUNK_STATE), lambda c, t: (c, 0, 0)),
                   pl.BlockSpec((1, LANES, 2 * CHUNK_STATE), lambda c, t: (c, 0, 0)),
                   pl.BlockSpec((1, 2 * CHUNK_STATE, LANES), lambda c, t: (c, 0, 0))],
        out_shape=[jax.ShapeDtypeStruct((s, SSM_WIDTH), F32),
                   jax.ShapeDtypeStruct((N_CHUNKS, SUBLANES, 2 * CHUNK_STATE), F32),
                   jax.ShapeDtypeStruct((N_CHUNKS, LANES, 2 * CHUNK_STATE), F32),
                   jax.ShapeDtypeStruct((N_CHUNKS, 2 * CHUNK_STATE, LANES), F32)],
        scratch_shapes=[pltpu.VMEM((2, SUBLANES, CHUNK_STATE), F32), pltpu.VMEM((t_blk, CHUNK_STATE), F32),
                        pltpu.VMEM((t_blk, CHUNK_STATE), F32)],
        name=name, compiler_params=_cparams(("parallel", "arbitrary")),
    )(dy, proj, x_re, x_im, b_blk, c_blk, coef)


def _zoh(lam_re, lam_im, log_step, b_re, b_im):
    step = jnp.exp(log_step)[:, None]
    mag = jnp.exp(lam_re * step)
    abar_r = mag * jnp.cos(lam_im * step)
    abar_i = mag * jnp.sin(lam_im * step)
    nr = abar_r - 1.0
    ni = abar_i
    den = lam_re * lam_re + lam_im * lam_im
    fr = (nr * lam_re + ni * lam_im) / den
    fi = (ni * lam_re - nr * lam_im) / den
    bbar_r = fr[..., None] * b_re - fi[..., None] * b_im
    bbar_i = fr[..., None] * b_im + fi[..., None] * b_re
    return abar_r, abar_i, bbar_r, bbar_i


def _b_block(bbar_r, bbar_i):
    eye = jnp.eye(CHUNK_GROUPS, dtype=F32)

    def one(b):
        b = b.reshape(N_CHUNKS, CHUNK_GROUPS, SSM_STATE, SSM_GROUP)
        return jnp.einsum("cgnp,gh->cgphn", b, eye).reshape(N_CHUNKS, LANES, CHUNK_STATE)

    return jnp.concatenate([one(bbar_r), one(bbar_i)], axis=2)


def _b_unblock(db):
    eye = jnp.eye(CHUNK_GROUPS, dtype=F32)

    def one(d):
        d = d.reshape(N_CHUNKS, CHUNK_GROUPS, SSM_GROUP, CHUNK_GROUPS, SSM_STATE)
        return jnp.einsum("cgphn,gh->cgnp", d, eye).reshape(SSM_GROUPS, SSM_STATE, SSM_GROUP)

    return one(db[:, :, :CHUNK_STATE]), one(db[:, :, CHUNK_STATE:])


def _c_block(c_re, c_im):
    eye = jnp.eye(CHUNK_GROUPS, dtype=F32)

    def one(c):
        c = c.reshape(N_CHUNKS, CHUNK_GROUPS, SSM_GROUP, SSM_STATE)
        return jnp.einsum("cgpn,gh->cgnhp", c, eye).reshape(N_CHUNKS, CHUNK_STATE, LANES)

    return jnp.concatenate([one(c_re), -one(c_im)], axis=1)


def _c_unblock(dc):
    eye = jnp.eye(CHUNK_GROUPS, dtype=F32)

    def one(d):
        d = d.reshape(N_CHUNKS, CHUNK_GROUPS, SSM_STATE, CHUNK_GROUPS, SSM_GROUP)
        return jnp.einsum("cgnhp,gh->cgpn", d, eye).reshape(SSM_GROUPS, SSM_GROUP, SSM_STATE)

    return one(dc[:, :CHUNK_STATE, :]), -one(dc[:, CHUNK_STATE:, :])


def _attn_fwd(q, k, v, name):
    s = q.shape[0]
    tq = min(1024, s)
    tk = min(1024, s)
    n_k = s // tk

    def body(q_ref, k_ref, v_ref, o_ref, lse_ref, m_ref, acc_ref):
        m_ref[...] = jnp.full_like(m_ref, -jnp.inf)
        acc_ref[...] = jnp.zeros_like(acc_ref)
        qv = q_ref[...]

        def step(j, _):
            rows = pl.ds(pl.multiple_of(j * tk, tk), tk)
            sc = _dot_nt(qv, k_ref[rows, :])
            m_old = m_ref[...]
            m_new = jnp.maximum(m_old, jnp.max(sc, axis=1, keepdims=True))
            p = jnp.exp(sc - m_new)
            alpha = jnp.exp(m_old - m_new)
            acc_ref[...] = alpha * acc_ref[...] + _dot_nn(p, v_ref[rows, :])
            m_ref[...] = m_new
            return 0

        lax.fori_loop(0, n_k, step, 0, unroll=min(4, n_k))
        acc = acc_ref[...]
        l = acc[:, V_HEAD:V_HEAD + 1]
        o_ref[...] = acc / l
        lse = m_ref[...] + jnp.log(l)
        lse_ref[0] = jnp.broadcast_to(lse, (tq, LANES)).T[:SUBLANES, :]

    return pl.pallas_call(
        body, grid=(MLA_HEADS, s // tq),
        in_specs=[pl.BlockSpec((tq, HEAD_PAD), lambda h, i: (i, h)),
                  pl.BlockSpec((s, HEAD_PAD), lambda h, i: (0, h)),
                  pl.BlockSpec((s, HEAD_PAD), lambda h, i: (0, h))],
        out_specs=[pl.BlockSpec((tq, HEAD_PAD), lambda h, i: (i, h)),
                   pl.BlockSpec((1, SUBLANES, tq), lambda h, i: (h, 0, i))],
        out_shape=[jax.ShapeDtypeStruct((s, MLA_HEADS * HEAD_PAD), F32),
                   jax.ShapeDtypeStruct((MLA_HEADS, SUBLANES, s), F32)],
        scratch_shapes=[pltpu.VMEM((tq, 1), F32), pltpu.VMEM((tq, HEAD_PAD), F32)],
        name=name, compiler_params=_cparams(("parallel", "parallel")),
    )(q, k, v)


def _attn_bwd(q, k, v, o, do, lse, name):
    s = q.shape[0]
    tq = min(1024, s)
    tk = min(1024, s)

    def body(q_ref, k_ref, v_ref, o_ref, do_ref, lse_ref, dq_ref, dk_ref, dv_ref):
        j, i = pl.program_id(1), pl.program_id(2)

        @pl.when(jnp.logical_and(j == 0, i == 0))
        def _():
            dq_ref[...] = jnp.zeros_like(dq_ref)

        @pl.when(i == 0)
        def _():
            dk_ref[...] = jnp.zeros_like(dk_ref)
            dv_ref[...] = jnp.zeros_like(dv_ref)

        qv, kv, vv, dov = q_ref[...], k_ref[...], v_ref[...], do_ref[...]
        delta_col = jnp.sum(dov * o_ref[...], axis=1, keepdims=True)
        delta = jnp.broadcast_to(delta_col, (tq, LANES)).T[:1, :]
        st = _dot_nt(kv, qv)
        pt = jnp.exp(st - lse_ref[0, :1, :])
        dv_ref[...] += _dot_nn(pt, dov)
        dpt = _dot_nt(vv, dov)
        dst = pt * (dpt - delta)
        dk_ref[...] += _dot_nn(dst, qv)
        rows = pl.ds(pl.multiple_of(i * tq, tq), tq)
        dq_ref[rows, :] += _dot_tn(dst, kv)

    return pl.pallas_call(
        body, grid=(MLA_HEADS, s // tk, s // tq),
        in_specs=[pl.BlockSpec((tq, HEAD_PAD), lambda h, j, i: (i, h)),
                  pl.BlockSpec((tk, HEAD_PAD), lambda h, j, i: (j, h)),
                  pl.BlockSpec((tk, HEAD_PAD), lambda h, j, i: (j, h)),
                  pl.BlockSpec((tq, HEAD_PAD), lambda h, j, i: (i, h)),
                  pl.BlockSpec((tq, HEAD_PAD), lambda h, j, i: (i, h)),
                  pl.BlockSpec((1, SUBLANES, tq), lambda h, j, i: (h, 0, i))],
        out_specs=[pl.BlockSpec((s, HEAD_PAD), lambda h, j, i: (0, h)),
                   pl.BlockSpec((tk, HEAD_PAD), lambda h, j, i: (j, h)),
                   pl.BlockSpec((tk, HEAD_PAD), lambda h, j, i: (j, h))],
        out_shape=[jax.ShapeDtypeStruct((s, MLA_HEADS * HEAD_PAD), F32)] * 3,
        name=name, compiler_params=_cparams(("parallel", "arbitrary", "arbitrary")),
    )(q, k, v, o, do, lse)


def _pad_w_in(w):
    z = functools.partial(jnp.zeros, dtype=w.dtype)
    return jnp.concatenate([
        w[:, IN_GATE:IN_COLS], w[:, IN_U:IN_CQ], w[:, IN_CKV:IN_KR],
        z((D_MODEL, QK_NOPE)), w[:, IN_KR:IN_GATE], z((D_MODEL, HEAD_PAD - QK_HEAD)),
        z((D_MODEL, P_CQ - P_KR - HEAD_PAD)), w[:, IN_CQ:IN_CKV], z((D_MODEL, P_COLS - P_CQ - Q_LORA))], axis=1)


def _unpad_w_in(d):
    return jnp.concatenate([d[:, P_U:P_CKV], d[:, P_CQ:P_CQ + Q_LORA], d[:, P_CKV:P_KR],
                            d[:, P_KR + QK_NOPE:P_KR + QK_HEAD], d[:, P_GATE:P_U]], axis=1)


def _pad_heads_cols(w, real):
    k = w.shape[0]
    w = w.reshape(k, MLA_HEADS, real)
    return jnp.pad(w, ((0, 0), (0, 0), (0, HEAD_PAD - real))).reshape(k, MLA_HEADS * HEAD_PAD)


def _unpad_heads_cols(d, real):
    k = d.shape[0]
    return d.reshape(k, MLA_HEADS, HEAD_PAD)[:, :, :real].reshape(k, MLA_HEADS * real)


def _pad_kv(w):
    w = w.reshape(KV_LORA, MLA_HEADS, QK_NOPE + V_HEAD)
    kn = jnp.pad(w[:, :, :QK_NOPE], ((0, 0), (0, 0), (0, HEAD_PAD - QK_NOPE)))
    vv = jnp.pad(w[:, :, QK_NOPE:], ((0, 0), (0, 0), (0, HEAD_PAD - V_HEAD)))
    return jnp.concatenate([kn.reshape(KV_LORA, -1), vv.reshape(KV_LORA, -1)], axis=1)


def _unpad_kv(d):
    n = MLA_HEADS * HEAD_PAD
    kn = d[:, :n].reshape(KV_LORA, MLA_HEADS, HEAD_PAD)[:, :, :QK_NOPE]
    vv = d[:, n:].reshape(KV_LORA, MLA_HEADS, HEAD_PAD)[:, :, :V_HEAD]
    return jnp.concatenate([kn, vv], axis=2).reshape(KV_LORA, MLA_HEADS * (QK_NOPE + V_HEAD))


def _pad_out_mla(w):
    w = w.reshape(MLA_HEADS, V_HEAD, D_MODEL)
    return jnp.pad(w, ((0, 0), (0, HEAD_PAD - V_HEAD), (0, 0))).reshape(MLA_HEADS * HEAD_PAD, D_MODEL)


def _unpad_out_mla(d):
    return d.reshape(MLA_HEADS, HEAD_PAD, D_MODEL)[:, :V_HEAD, :].reshape(MLA_HEADS * V_HEAD, D_MODEL)


def _rope_tables(seq):
    half = QK_ROPE // 2
    inv_freq = ROPE_THETA ** (-jnp.arange(half, dtype=F32) / half)
    ang = jnp.arange(seq, dtype=F32)[:, None] * inv_freq[None, :]
    cos, sin = jnp.cos(ang), jnp.sin(ang)
    one, zero = jnp.ones((seq, QK_NOPE), F32), jnp.zeros((seq, half), F32)
    tail1, tail0 = jnp.ones((seq, HEAD_PAD - QK_HEAD), F32), jnp.zeros((seq, HEAD_PAD - QK_HEAD), F32)
    cf = jnp.concatenate([one, cos, cos, tail1], axis=1)
    sa = jnp.concatenate([0.0 * one, -sin, zero, tail0], axis=1)
    sb = jnp.concatenate([0.0 * one, zero, sin, tail0], axis=1)
    return cf, sa, sb


def _prep_layer(w):
    p = {}
    p["w_in_p"] = _pad_w_in(w["w_in"])
    p["w_glu"] = w["w_glu"]
    p["w_out_ssm"] = w["w_out_ssm"]
    p["w_q_p"] = _pad_heads_cols(w["w_q_up"], QK_HEAD)
    p["w_kv_p"] = _pad_kv(w["w_kv_up"])
    p["w_out_mla_p"] = _pad_out_mla(w["w_out_mla"])
    p["w_o"] = w["w_o"]
    p["w_ff1"] = w["w_ff1"]
    p["w_ff2"] = w["w_ff2"]
    p["mix_g"] = w["mix_norm_g"].reshape(1, D_MODEL)
    p["ffn_g"] = w["ffn_norm_g"].reshape(1, D_MODEL)
    p["b_gate"] = w["b_gate"]
    p["b_glu"] = w["b_glu"].reshape(1, SSM_WIDTH)
    p["d"] = w["ssm_d"].reshape(1, SSM_WIDTH)
    p["q_g"] = w["q_norm_g"].reshape(1, Q_LORA)
    p["kv_g"] = w["kv_norm_g"].reshape(1, KV_LORA)
    p["qh_g"] = jnp.pad(w["q_head_g"], (0, HEAD_PAD - QK_HEAD)).reshape(1, HEAD_PAD)
    p["kh_g"] = jnp.pad(w["k_head_g"], (0, HEAD_PAD - QK_HEAD)).reshape(1, HEAD_PAD)
    p["c_blk"] = _c_block(w["ssm_c_re"], w["ssm_c_im"]).astype(BF16)
    zoh, p["zoh_vjp"] = [], []
    for dr in range(2):
        out, vjp = jax.vjp(_zoh, w["ssm_lam_re"][dr], w["ssm_lam_im"][dr], w["ssm_log_step"][dr],
                           w["ssm_b_re"][dr], w["ssm_b_im"][dr])
        zoh.append(out)
        p["zoh_vjp"].append(vjp)
    p["b_blk"] = [_b_block(z[2], z[3]).astype(BF16) for z in zoh]
    p["coef_fwd"] = [_scan_tables(zoh[0][0], zoh[0][1], False), _scan_tables(zoh[1][0], zoh[1][1], True)]
    p["coef_adj"] = [_scan_tables(zoh[0][0], -zoh[0][1], True), _scan_tables(zoh[1][0], -zoh[1][1], False)]
    return p


def _head_prep_fwd(q_raw, kv_raw, proj, tabs, p, li):
    cf, sa, sb = tabs
    scale = QK_HEAD ** -0.5

    def fn(qr, kn, vv, kr, cfv, sav, sbv, gq, gk):
        qo, ko = [], []
        for h in range(MLA_HEADS):
            sl = slice(h * HEAD_PAD, (h + 1) * HEAD_PAD)
            qo.append(_rope(_rms(qr[:, sl], gq, QK_HEAD), cfv, sav, sbv) * scale)
            ko.append(_rope(_rms(kn[:, sl] + kr, gk, QK_HEAD), cfv, sav, sbv))
        lane = lax.broadcasted_iota(jnp.int32, vv.shape, 1)
        return jnp.concatenate(qo, axis=1), jnp.concatenate(ko, axis=1), jnp.where(lane % HEAD_PAD == V_HEAD, 1.0, vv)

    n = MLA_HEADS * HEAD_PAD
    return _rowwise(fn, [(q_raw, n, 0), (kv_raw, n, 0), (kv_raw, n, 1), (proj, HEAD_PAD, P_KR // HEAD_PAD),
                         (cf, HEAD_PAD, 0), (sa, HEAD_PAD, 0), (sb, HEAD_PAD, 0)], [p["qh_g"], p["kh_g"]],
                    [(n, BF16), (n, BF16), (n, BF16)], tm=256, name=f"head_prep_fwd_{li}")


def _head_prep_bwd(dq, dk, dv, q_raw, kv_raw, proj, tabs, p, li):
    cf, sa, sb = tabs
    scale = QK_HEAD ** -0.5

    def fn(dqv, dkv, dvv, qr, kn, kr, cfv, sav, sbv, gq, gk):
        dqo, dko = [], []
        dkr = jnp.zeros_like(kr)
        dgq = jnp.zeros((1, HEAD_PAD), F32)
        dgk = jnp.zeros((1, HEAD_PAD), F32)
        for h in range(MLA_HEADS):
            sl = slice(h * HEAD_PAD, (h + 1) * HEAD_PAD)
            dx, dg = _rms_bwd(qr[:, sl], gq, _rope_t(dqv[:, sl] * scale, cfv, sav, sbv), QK_HEAD)
            dqo.append(dx)
            dgq = dgq + dg
            dx, dg = _rms_bwd(kn[:, sl] + kr, gk, _rope_t(dkv[:, sl], cfv, sav, sbv), QK_HEAD)
            dko.append(dx)
            dkr = dkr + dx
            dgk = dgk + dg
        return jnp.concatenate(dqo, axis=1), jnp.concatenate(dko + [dvv], axis=1), dkr, dgq, dgk

    n = MLA_HEADS * HEAD_PAD
    return _rowwise(fn, [(dq, n, 0), (dk, n, 0), (dv, n, 0), (q_raw, n, 0), (kv_raw, n, 0),
                         (proj, HEAD_PAD, P_KR // HEAD_PAD), (cf, HEAD_PAD, 0), (sa, HEAD_PAD, 0), (sb, HEAD_PAD, 0)],
                    [p["qh_g"], p["kh_g"]], [(n, BF16), (2 * n, BF16), (HEAD_PAD, BF16)],
                    [(HEAD_PAD, F32), (HEAD_PAD, F32)], tm=256, name=f"head_prep_bwd_{li}")


def _layer_fwd(x, p, tabs, li):
    sv = {"x": x}
    h = _rowwise(lambda xv, g: _rms(xv, g, D_MODEL), [(x, D_MODEL, 0)], [p["mix_g"]], [(D_MODEL, BF16)],
                 name=f"mix_norm_{li}")[0]
    proj = _mm_nn(h, p["w_in_p"], name=f"in_proj_{li}")
    sv["h"], sv["proj"] = h, proj
    y_f, xr_f, xi_f = _s5_scan_fwd(proj, p["b_blk"][0], p["c_blk"], p["coef_fwd"][0], False, f"s5_fwd_f_{li}")
    y_b, xr_b, xi_b = _s5_scan_fwd(proj, p["b_blk"][1], p["c_blk"], p["coef_fwd"][1], True, f"s5_fwd_b_{li}")
    sv["states"] = [(xr_f, xi_f), (xr_b, xi_b)]
    y_raw, yg = _rowwise(lambda a, b, u, d: (a + b + d * u, _gelu(a + b + d * u)),
                         [(y_f, SSM_WIDTH, 0), (y_b, SSM_WIDTH, 0), (proj, SSM_WIDTH, P_U // SSM_WIDTH)], [p["d"]],
                         [(SSM_WIDTH, F32), (SSM_WIDTH, BF16)], name=f"s5_gelu_{li}")
    z = _mm_nn(yg, p["w_glu"], name=f"glu_proj_{li}")
    y_ssm = _rowwise(lambda yr, zv, b: _gelu(yr) * _sigmoid(zv + b), [(y_raw, SSM_WIDTH, 0), (z, SSM_WIDTH, 0)],
                     [p["b_glu"]], [(SSM_WIDTH, BF16)], name=f"glu_{li}")[0]
    sv.update(y_raw=y_raw, yg=yg, z=z, y_ssm=y_ssm)
    cqn, ckvn = _rowwise(lambda cq, ckv, gq, gkv: (_rms(cq, gq, Q_LORA), _rms(ckv, gkv, KV_LORA)),
                         [(proj, Q_LORA, P_CQ // Q_LORA), (proj, KV_LORA, P_CKV // KV_LORA)], [p["q_g"], p["kv_g"]],
                         [(Q_LORA, BF16), (KV_LORA, BF16)], name=f"lora_norm_{li}")
    q_raw = _mm_nn(cqn, p["w_q_p"], name=f"q_up_{li}")
    kv_raw = _mm_nn(ckvn, p["w_kv_p"], name=f"kv_up_{li}")
    q, k, v = _head_prep_fwd(q_raw, kv_raw, proj, tabs, p, li)
    o, lse = _attn_fwd(q, k, v, f"attn_fwd_{li}")
    sv.update(cqn=cqn, ckvn=ckvn, q_raw=q_raw, kv_raw=kv_raw, q=q, k=k, v=v, o=o, lse=lse)
    t_ssm = _mm_nn(y_ssm, p["w_out_ssm"], name=f"out_ssm_{li}")
    t_mla = _mm_nn(o, p["w_out_mla_p"], name=f"out_mla_{li}")
    merged = _rowwise(lambda g0, g1, ts, tmv, b: _sigmoid(g0 + b[0:1]) * ts + _sigmoid(g1 + b[1:2]) * tmv,
                      [(proj, D_MODEL, 0), (proj, D_MODEL, 1), (t_ssm, D_MODEL, 0), (t_mla, D_MODEL, 0)],
                      [p["b_gate"]], [(D_MODEL, BF16)], name=f"merge_{li}")[0]
    x1 = _mm_nn(merged, p["w_o"], add=x, name=f"o_proj_{li}")
    sv.update(t_ssm=t_ssm, t_mla=t_mla, merged=merged, x1=x1)
    h2 = _rowwise(lambda xv, g: _rms(xv, g, D_MODEL), [(x1, D_MODEL, 0)], [p["ffn_g"]], [(D_MODEL, BF16)],
                  name=f"ffn_norm_{li}")[0]
    a = _mm_nn(h2, p["w_ff1"], name=f"ff1_{li}")
    r = _rowwise(lambda av: jnp.square(jnp.maximum(av, 0.0)), [(a, D_FF, 0)], [], [(D_FF, BF16)], tm=256,
                 name=f"relu2_{li}")[0]
    x2 = _mm_nn(r, p["w_ff2"], add=x1, name=f"ff2_{li}")
    sv.update(h2=h2, a=a, r=r)
    return x2, sv


def _layer_bwd(dx2, sv, p, tabs, li):
    g = {}
    dx2_b = dx2.astype(BF16)
    dr = _mm_nt(dx2_b, p["w_ff2"], name=f"d_ff2_x_{li}")
    g["w_ff2"] = _mm_tn(sv["r"], dx2_b, name=f"d_ff2_w_{li}")
    da = _rowwise(lambda drv, av: drv * (2.0 * jnp.maximum(av, 0.0)), [(dr, D_FF, 0), (sv["a"], D_FF, 0)], [],
                  [(D_FF, BF16)], tm=256, name=f"d_relu2_{li}")[0]
    dh2 = _mm_nt(da, p["w_ff1"], name=f"d_ff1_x_{li}")
    g["w_ff1"] = _mm_tn(sv["h2"], da, name=f"d_ff1_w_{li}")

    def norm_bwd(xv, dyv, dres, gg):
        dx, dg = _rms_bwd(xv, gg, dyv, D_MODEL)
        return dres + dx, dg

    dx1, dg = _rowwise(norm_bwd, [(sv["x1"], D_MODEL, 0), (dh2, D_MODEL, 0), (dx2, D_MODEL, 0)], [p["ffn_g"]],
                       [(D_MODEL, F32)], [(D_MODEL, F32)], name=f"d_ffn_norm_{li}")
    g["ffn_norm_g"] = dg.reshape(D_MODEL)
    dx1_b = dx1.astype(BF16)
    dmerged = _mm_nt(dx1_b, p["w_o"], name=f"d_o_x_{li}")
    g["w_o"] = _mm_tn(sv["merged"], dx1_b, name=f"d_o_w_{li}")

    def merge_bwd(dm, g0, g1, ts, tmv, b):
        s0, s1 = _sigmoid(g0 + b[0:1]), _sigmoid(g1 + b[1:2])
        dg0, dg1 = dm * ts * s0 * (1.0 - s0), dm * tmv * s1 * (1.0 - s1)
        return (dm * s0, dm * s1, jnp.concatenate([dg0, dg1], axis=1),
                jnp.sum(dg0, axis=0, keepdims=True), jnp.sum(dg1, axis=0, keepdims=True))

    proj = sv["proj"]
    dt_ssm, dt_mla, dgate, db0, db1 = _rowwise(
        merge_bwd, [(dmerged, D_MODEL, 0), (proj, D_MODEL, 0), (proj, D_MODEL, 1), (sv["t_ssm"], D_MODEL, 0),
                    (sv["t_mla"], D_MODEL, 0)], [p["b_gate"]],
        [(D_MODEL, BF16), (D_MODEL, BF16), (2 * D_MODEL, BF16)], [(D_MODEL, F32), (D_MODEL, F32)], tm=256,
        name=f"d_merge_{li}")
    g["b_gate"] = jnp.concatenate([db0, db1], axis=0)
    dy_ssm = _mm_nt(dt_ssm, p["w_out_ssm"], name=f"d_out_ssm_x_{li}")
    g["w_out_ssm"] = _mm_tn(sv["y_ssm"], dt_ssm, name=f"d_out_ssm_w_{li}")
    do = _mm_nt(dt_mla, p["w_out_mla_p"], name=f"d_out_mla_x_{li}")
    g["w_out_mla"] = _unpad_out_mla(_mm_tn(sv["o"], dt_mla, name=f"d_out_mla_w_{li}"))

    def glu_bwd(dyv, yr, zv, b):
        yg = _gelu(yr)
        sg = _sigmoid(zv + b)
        dz = dyv * yg * sg * (1.0 - sg)
        return dz, dyv * sg, jnp.sum(dz, axis=0, keepdims=True)

    dz, dyg_direct, dbglu = _rowwise(glu_bwd, [(dy_ssm, SSM_WIDTH, 0), (sv["y_raw"], SSM_WIDTH, 0), (sv["z"], SSM_WIDTH, 0)],
                                     [p["b_glu"]], [(SSM_WIDTH, BF16), (SSM_WIDTH, F32)], [(SSM_WIDTH, F32)],
                                     name=f"d_glu_{li}")
    g["b_glu"] = dbglu.reshape(SSM_WIDTH)
    dyg_mm = _mm_nt(dz, p["w_glu"], name=f"d_glu_x_{li}")
    g["w_glu"] = _mm_tn(sv["yg"], dz, name=f"d_glu_w_{li}")

    def gelu_bwd(d1, d2, yr, u, d):
        dyr = (d1 + d2) * _gelu_grad(yr)
        return dyr, dyr * d, jnp.sum(dyr * u, axis=0, keepdims=True)

    dy_raw, du_d, dd = _rowwise(gelu_bwd, [(dyg_direct, SSM_WIDTH, 0), (dyg_mm, SSM_WIDTH, 0), (sv["y_raw"], SSM_WIDTH, 0),
                                           (proj, SSM_WIDTH, P_U // SSM_WIDTH)], [p["d"]],
                                [(SSM_WIDTH, BF16), (SSM_WIDTH, F32)], [(SSM_WIDTH, F32)], name=f"d_gelu_{li}")
    g["ssm_d"] = dd.reshape(SSM_GROUPS, SSM_GROUP)
    du_parts, dc_sum = [du_d], None
    zoh_grads = []
    for dr_i in range(2):
        xr, xi = sv["states"][dr_i]
        du_i, da_i, db_i, dc_i = _s5_scan_bwd(dy_raw, proj, xr, xi, p["b_blk"][dr_i], p["c_blk"], p["coef_adj"][dr_i],
                                              dr_i == 0, f"s5_bwd_{'fb'[dr_i]}_{li}")
        du_parts.append(du_i)
        dc_sum = dc_i if dc_sum is None else dc_sum + dc_i
        da_i = jnp.sum(da_i, axis=1)
        dar = da_i[:, :CHUNK_STATE].reshape(SSM_GROUPS, SSM_STATE)
        dai = da_i[:, CHUNK_STATE:].reshape(SSM_GROUPS, SSM_STATE)
        dbr, dbi = _b_unblock(db_i)
        zoh_grads.append(p["zoh_vjp"][dr_i]((dar, dai, dbr, dbi)))
    for k_i, nm in enumerate(("ssm_lam_re", "ssm_lam_im", "ssm_log_step", "ssm_b_re", "ssm_b_im")):
        g[nm] = jnp.stack([zoh_grads[0][k_i], zoh_grads[1][k_i]], axis=0)
    g["ssm_c_re"], g["ssm_c_im"] = _c_unblock(dc_sum)
    du = _rowwise(lambda a, b, c: a + b + c, [(d_, SSM_WIDTH, 0) for d_ in du_parts], [], [(SSM_WIDTH, BF16)],
                  name=f"d_u_sum_{li}")[0]
    dq, dk, dv = _attn_bwd(sv["q"], sv["k"], sv["v"], sv["o"], do, sv["lse"], f"attn_bwd_{li}")
    dq_raw, dkv_raw, dkr, dgq, dgk = _head_prep_bwd(dq, dk, dv, sv["q_raw"], sv["kv_raw"], proj, tabs, p, li)
    g["q_head_g"] = dgq.reshape(HEAD_PAD)[:QK_HEAD]
    g["k_head_g"] = dgk.reshape(HEAD_PAD)[:QK_HEAD]
    dcqn = _mm_nt(dq_raw, p["w_q_p"], name=f"d_q_up_x_{li}")
    g["w_q_up"] = _unpad_heads_cols(_mm_tn(sv["cqn"], dq_raw, name=f"d_q_up_w_{li}"), QK_HEAD)
    dckvn = _mm_nt(dkv_raw, p["w_kv_p"], name=f"d_kv_up_x_{li}")
    g["w_kv_up"] = _unpad_kv(_mm_tn(sv["ckvn"], dkv_raw, name=f"d_kv_up_w_{li}"))

    def lora_bwd(cq, ckv, d1, d2, gq, gkv):
        dx1_, dg1 = _rms_bwd(cq, gq, d1, Q_LORA)
        dx2_, dg2 = _rms_bwd(ckv, gkv, d2, KV_LORA)
        return dx1_, dx2_, dg1, dg2

    dcq, dckv, dgqn, dgkvn = _rowwise(
        lora_bwd, [(proj, Q_LORA, P_CQ // Q_LORA), (proj, KV_LORA, P_CKV // KV_LORA), (dcqn, Q_LORA, 0), (dckvn, KV_LORA, 0)],
        [p["q_g"], p["kv_g"]], [(Q_LORA, BF16), (KV_LORA, BF16)], [(Q_LORA, F32), (KV_LORA, F32)], name=f"d_lora_norm_{li}")
    g["q_norm_g"], g["kv_norm_g"] = dgqn.reshape(Q_LORA), dgkvn.reshape(KV_LORA)
    gap = jnp.zeros((dx2.shape[0], P_CQ - P_KR - HEAD_PAD), BF16)
    tail = jnp.zeros((dx2.shape[0], P_COLS - P_CQ - Q_LORA), BF16)
    dproj = jnp.concatenate([dgate, du, dckv, dkr, gap, dcq, tail], axis=1)
    dh = _mm_nt(dproj, p["w_in_p"], name=f"d_in_x_{li}")
    g["w_in"] = _unpad_w_in(_mm_tn(sv["h"], dproj, name=f"d_in_w_{li}"))
    dx, dg = _rowwise(norm_bwd, [(sv["x"], D_MODEL, 0), (dh, D_MODEL, 0), (dx1, D_MODEL, 0)], [p["mix_g"]],
                      [(D_MODEL, F32)], [(D_MODEL, F32)], name=f"d_mix_norm_{li}")
    g["mix_norm_g"] = dg.reshape(D_MODEL)
    return dx, g


def _local_step(x, target, full):
    tabs = _rope_tables(x.shape[0])
    saved, preps = [], []
    for li in range(DEPTH):
        p = _prep_layer({k: v[li] for k, v in full.items()})
        x, sv = _layer_fwd(x, p, tabs, li)
        saved.append(sv)
        preps.append(p)

    def loss_fn(y, t):
        err = y - t
        return err * (1.0 / D_MODEL), jnp.sum(jnp.sum(err * err, axis=1, keepdims=True), axis=0, keepdims=True) * jnp.ones((1, LANES), F32)

    dx, lsum = _rowwise(loss_fn, [(x, D_MODEL, 0), (target, D_MODEL, 0)], [], [(D_MODEL, F32)], [(LANES, F32)], name="loss")
    loss = 0.5 * lsum[0, 0] * (1.0 / D_MODEL)
    grads = [None] * DEPTH
    for li in reversed(range(DEPTH)):
        dx, grads[li] = _layer_bwd(dx, saved[li], preps[li], tabs, li)
    stacked = {k: jnp.stack([grads[li][k] for li in range(DEPTH)], axis=0) for k in WEIGHTS}
    return loss, dx, stacked


def _exchange(bufs, scatter, name):
    n = len(bufs)

    def body(*refs):
        ins, outs = refs[:n], refs[n:2 * n]
        send_sems, recv_sems, local_sems = refs[2 * n:]
        x, y, c = lax.axis_index("x"), lax.axis_index("y"), lax.axis_index("c")
        me = 4 * x + 2 * y + c
        copies = []
        for t in range(n):
            cp = pltpu.make_async_copy(ins[t].at[me] if scatter else ins[t], outs[t].at[me], local_sems.at[t])
            cp.start()
            copies.append(cp)
        for k in range(1, N_DEV):
            kx, ky, kc = (k >> 2) & 1, (k >> 1) & 1, k & 1
            peer = (x ^ kx, y ^ ky, c ^ kc)
            peer_idx = 4 * peer[0] + 2 * peer[1] + peer[2]
            for t in range(n):
                cp = pltpu.make_async_remote_copy(
                    src_ref=ins[t].at[peer_idx] if scatter else ins[t], dst_ref=outs[t].at[me],
                    send_sem=send_sems.at[t, k], recv_sem=recv_sems.at[t, k], device_id=peer,
                    device_id_type=pl.DeviceIdType.MESH)
                cp.start()
                copies.append(cp)
        for cp in copies:
            cp.wait()

    shapes = [tuple(b.shape[1:]) if scatter else tuple(b.shape) for b in bufs]
    return pl.pallas_call(
        body, out_shape=[jax.ShapeDtypeStruct((N_DEV,) + s, b.dtype) for s, b in zip(shapes, bufs)],
        in_specs=[pl.BlockSpec(memory_space=pl.ANY)] * n, out_specs=[pl.BlockSpec(memory_space=pl.ANY)] * n,
        scratch_shapes=[pltpu.SemaphoreType.DMA((n, N_DEV)), pltpu.SemaphoreType.DMA((n, N_DEV)),
                        pltpu.SemaphoreType.DMA((n,))],
        name=name, compiler_params=pltpu.CompilerParams(has_side_effects=True),
    )(*bufs)


def _adamw(parts, w, m, v, name):
    shape = w.shape
    cols = shape[-1]
    r = math.prod(shape[:-1])
    parts, w, m, v = parts.reshape(N_DEV, r, cols), w.reshape(r, cols), m.reshape(r, cols), v.reshape(r, cols)
    tm = _pick_rows(r, cols)

    def body(p_ref, w_ref, m_ref, v_ref, g_ref, d_ref, nm_ref, nv_ref):
        g = p_ref[0].astype(F32)
        for j in range(1, N_DEV):
            g = g + p_ref[j].astype(F32)
        m_new = ADAM_B1 * m_ref[...] + (1.0 - ADAM_B1) * g
        v_new = ADAM_B2 * v_ref[...] + (1.0 - ADAM_B2) * (g * g)
        m_hat = m_new / (1.0 - ADAM_B1 ** ADAM_STEP)
        v_hat = v_new / (1.0 - ADAM_B2 ** ADAM_STEP)
        g_ref[...] = g
        d_ref[...] = -ADAM_LR * (m_hat / (jnp.sqrt(v_hat) + ADAM_EPS) + ADAM_WD * w_ref[...])
        nm_ref[...] = m_new
        nv_ref[...] = v_new

    spec = pl.BlockSpec((tm, cols), lambda i: (i, 0))
    res = pl.pallas_call(
        body, grid=(r // tm,), in_specs=[pl.BlockSpec((N_DEV, tm, cols), lambda i: (0, i, 0)), spec, spec, spec],
        out_specs=[spec] * 4, out_shape=[jax.ShapeDtypeStruct((r, cols), F32)] * 4, name=name,
        compiler_params=_cparams(("parallel",)),
    )(parts, w, m, v)
    return [a.reshape(shape) for a in res]


def _pick_rows(r, cols):
    for t in (512, 256, 128, 64, 32, 16):
        if r % t == 0 and t * cols <= 512 * 512:
            return t
    return r


def _pack(arrs, dtype, row_mult):
    flat = jnp.concatenate([a.reshape(-1).astype(dtype) for a in arrs])
    n = flat.shape[0]
    per = row_mult * D_MODEL
    total = -(-n // per) * per
    return jnp.pad(flat, (0, total - n)).reshape(total // D_MODEL, D_MODEL)


def _unpack(flat, shapes):
    lead = flat.shape[:-2]
    flat = flat.reshape(lead + (-1,))
    out, off = [], 0
    for shp in shapes:
        n = math.prod(shp)
        out.append(flat[..., off:off + n].reshape(lead + tuple(shp)))
        off += n
    return out


def _to_shards(gfull, axis):
    shp = gfull.shape
    gfull = gfull.reshape(shp[:axis] + (N_DEV, shp[axis] // N_DEV) + shp[axis + 1:])
    return jnp.moveaxis(gfull, axis, 0)


def _from_shards(parts, axis):
    parts = jnp.moveaxis(parts, 0, axis)
    shp = parts.shape
    return parts.reshape(shp[:axis] + (shp[axis] * shp[axis + 1],) + shp[axis + 2:])


def kernel(x, mix_norm_g, w_in, b_gate, ssm_lam_re, ssm_lam_im, ssm_log_step, ssm_b_re, ssm_b_im, ssm_c_re, ssm_c_im, ssm_d, w_glu, b_glu, w_out_ssm, q_norm_g, kv_norm_g, w_q_up, w_kv_up, q_head_g, k_head_g, w_out_mla, w_o, ffn_norm_g, w_ff1, w_ff2, loss_target, m_mix_norm_g, m_w_in, m_b_gate, m_ssm_lam_re, m_ssm_lam_im, m_ssm_log_step, m_ssm_b_re, m_ssm_b_im, m_ssm_c_re, m_ssm_c_im, m_ssm_d, m_w_glu, m_b_glu, m_w_out_ssm, m_q_norm_g, m_kv_norm_g, m_w_q_up, m_w_kv_up, m_q_head_g, m_k_head_g, m_w_out_mla, m_w_o, m_ffn_norm_g, m_w_ff1, m_w_ff2, v_mix_norm_g, v_w_in, v_b_gate, v_ssm_lam_re, v_ssm_lam_im, v_ssm_log_step, v_ssm_b_re, v_ssm_b_im, v_ssm_c_re, v_ssm_c_im, v_ssm_d, v_w_glu, v_b_glu, v_w_out_ssm, v_q_norm_g, v_kv_norm_g, v_w_q_up, v_w_kv_up, v_q_head_g, v_k_head_g, v_w_out_mla, v_w_o, v_ffn_norm_g, v_w_ff1, v_w_ff2):
    args = dict(locals())
    w = {n: args[n] for n in WEIGHTS}
    m = {n: args["m_" + n] for n in WEIGHTS}
    v = {n: args["v_" + n] for n in WEIGHTS}

    send = [w[n] if n == "b_gate" else w[n].astype(BF16) for n in SHARDED]
    gathered = _exchange(send, False, "weight_all_gather")
    full = {n: _from_shards(pt, SHARD_AXIS[n]) for n, pt in zip(SHARDED, gathered)}
    for n in REPLICATED:
        full[n] = w[n]

    loss_part, dx, grads = _local_step(x[0], loss_target[0], full)
    loss = lax.psum(loss_part, ("x", "y", "c"))

    g_send = [_to_shards(grads[n], SHARD_AXIS[n]).astype(BF16) for n in SHARDED]
    g_parts = _exchange(g_send, True, "grad_reduce_scatter")
    r_parts = _exchange([_pack([grads[n] for n in REPLICATED], F32, 8)], False, "grad_all_gather")[0]

    outs = {}
    for n, parts in zip(SHARDED, g_parts):
        for kind, a in zip(("grad", "delta", "new_m", "new_v"), _adamw(parts, w[n], m[n], v[n], "adamw_" + n)):
            outs[kind + "_" + n] = a
    rep_shapes = [w[n].shape for n in REPLICATED]
    res = _adamw(r_parts, _pack([w[n] for n in REPLICATED], F32, 8), _pack([m[n] for n in REPLICATED], F32, 8),
                 _pack([v[n] for n in REPLICATED], F32, 8), "adamw_replicated")
    for kind, flat in zip(("grad", "delta", "new_m", "new_v"), res):
        for n, a in zip(REPLICATED, _unpack(flat, rep_shapes)):
            outs[kind + "_" + n] = a
    return (loss, dx[None], *[outs[k + "_" + n] for k in ("grad", "delta", "new_m", "new_v") for n in WEIGHTS])
```

```python
import functools
import math

import jax
import jax.numpy as jnp
from jax import lax
from jax.experimental import pallas as pl
from jax.experimental.pallas import tpu as pltpu

F32 = jnp.float32
BF16 = jnp.bfloat16
_MXU = jnp.bfloat16

D_MODEL = 1024
DEPTH = 4
SSM_WIDTH = 512
SSM_GROUP = 16
SSM_GROUPS = 32
SSM_STATE = 64
MLA_HEADS = 8
QK_NOPE = 64
QK_ROPE = 32
QK_HEAD = 96
V_HEAD = 64
Q_LORA = 384
KV_LORA = 256
ROPE_THETA = 10000.0
D_FF = 4096
EPS = 1e-6
HEAD_PAD = 128
N_DEV = 8
LANES = 128
SUBLANES = 8
CHUNK_GROUPS = 8
N_CHUNKS = SSM_GROUPS // CHUNK_GROUPS
CHUNK_STATE = CHUNK_GROUPS * SSM_STATE

P_GATE, P_U, P_CKV, P_KR, P_CQ = 0, 2048, 2560, 2816, 3072
P_COLS = 3584
IN_U, IN_CQ, IN_CKV, IN_KR, IN_GATE = 0, 512, 896, 1152, 1184
IN_COLS = 3232

ADAM_LR = 0.001
ADAM_B1 = 0.9
ADAM_B2 = 0.999
ADAM_EPS = 1e-08
ADAM_WD = 0.01
ADAM_STEP = 10

VMEM_LIMIT = 56 * 1024 * 1024
TN_OUT_BLOCK_BYTES = 8 * 1024 * 1024
MXU_WIDTH = 256

SHARDED = ("w_in", "b_gate", "w_glu", "w_out_ssm", "w_q_up", "w_kv_up", "w_out_mla", "w_o", "w_ff1", "w_ff2")
SHARD_AXIS = {"w_in": 2, "b_gate": 2, "w_glu": 1, "w_out_ssm": 2, "w_q_up": 2, "w_kv_up": 2, "w_out_mla": 2,
              "w_o": 1, "w_ff1": 2, "w_ff2": 1}
REPLICATED = ("mix_norm_g", "ssm_lam_re", "ssm_lam_im", "ssm_log_step", "ssm_b_re", "ssm_b_im", "ssm_c_re",
              "ssm_c_im", "ssm_d", "b_glu", "q_norm_g", "kv_norm_g", "q_head_g", "k_head_g", "ffn_norm_g")
WEIGHTS = ("mix_norm_g", "w_in", "b_gate", "ssm_lam_re", "ssm_lam_im", "ssm_log_step", "ssm_b_re", "ssm_b_im",
           "ssm_c_re", "ssm_c_im", "ssm_d", "w_glu", "b_glu", "w_out_ssm", "q_norm_g", "kv_norm_g", "w_q_up",
           "w_kv_up", "q_head_g", "k_head_g", "w_out_mla", "w_o", "ffn_norm_g", "w_ff1", "w_ff2")


def _cparams(sem):
    return pltpu.CompilerParams(dimension_semantics=sem, vmem_limit_bytes=VMEM_LIMIT)


def _dot(a, b, dims):
    return lax.dot_general(a.astype(_MXU), b.astype(_MXU), (dims, ((), ())), preferred_element_type=F32)


def _dot_nn(a, b):
    return _dot(a, b, ((1,), (0,)))


def _dot_nt(a, b):
    return _dot(a, b, ((1,), (1,)))


def _dot_tn(a, b):
    return _dot(a, b, ((0,), (0,)))


def _rowwise(fn, rows, consts, outs, accs=(), *, tm=512, name):
    n_rows = rows[0][0].shape[0]
    tm = min(tm, n_rows)
    n_in = len(rows) + len(consts)
    n_o, n_a = len(outs), len(accs)

    def body(*refs):
        res = fn(*[r[...] for r in refs[:n_in]])
        if not isinstance(res, (tuple, list)):
            res = (res,)
        orefs = refs[n_in:]
        for k in range(n_o):
            orefs[k][...] = res[k].astype(orefs[k].dtype)
        if n_a:
            @pl.when(pl.program_id(0) == 0)
            def _():
                for k in range(n_a):
                    orefs[n_o + k][...] = jnp.zeros_like(orefs[n_o + k])
            for k in range(n_a):
                orefs[n_o + k][...] += res[n_o + k]

    in_specs = [pl.BlockSpec((tm, w), functools.partial(lambda i, j: (i, j), j=j)) for (_, w, j) in rows]
    in_specs += [pl.BlockSpec(c.shape, functools.partial(lambda i, nd: (0,) * nd, nd=c.ndim)) for c in consts]
    out_specs = [pl.BlockSpec((tm, w), lambda i: (i, 0)) for (w, _) in outs]
    out_specs += [pl.BlockSpec((1, w), lambda i: (0, 0)) for (w, _) in accs]
    out_shape = [jax.ShapeDtypeStruct((n_rows, w), dt) for (w, dt) in outs]
    out_shape += [jax.ShapeDtypeStruct((1, w), dt) for (w, dt) in accs]
    res = pl.pallas_call(
        body, grid=(n_rows // tm,), in_specs=in_specs, out_specs=out_specs, out_shape=out_shape, name=name,
        compiler_params=_cparams(("arbitrary",) if n_a else ("parallel",)),
    )(*[r[0] for r in rows], *consts)
    return res


def _pick(n, cap):
    if n <= cap:
        return n
    for unit in (MXU_WIDTH, LANES):
        best = 0
        for t in range(unit, cap + 1, unit):
            if n % t == 0:
                best = t
        if best:
            return best
    return n


def _mm(a, b, transpose_b, extras, epilogue, out_dtypes, name):
    m, k = a.shape
    n = b.shape[0] if transpose_b else b.shape[1]
    tm, tn = min(512, m), _pick(n, 1024)
    n_in = 2 + len(extras)

    def body(*refs):
        acc = (_dot_nt if transpose_b else _dot_nn)(refs[0][...], refs[1][...])
        res = epilogue(acc, *[r[...] for r in refs[2:n_in]]) if epilogue is not None else acc
        if not isinstance(res, (tuple, list)):
            res = (res,)
        for o_ref, val in zip(refs[n_in:], res):
            o_ref[...] = val.astype(o_ref.dtype)

    blk = pl.BlockSpec((tm, tn), lambda j, i: (i, j))
    b_spec = pl.BlockSpec((tn, k), lambda j, i: (j, 0)) if transpose_b else pl.BlockSpec((k, tn), lambda j, i: (0, j))
    res = pl.pallas_call(
        body, grid=(n // tn, m // tm),
        in_specs=[pl.BlockSpec((tm, k), lambda j, i: (i, 0)), b_spec] + [blk] * len(extras),
        out_specs=[blk] * len(out_dtypes), out_shape=[jax.ShapeDtypeStruct((m, n), dt) for dt in out_dtypes],
        name=name, compiler_params=_cparams(("parallel", "parallel")),
    )(a, b, *extras)
    return res[0] if len(out_dtypes) == 1 else res


def _mm_nn(a, b, *, add=None, extras=(), epilogue=None, out_dtypes=(F32,), name):
    if add is not None:
        extras, epilogue = (add,), (lambda acc, r: acc + r)
    return _mm(a, b, False, tuple(extras), epilogue, out_dtypes, name)


def _mm_nt(a, b, *, extras=(), epilogue=None, out_dtypes=(F32,), name):
    return _mm(a, b, True, tuple(extras), epilogue, out_dtypes, name)


def _mm_tn(a, b, *, a_cols=None, name):
    s = a.shape[0]
    n = b.shape[1]
    mw, mj = (a.shape[1], 0) if a_cols is None else a_cols
    ts = min(512, s)
    tm = _pick(mw, 1024)
    tn = _pick(n, max(1024, TN_OUT_BLOCK_BYTES // (4 * tm)))
    n_mb = mw // tm

    def body(a_ref, b_ref, o_ref):
        @pl.when(pl.program_id(2) == 0)
        def _():
            o_ref[...] = jnp.zeros_like(o_ref)
        o_ref[...] += _dot_tn(a_ref[...], b_ref[...])

    return pl.pallas_call(
        body, grid=(n_mb, n // tn, s // ts),
        in_specs=[pl.BlockSpec((ts, tm), lambda i, j, t: (t, mj * n_mb + i)), pl.BlockSpec((ts, tn), lambda i, j, t: (t, j))],
        out_specs=pl.BlockSpec((tm, tn), lambda i, j, t: (i, j)),
        out_shape=jax.ShapeDtypeStruct((mw, n), F32), name=name,
        compiler_params=_cparams(("parallel", "parallel", "arbitrary")),
    )(a, b)


def _rms(x, g, n):
    r = lax.rsqrt(jnp.sum(x * x, axis=-1, keepdims=True) * (1.0 / n) + EPS)
    return x * r * g


def _rms_bwd(x, g, dy, n):
    r = lax.rsqrt(jnp.sum(x * x, axis=-1, keepdims=True) * (1.0 / n) + EPS)
    xr = x * r
    dyg = dy * g
    dx = r * dyg - xr * (r * r) * (jnp.sum(dyg * x, axis=-1, keepdims=True) * (1.0 / n))
    return dx, jnp.sum(dy * xr, axis=0, keepdims=True)


def _gelu(x):
    c = math.sqrt(2.0 / math.pi)
    return 0.5 * x * (1.0 + jnp.tanh(c * (x + 0.044715 * (x * x * x))))


def _gelu_grad(x):
    c = math.sqrt(2.0 / math.pi)
    t = jnp.tanh(c * (x + 0.044715 * (x * x * x)))
    return 0.5 * (1.0 + t) + 0.5 * x * (1.0 - t * t) * (c * (1.0 + 3.0 * 0.044715 * (x * x)))


def _sigmoid(x):
    return 1.0 / (1.0 + jnp.exp(-x))


def _rope(x, cf, sa, sb):
    return x * cf + pltpu.roll(x, HEAD_PAD - QK_ROPE // 2, 1) * sa + pltpu.roll(x, QK_ROPE // 2, 1) * sb


def _rope_t(d, cf, sa, sb):
    return d * cf + pltpu.roll(d * sa, QK_ROPE // 2, 1) + pltpu.roll(d * sb, HEAD_PAD - QK_ROPE // 2, 1)


def _scan_tables(ar, ai, reverse):
    ar = ar.reshape(N_CHUNKS, CHUNK_STATE)
    ai = ai.reshape(N_CHUNKS, CHUNK_STATE)
    pr, pi = [ar], [ai]
    for _ in range(SUBLANES - 1):
        pr, pi = pr + [pr[-1] * ar - pi[-1] * ai], pi + [pr[-1] * ai + pi[-1] * ar]
    row = jnp.arange(SUBLANES)[None, :, None]
    tiles = []
    for k in (1, 2, 4):
        mask = (row <= SUBLANES - 1 - k) if reverse else (row >= k)
        tiles.append(jnp.where(mask, pr[k - 1][:, None, :], 0.0))
        tiles.append(jnp.where(mask, pi[k - 1][:, None, :], 0.0))
    order = list(range(SUBLANES))[::-1] if reverse else list(range(SUBLANES))
    tiles.append(jnp.stack([pr[j] for j in order], axis=1))
    tiles.append(jnp.stack([pi[j] for j in order], axis=1))
    return jnp.stack(tiles, axis=1).astype(F32)


def _slab_scan(xr, xi, coef, carry_r, carry_i, reverse):
    for idx, k in enumerate((1, 2, 4)):
        sh = SUBLANES - k if reverse else k
        sr, si = pltpu.roll(xr, sh, 0), pltpu.roll(xi, sh, 0)
        cr, ci = coef[2 * idx], coef[2 * idx + 1]
        xr, xi = xr + cr * sr - ci * si, xi + cr * si + ci * sr
    pr, pi = coef[6], coef[7]
    xr = xr + pr * carry_r - pi * carry_i
    xi = xi + pr * carry_i + pi * carry_r
    return xr, xi


def _s5_scan_fwd(proj, b_blk, c_blk, coef, reverse, name):
    s = proj.shape[0]
    t_blk = min(512, s)
    n_t = s // t_blk
    n_slab = t_blk // SUBLANES
    last = 0 if reverse else SUBLANES - 1

    def tmap(t):
        return n_t - 1 - t if reverse else t

    def body(u_ref, b_ref, c_ref, coef_ref, y_ref, xr_ref, xi_ref, carry_ref):
        @pl.when(pl.program_id(1) == 0)
        def _():
            carry_ref[...] = jnp.zeros_like(carry_ref)
        bu = _dot_nn(u_ref[...], b_ref[0])
        xr_ref[...] = bu[:, :CHUNK_STATE]
        xi_ref[...] = bu[:, CHUNK_STATE:]
        coef_v = [coef_ref[0, k] for k in range(8)]

        def slab(i, carry):
            sl = (n_slab - 1 - i) if reverse else i
            rows = pl.ds(pl.multiple_of(sl * SUBLANES, SUBLANES), SUBLANES)
            xr, xi = _slab_scan(xr_ref[rows, :], xi_ref[rows, :], coef_v, carry[0], carry[1], reverse)
            xr_ref[rows, :] = xr
            xi_ref[rows, :] = xi
            return (jnp.broadcast_to(xr[last:last + 1, :], xr.shape), jnp.broadcast_to(xi[last:last + 1, :], xi.shape))

        cr, ci = lax.fori_loop(0, n_slab, slab, (carry_ref[0], carry_ref[1]))
        carry_ref[0] = cr
        carry_ref[1] = ci
        y_ref[...] = _dot_nn(xr_ref[...], c_ref[0, :CHUNK_STATE, :]) + _dot_nn(xi_ref[...], c_ref[0, CHUNK_STATE:, :])

    u_blk0 = P_U // LANES
    return pl.pallas_call(
        body, grid=(N_CHUNKS, n_t),
        in_specs=[pl.BlockSpec((t_blk, LANES), lambda c, t: (tmap(t), u_blk0 + c)),
                  pl.BlockSpec((1, LANES, 2 * CHUNK_STATE), lambda c, t: (c, 0, 0)),
                  pl.BlockSpec((1, 2 * CHUNK_STATE, LANES), lambda c, t: (c, 0, 0)),
                  pl.BlockSpec((1, 8, SUBLANES, CHUNK_STATE), lambda c, t: (c, 0, 0, 0))],
        out_specs=[pl.BlockSpec((t_blk, LANES), lambda c, t: (tmap(t), c)),
                   pl.BlockSpec((t_blk, CHUNK_STATE), lambda c, t: (tmap(t), c)),
                   pl.BlockSpec((t_blk, CHUNK_STATE), lambda c, t: (tmap(t), c))],
        out_shape=[jax.ShapeDtypeStruct((s, SSM_WIDTH), F32),
                   jax.ShapeDtypeStruct((s, N_CHUNKS * CHUNK_STATE), F32),
                   jax.ShapeDtypeStruct((s, N_CHUNKS * CHUNK_STATE), F32)],
        scratch_shapes=[pltpu.VMEM((2, SUBLANES, CHUNK_STATE), F32)],
        name=name, compiler_params=_cparams(("parallel", "arbitrary")),
    )(proj, b_blk, c_blk, coef)


def _s5_scan_bwd(dy, proj, x_re, x_im, b_blk, c_blk, coef, reverse, name):
    s = dy.shape[0]
    t_blk = min(512, s)
    n_t = s // t_blk
    n_slab = t_blk // SUBLANES
    last = 0 if reverse else SUBLANES - 1
    first = SUBLANES - 1 if reverse else 0

    def tmap(t):
        return n_t - 1 - t if reverse else t

    def body(dy_ref, u_ref, xr_ref, xi_ref, b_ref, c_ref, coef_ref, du_ref, da_ref, db_ref, dc_ref,
             carry_ref, lr_ref, li_ref):
        @pl.when(pl.program_id(1) == 0)
        def _():
            carry_ref[...] = jnp.zeros_like(carry_ref)
            da_ref[...] = jnp.zeros_like(da_ref)
            db_ref[...] = jnp.zeros_like(db_ref)
            dc_ref[...] = jnp.zeros_like(dc_ref)
        g = _dot_nt(dy_ref[...], c_ref[0])
        lr_ref[...] = g[:, :CHUNK_STATE]
        li_ref[...] = g[:, CHUNK_STATE:]
        coef_v = [coef_ref[0, k] for k in range(8)]
        row = lax.broadcasted_iota(jnp.int32, (SUBLANES, CHUNK_STATE), 0)
        sh_prev = SUBLANES - 1 if reverse else 1

        def slab(i, carry):
            cr, ci, ar_acc, ai_acc = carry
            sl = (n_slab - 1 - i) if reverse else i
            rows = pl.ds(pl.multiple_of(sl * SUBLANES, SUBLANES), SUBLANES)
            lr, li = _slab_scan(lr_ref[rows, :], li_ref[rows, :], coef_v, cr, ci, reverse)
            lr_ref[rows, :] = lr
            li_ref[rows, :] = li
            pr = jnp.where(row == first, cr, pltpu.roll(lr, sh_prev, 0))
            pi = jnp.where(row == first, ci, pltpu.roll(li, sh_prev, 0))
            xr, xi = xr_ref[rows, :], xi_ref[rows, :]
            ar_acc = ar_acc + xr * pr + xi * pi
            ai_acc = ai_acc + xr * pi - xi * pr
            return (jnp.broadcast_to(lr[last:last + 1, :], lr.shape), jnp.broadcast_to(li[last:last + 1, :], li.shape),
                    ar_acc, ai_acc)

        zero = jnp.zeros((SUBLANES, CHUNK_STATE), F32)
        cr, ci, ar_acc, ai_acc = lax.fori_loop(0, n_slab, slab, (carry_ref[0], carry_ref[1], zero, zero))
        carry_ref[0] = cr
        carry_ref[1] = ci
        da_ref[0, :, :CHUNK_STATE] += ar_acc
        da_ref[0, :, CHUNK_STATE:] += ai_acc
        lam_r, lam_i = lr_ref[...], li_ref[...]
        u = u_ref[...]
        du_ref[...] = _dot_nt(lam_r, b_ref[0, :, :CHUNK_STATE]) + _dot_nt(lam_i, b_ref[0, :, CHUNK_STATE:])
        db_ref[0, :, :CHUNK_STATE] += _dot_tn(u, lam_r)
        db_ref[0, :, CHUNK_STATE:] += _dot_tn(u, lam_i)
        dyv = dy_ref[...]
        dc_ref[0, :CHUNK_STATE, :] += _dot_tn(xr_ref[...], dyv)
        dc_ref[0, CHUNK_STATE:, :] += _dot_tn(xi_ref[...], dyv)

    u_blk0 = P_U // LANES
    return pl.pallas_call(
        body, grid=(N_CHUNKS, n_t),
        in_specs=[pl.BlockSpec((t_blk, LANES), lambda c, t: (tmap(t), c)),
                  pl.BlockSpec((t_blk, LANES), lambda c, t: (tmap(t), u_blk0 + c)),
                  pl.BlockSpec((t_blk, CHUNK_STATE), lambda c, t: (tmap(t), c)),
                  pl.BlockSpec((t_blk, CHUNK_STATE), lambda c, t: (tmap(t), c)),
                  pl.BlockSpec((1, LANES, 2 * CHUNK_STATE), lambda c, t: (c, 0, 0)),
                  pl.BlockSpec((1, 2 * CHUNK_STATE, LANES), lambda c, t: (c, 0, 0)),
                  pl.BlockSpec((1, 8, SUBLANES, CHUNK_STATE), lambda c, t: (c, 0, 0, 0))],
        out_specs=[pl.BlockSpec((t_blk, LANES), lambda c, t: (tmap(t), c)),
                   pl.BlockSpec((1, SUBLANES, 2 * CHUNK_STATE), lambda c, t: (c, 0, 0)),
                   pl.BlockSpec((1, LANES, 2 * CHUNK_STATE), lambda c, t: (c, 0, 0)),
                   pl.BlockSpec((1, 2 * CHUNK_STATE, LANES), lambda c, t: (c, 0, 0))],
        out_shape=[jax.ShapeDtypeStruct((s, SSM_WIDTH), F32),
                   jax.ShapeDtypeStruct((N_CHUNKS, SUBLANES, 2 * CHUNK_STATE), F32),
                   jax.ShapeDtypeStruct((N_CHUNKS, LANES, 2 * CHUNK_STATE), F32),
                   jax.ShapeDtypeStruct((N_CHUNKS, 2 * CHUNK_STATE, LANES), F32)],
        scratch_shapes=[pltpu.VMEM((2, SUBLANES, CHUNK_STATE), F32), pltpu.VMEM((t_blk, CHUNK_STATE), F32),
                        pltpu.VMEM((t_blk, CHUNK_STATE), F32)],
        name=name, compiler_params=_cparams(("parallel", "arbitrary")),
    )(dy, proj, x_re, x_im, b_blk, c_blk, coef)


def _zoh(lam_re, lam_im, log_step, b_re, b_im):
    step = jnp.exp(log_step)[:, None]
    mag = jnp.exp(lam_re * step)
    abar_r = mag * jnp.cos(lam_im * step)
    abar_i = mag * jnp.sin(lam_im * step)
    nr = abar_r - 1.0
    ni = abar_i
    den = lam_re * lam_re + lam_im * lam_im
    fr = (nr * lam_re + ni * lam_im) / den
    fi = (ni * lam_re - nr * lam_im) / den
    bbar_r = fr[..., None] * b_re - fi[..., None] * b_im
    bbar_i = fr[..., None] * b_im + fi[..., None] * b_re
    return abar_r, abar_i, bbar_r, bbar_i


def _b_block(bbar_r, bbar_i):
    eye = jnp.eye(CHUNK_GROUPS, dtype=F32)

    def one(b):
        b = b.reshape(N_CHUNKS, CHUNK_GROUPS, SSM_STATE, SSM_GROUP)
        return jnp.einsum("cgnp,gh->cgphn", b, eye).reshape(N_CHUNKS, LANES, CHUNK_STATE)

    return jnp.concatenate([one(bbar_r), one(bbar_i)], axis=2)


def _b_unblock(db):
    eye = jnp.eye(CHUNK_GROUPS, dtype=F32)

    def one(d):
        d = d.reshape(N_CHUNKS, CHUNK_GROUPS, SSM_GROUP, CHUNK_GROUPS, SSM_STATE)
        return jnp.einsum("cgphn,gh->cgnp", d, eye).reshape(SSM_GROUPS, SSM_STATE, SSM_GROUP)

    return one(db[:, :, :CHUNK_STATE]), one(db[:, :, CHUNK_STATE:])


def _c_block(c_re, c_im):
    eye = jnp.eye(CHUNK_GROUPS, dtype=F32)

    def one(c):
        c = c.reshape(N_CHUNKS, CHUNK_GROUPS, SSM_GROUP, SSM_STATE)
        return jnp.einsum("cgpn,gh->cgnhp", c, eye).reshape(N_CHUNKS, CHUNK_STATE, LANES)

    return jnp.concatenate([one(c_re), -one(c_im)], axis=1)


def _c_unblock(dc):
    eye = jnp.eye(CHUNK_GROUPS, dtype=F32)

    def one(d):
        d = d.reshape(N_CHUNKS, CHUNK_GROUPS, SSM_STATE, CHUNK_GROUPS, SSM_GROUP)
        return jnp.einsum("cgnhp,gh->cgpn", d, eye).reshape(SSM_GROUPS, SSM_GROUP, SSM_STATE)

    return one(dc[:, :CHUNK_STATE, :]), -one(dc[:, CHUNK_STATE:, :])


def _exchange_copies(ins, outs, send_sems, recv_sems, local_sems, scatter):
    x, y, c = lax.axis_index("x"), lax.axis_index("y"), lax.axis_index("c")
    me = 4 * x + 2 * y + c
    copies = [pltpu.make_async_copy(ins[t].at[me] if scatter else ins[t], outs[t].at[me], local_sems.at[t])
              for t in range(len(ins))]
    for k in range(1, N_DEV):
        peer = (x ^ ((k >> 2) & 1), y ^ ((k >> 1) & 1), c ^ (k & 1))
        peer_idx = 4 * peer[0] + 2 * peer[1] + peer[2]
        for t in range(len(ins)):
            copies.append(pltpu.make_async_remote_copy(
                src_ref=ins[t].at[peer_idx] if scatter else ins[t], dst_ref=outs[t].at[me],
                send_sem=send_sems.at[t, k], recv_sem=recv_sems.at[t, k], device_id=peer,
                device_id_type=pl.DeviceIdType.MESH))
    return copies


def _call_with_exchange(body, *, grid, in_specs, out_specs, out_shape, scratch_shapes, args, semantics, name,
                        exchange=None, scatter=False):
    if exchange is None:
        return pl.pallas_call(body, grid=grid, in_specs=in_specs, out_specs=out_specs, out_shape=out_shape,
                              scratch_shapes=scratch_shapes, name=name, compiler_params=_cparams(semantics))(*args)
    n, n_in, n_out, n_scr = len(exchange), len(in_specs), len(out_specs), len(scratch_shapes)

    def wrapped(*refs):
        ins, refs = refs[:n_in], refs[n_in:]
        c_ins, refs = refs[:n], refs[n:]
        outs, refs = refs[:n_out], refs[n_out:]
        c_outs, refs = refs[:n], refs[n:]
        scr, sems = refs[:n_scr], refs[n_scr:]
        ids = [pl.program_id(a) for a in range(len(grid))]
        first = functools.reduce(jnp.logical_and, [i == 0 for i in ids])
        last = functools.reduce(jnp.logical_and, [i == g - 1 for i, g in zip(ids, grid)])

        @pl.when(first)
        def _():
            for cp in _exchange_copies(c_ins, c_outs, *sems, scatter):
                cp.start()

        body(*ins, *outs, *scr)

        @pl.when(last)
        def _():
            for cp in _exchange_copies(c_ins, c_outs, *sems, scatter):
                cp.wait()

    shapes = [tuple(b.shape[1:]) if scatter else tuple(b.shape) for b in exchange]
    hbm = pl.BlockSpec(memory_space=pl.ANY)
    res = pl.pallas_call(
        wrapped, grid=grid, in_specs=list(in_specs) + [hbm] * n, out_specs=list(out_specs) + [hbm] * n,
        out_shape=list(out_shape) + [jax.ShapeDtypeStruct((N_DEV,) + s, b.dtype) for s, b in zip(shapes, exchange)],
        scratch_shapes=list(scratch_shapes) + [pltpu.SemaphoreType.DMA((n, N_DEV)), pltpu.SemaphoreType.DMA((n, N_DEV)),
                                              pltpu.SemaphoreType.DMA((n,))],
        name=name, compiler_params=pltpu.CompilerParams(dimension_semantics=("arbitrary",) * len(grid),
                                                        vmem_limit_bytes=VMEM_LIMIT, has_side_effects=True),
    )(*args, *exchange)
    return res


def _attn_fwd(q, k, v, name, exchange=None):
    s = q.shape[0]
    tq = min(1024, s)
    tk = min(1024, s)
    n_k = s // tk

    def body(q_ref, k_ref, v_ref, o_ref, lse_ref, m_ref, acc_ref):
        m_ref[...] = jnp.full_like(m_ref, -jnp.inf)
        acc_ref[...] = jnp.zeros_like(acc_ref)
        qv = q_ref[...]

        def step(j, _):
            rows = pl.ds(pl.multiple_of(j * tk, tk), tk)
            sc = _dot_nt(qv, k_ref[rows, :])
            m_old = m_ref[...]
            m_new = jnp.maximum(m_old, jnp.max(sc, axis=1, keepdims=True))
            p = jnp.exp(sc - m_new)
            alpha = jnp.exp(m_old - m_new)
            acc_ref[...] = alpha * acc_ref[...] + _dot_nn(p, v_ref[rows, :])
            m_ref[...] = m_new
            return 0

        lax.fori_loop(0, n_k, step, 0, unroll=min(4, n_k))
        acc = acc_ref[...]
        l = acc[:, V_HEAD:V_HEAD + 1]
        o_ref[...] = acc / l
        lse = m_ref[...] + jnp.log(l)
        lse_ref[0] = jnp.broadcast_to(lse, (tq, LANES)).T[:SUBLANES, :]

    return _call_with_exchange(
        body, grid=(MLA_HEADS, s // tq),
        in_specs=[pl.BlockSpec((tq, HEAD_PAD), lambda h, i: (i, h)),
                  pl.BlockSpec((s, HEAD_PAD), lambda h, i: (0, h)),
                  pl.BlockSpec((s, HEAD_PAD), lambda h, i: (0, h))],
        out_specs=[pl.BlockSpec((tq, HEAD_PAD), lambda h, i: (i, h)),
                   pl.BlockSpec((1, SUBLANES, tq), lambda h, i: (h, 0, i))],
        out_shape=[jax.ShapeDtypeStruct((s, MLA_HEADS * HEAD_PAD), F32),
                   jax.ShapeDtypeStruct((MLA_HEADS, SUBLANES, s), F32)],
        scratch_shapes=[pltpu.VMEM((tq, 1), F32), pltpu.VMEM((tq, HEAD_PAD), F32)],
        args=(q, k, v), semantics=("parallel", "parallel"), name=name, exchange=exchange, scatter=False)


def _attn_bwd(q, k, v, o, do, lse, name, exchange=None):
    s = q.shape[0]
    tq = min(1024, s)
    tk = min(1024, s)

    def body(q_ref, k_ref, v_ref, o_ref, do_ref, lse_ref, dq_ref, dk_ref, dv_ref):
        j, i = pl.program_id(1), pl.program_id(2)

        @pl.when(jnp.logical_and(j == 0, i == 0))
        def _():
            dq_ref[...] = jnp.zeros_like(dq_ref)

        @pl.when(i == 0)
        def _():
            dk_ref[...] = jnp.zeros_like(dk_ref)
            dv_ref[...] = jnp.zeros_like(dv_ref)

        qv, kv, vv, dov = q_ref[...], k_ref[...], v_ref[...], do_ref[...]
        delta_col = jnp.sum(dov * o_ref[...], axis=1, keepdims=True)
        delta = jnp.broadcast_to(delta_col, (tq, LANES)).T[:1, :]
        st = _dot_nt(kv, qv)
        pt = jnp.exp(st - lse_ref[0, :1, :])
        dv_ref[...] += _dot_nn(pt, dov)
        dpt = _dot_nt(vv, dov)
        dst = pt * (dpt - delta)
        dk_ref[...] += _dot_nn(dst, qv)
        rows = pl.ds(pl.multiple_of(i * tq, tq), tq)
        dq_ref[rows, :] += _dot_tn(dst, kv)

    return _call_with_exchange(
        body, grid=(MLA_HEADS, s // tk, s // tq),
        in_specs=[pl.BlockSpec((tq, HEAD_PAD), lambda h, j, i: (i, h)),
                  pl.BlockSpec((tk, HEAD_PAD), lambda h, j, i: (j, h)),
                  pl.BlockSpec((tk, HEAD_PAD), lambda h, j, i: (j, h)),
                  pl.BlockSpec((tq, HEAD_PAD), lambda h, j, i: (i, h)),
                  pl.BlockSpec((tq, HEAD_PAD), lambda h, j, i: (i, h)),
                  pl.BlockSpec((1, SUBLANES, tq), lambda h, j, i: (h, 0, i))],
        out_specs=[pl.BlockSpec((s, HEAD_PAD), lambda h, j, i: (0, h)),
                   pl.BlockSpec((tk, HEAD_PAD), lambda h, j, i: (j, h)),
                   pl.BlockSpec((tk, HEAD_PAD), lambda h, j, i: (j, h))],
        out_shape=[jax.ShapeDtypeStruct((s, MLA_HEADS * HEAD_PAD), F32)] * 3, scratch_shapes=[],
        args=(q, k, v, o, do, lse), semantics=("parallel", "arbitrary", "arbitrary"), name=name,
        exchange=exchange, scatter=True)


def _pad_w_in(w):
    z = functools.partial(jnp.zeros, dtype=w.dtype)
    return jnp.concatenate([
        w[:, IN_GATE:IN_COLS], w[:, IN_U:IN_CQ], w[:, IN_CKV:IN_KR],
        z((D_MODEL, QK_NOPE)), w[:, IN_KR:IN_GATE], z((D_MODEL, HEAD_PAD - QK_HEAD)),
        z((D_MODEL, P_CQ - P_KR - HEAD_PAD)), w[:, IN_CQ:IN_CKV], z((D_MODEL, P_COLS - P_CQ - Q_LORA))], axis=1)


def _unpad_w_in(d):
    return jnp.concatenate([d[:, P_U:P_CKV], d[:, P_CQ:P_CQ + Q_LORA], d[:, P_CKV:P_KR],
                            d[:, P_KR + QK_NOPE:P_KR + QK_HEAD], d[:, P_GATE:P_U]], axis=1)


def _pad_heads_cols(w, real):
    k = w.shape[0]
    w = w.reshape(k, MLA_HEADS, real)
    return jnp.pad(w, ((0, 0), (0, 0), (0, HEAD_PAD - real))).reshape(k, MLA_HEADS * HEAD_PAD)


def _unpad_heads_cols(d, real):
    k = d.shape[0]
    return d.reshape(k, MLA_HEADS, HEAD_PAD)[:, :, :real].reshape(k, MLA_HEADS * real)


def _pad_kv(w):
    w = w.reshape(KV_LORA, MLA_HEADS, QK_NOPE + V_HEAD)
    kn = jnp.pad(w[:, :, :QK_NOPE], ((0, 0), (0, 0), (0, HEAD_PAD - QK_NOPE)))
    vv = jnp.pad(w[:, :, QK_NOPE:], ((0, 0), (0, 0), (0, HEAD_PAD - V_HEAD)))
    return jnp.concatenate([kn.reshape(KV_LORA, -1), vv.reshape(KV_LORA, -1)], axis=1)


def _unpad_kv(d):
    n = MLA_HEADS * HEAD_PAD
    kn = d[:, :n].reshape(KV_LORA, MLA_HEADS, HEAD_PAD)[:, :, :QK_NOPE]
    vv = d[:, n:].reshape(KV_LORA, MLA_HEADS, HEAD_PAD)[:, :, :V_HEAD]
    return jnp.concatenate([kn, vv], axis=2).reshape(KV_LORA, MLA_HEADS * (QK_NOPE + V_HEAD))


def _pad_out_mla(w):
    w = w.reshape(MLA_HEADS, V_HEAD, D_MODEL)
    return jnp.pad(w, ((0, 0), (0, HEAD_PAD - V_HEAD), (0, 0))).reshape(MLA_HEADS * HEAD_PAD, D_MODEL)


def _unpad_out_mla(d):
    return d.reshape(MLA_HEADS, HEAD_PAD, D_MODEL)[:, :V_HEAD, :].reshape(MLA_HEADS * V_HEAD, D_MODEL)


def _rope_tables(seq):
    half = QK_ROPE // 2
    inv_freq = ROPE_THETA ** (-jnp.arange(half, dtype=F32) / half)
    ang = jnp.arange(seq, dtype=F32)[:, None] * inv_freq[None, :]
    cos, sin = jnp.cos(ang), jnp.sin(ang)
    one, zero = jnp.ones((seq, QK_NOPE), F32), jnp.zeros((seq, half), F32)
    tail1, tail0 = jnp.ones((seq, HEAD_PAD - QK_HEAD), F32), jnp.zeros((seq, HEAD_PAD - QK_HEAD), F32)
    cf = jnp.concatenate([one, cos, cos, tail1], axis=1)
    sa = jnp.concatenate([0.0 * one, -sin, zero, tail0], axis=1)
    sb = jnp.concatenate([0.0 * one, zero, sin, tail0], axis=1)
    return cf, sa, sb


def _prep_layer(w):
    p = {}
    p["w_in_p"] = _pad_w_in(w["w_in"])
    p["w_glu"] = w["w_glu"]
    p["w_out_ssm"] = w["w_out_ssm"]
    p["w_q_p"] = _pad_heads_cols(w["w_q_up"], QK_HEAD)
    p["w_kv_p"] = _pad_kv(w["w_kv_up"])
    p["w_out_mla_p"] = _pad_out_mla(w["w_out_mla"])
    p["w_o"] = w["w_o"]
    p["w_ff1"] = w["w_ff1"]
    p["w_ff2"] = w["w_ff2"]
    p["mix_g"] = w["mix_norm_g"].reshape(1, D_MODEL)
    p["ffn_g"] = w["ffn_norm_g"].reshape(1, D_MODEL)
    p["b_gate"] = w["b_gate"]
    p["b_glu"] = w["b_glu"].reshape(1, SSM_WIDTH)
    p["d"] = w["ssm_d"].reshape(1, SSM_WIDTH)
    p["q_g"] = w["q_norm_g"].reshape(1, Q_LORA)
    p["kv_g"] = w["kv_norm_g"].reshape(1, KV_LORA)
    p["qh_g"] = jnp.pad(w["q_head_g"], (0, HEAD_PAD - QK_HEAD)).reshape(1, HEAD_PAD)
    p["kh_g"] = jnp.pad(w["k_head_g"], (0, HEAD_PAD - QK_HEAD)).reshape(1, HEAD_PAD)
    p["c_blk"] = _c_block(w["ssm_c_re"], w["ssm_c_im"]).astype(BF16)
    zoh, p["zoh_vjp"] = [], []
    for dr in range(2):
        out, vjp = jax.vjp(_zoh, w["ssm_lam_re"][dr], w["ssm_lam_im"][dr], w["ssm_log_step"][dr],
                           w["ssm_b_re"][dr], w["ssm_b_im"][dr])
        zoh.append(out)
        p["zoh_vjp"].append(vjp)
    p["b_blk"] = [_b_block(z[2], z[3]).astype(BF16) for z in zoh]
    p["coef_fwd"] = [_scan_tables(zoh[0][0], zoh[0][1], False), _scan_tables(zoh[1][0], zoh[1][1], True)]
    p["coef_adj"] = [_scan_tables(zoh[0][0], -zoh[0][1], True), _scan_tables(zoh[1][0], -zoh[1][1], False)]
    return p


def _head_prep_fwd(q_raw, kv_raw, proj, tabs, p, li):
    cf, sa, sb = tabs
    scale = QK_HEAD ** -0.5

    def fn(qr, kn, vv, kr, cfv, sav, sbv, gq, gk):
        qo, ko = [], []
        for h in range(MLA_HEADS):
            sl = slice(h * HEAD_PAD, (h + 1) * HEAD_PAD)
            qo.append(_rope(_rms(qr[:, sl], gq, QK_HEAD), cfv, sav, sbv) * scale)
            ko.append(_rope(_rms(kn[:, sl] + kr, gk, QK_HEAD), cfv, sav, sbv))
        lane = lax.broadcasted_iota(jnp.int32, vv.shape, 1)
        return jnp.concatenate(qo, axis=1), jnp.concatenate(ko, axis=1), jnp.where(lane % HEAD_PAD == V_HEAD, 1.0, vv)

    n = MLA_HEADS * HEAD_PAD
    return _rowwise(fn, [(q_raw, n, 0), (kv_raw, n, 0), (kv_raw, n, 1), (proj, HEAD_PAD, P_KR // HEAD_PAD),
                         (cf, HEAD_PAD, 0), (sa, HEAD_PAD, 0), (sb, HEAD_PAD, 0)], [p["qh_g"], p["kh_g"]],
                    [(n, BF16), (n, BF16), (n, BF16)], tm=256, name=f"head_prep_fwd_{li}")


def _head_prep_bwd(dq, dk, dv, q_raw, kv_raw, proj, tabs, p, li):
    cf, sa, sb = tabs
    scale = QK_HEAD ** -0.5

    def fn(dqv, dkv, dvv, qr, kn, kr, cfv, sav, sbv, gq, gk):
        dqo, dko = [], []
        dkr = jnp.zeros_like(kr)
        dgq = jnp.zeros((1, HEAD_PAD), F32)
        dgk = jnp.zeros((1, HEAD_PAD), F32)
        for h in range(MLA_HEADS):
            sl = slice(h * HEAD_PAD, (h + 1) * HEAD_PAD)
            dx, dg = _rms_bwd(qr[:, sl], gq, _rope_t(dqv[:, sl] * scale, cfv, sav, sbv), QK_HEAD)
            dqo.append(dx)
            dgq = dgq + dg
            dx, dg = _rms_bwd(kn[:, sl] + kr, gk, _rope_t(dkv[:, sl], cfv, sav, sbv), QK_HEAD)
            dko.append(dx)
            dkr = dkr + dx
            dgk = dgk + dg
        return jnp.concatenate(dqo, axis=1), jnp.concatenate(dko + [dvv], axis=1), dkr, dgq, dgk

    n = MLA_HEADS * HEAD_PAD
    return _rowwise(fn, [(dq, n, 0), (dk, n, 0), (dv, n, 0), (q_raw, n, 0), (kv_raw, n, 0),
                         (proj, HEAD_PAD, P_KR // HEAD_PAD), (cf, HEAD_PAD, 0), (sa, HEAD_PAD, 0), (sb, HEAD_PAD, 0)],
                    [p["qh_g"], p["kh_g"]], [(n, BF16), (2 * n, BF16), (HEAD_PAD, BF16)],
                    [(HEAD_PAD, F32), (HEAD_PAD, F32)], tm=256, name=f"head_prep_bwd_{li}")


def _layer_fwd(x, p, tabs, li, exchange=None):
    sv = {"x": x}
    h = _rowwise(lambda xv, g: _rms(xv, g, D_MODEL), [(x, D_MODEL, 0)], [p["mix_g"]], [(D_MODEL, BF16)],
                 name=f"mix_norm_{li}")[0]
    proj = _mm_nn(h, p["w_in_p"], name=f"in_proj_{li}")
    sv["h"], sv["proj"] = h, proj
    y_f, xr_f, xi_f = _s5_scan_fwd(proj, p["b_blk"][0], p["c_blk"], p["coef_fwd"][0], False, f"s5_fwd_f_{li}")
    y_b, xr_b, xi_b = _s5_scan_fwd(proj, p["b_blk"][1], p["c_blk"], p["coef_fwd"][1], True, f"s5_fwd_b_{li}")
    sv["states"] = [(xr_f, xi_f), (xr_b, xi_b)]
    y_raw, yg = _rowwise(lambda a, b, u, d: (a + b + d * u, _gelu(a + b + d * u)),
                         [(y_f, SSM_WIDTH, 0), (y_b, SSM_WIDTH, 0), (proj, SSM_WIDTH, P_U // SSM_WIDTH)], [p["d"]],
                         [(SSM_WIDTH, F32), (SSM_WIDTH, BF16)], name=f"s5_gelu_{li}")
    z = _mm_nn(yg, p["w_glu"], name=f"glu_proj_{li}")
    y_ssm = _rowwise(lambda yr, zv, b: _gelu(yr) * _sigmoid(zv + b), [(y_raw, SSM_WIDTH, 0), (z, SSM_WIDTH, 0)],
                     [p["b_glu"]], [(SSM_WIDTH, BF16)], name=f"glu_{li}")[0]
    sv.update(y_raw=y_raw, yg=yg, z=z, y_ssm=y_ssm)
    cqn, ckvn = _rowwise(lambda cq, ckv, gq, gkv: (_rms(cq, gq, Q_LORA), _rms(ckv, gkv, KV_LORA)),
                         [(proj, Q_LORA, P_CQ // Q_LORA), (proj, KV_LORA, P_CKV // KV_LORA)], [p["q_g"], p["kv_g"]],
                         [(Q_LORA, BF16), (KV_LORA, BF16)], name=f"lora_norm_{li}")
    q_raw = _mm_nn(cqn, p["w_q_p"], name=f"q_up_{li}")
    kv_raw = _mm_nn(ckvn, p["w_kv_p"], name=f"kv_up_{li}")
    q, k, v = _head_prep_fwd(q_raw, kv_raw, proj, tabs, p, li)
    o, lse, *gathered = _attn_fwd(q, k, v, f"attn_fwd_{li}", exchange)
    sv.update(cqn=cqn, ckvn=ckvn, q_raw=q_raw, kv_raw=kv_raw, q=q, k=k, v=v, o=o, lse=lse)
    t_ssm = _mm_nn(y_ssm, p["w_out_ssm"], name=f"out_ssm_{li}")
    t_mla = _mm_nn(o, p["w_out_mla_p"], name=f"out_mla_{li}")
    merged = _rowwise(lambda g0, g1, ts, tmv, b: _sigmoid(g0 + b[0:1]) * ts + _sigmoid(g1 + b[1:2]) * tmv,
                      [(proj, D_MODEL, 0), (proj, D_MODEL, 1), (t_ssm, D_MODEL, 0), (t_mla, D_MODEL, 0)],
                      [p["b_gate"]], [(D_MODEL, BF16)], name=f"merge_{li}")[0]
    x1 = _mm_nn(merged, p["w_o"], add=x, name=f"o_proj_{li}")
    sv.update(t_ssm=t_ssm, t_mla=t_mla, merged=merged, x1=x1)
    h2 = _rowwise(lambda xv, g: _rms(xv, g, D_MODEL), [(x1, D_MODEL, 0)], [p["ffn_g"]], [(D_MODEL, BF16)],
                  name=f"ffn_norm_{li}")[0]
    a, r = _mm_nn(h2, p["w_ff1"], epilogue=lambda acc: (acc, jnp.square(jnp.maximum(acc, 0.0))),
                  out_dtypes=(F32, BF16), name=f"ff1_{li}")
    x2 = _mm_nn(r, p["w_ff2"], add=x1, name=f"ff2_{li}")
    sv.update(h2=h2, a=a, r=r)
    return x2, sv, gathered


def _layer_bwd(dx2, dx2_b, sv, p, tabs, li, exchange=None):
    g = {}
    da = _mm_nt(dx2_b, p["w_ff2"], extras=(sv["a"],), epilogue=lambda acc, av: acc * (2.0 * jnp.maximum(av, 0.0)),
                out_dtypes=(BF16,), name=f"d_ff2_x_{li}")
    g["w_ff2"] = _mm_tn(sv["r"], dx2_b, name=f"d_ff2_w_{li}")
    dh2 = _mm_nt(da, p["w_ff1"], name=f"d_ff1_x_{li}")
    g["w_ff1"] = _mm_tn(sv["h2"], da, name=f"d_ff1_w_{li}")

    def norm_bwd(xv, dyv, dres, gg):
        dx, dg = _rms_bwd(xv, gg, dyv, D_MODEL)
        return dres + dx, dres + dx, dg

    dx1, dx1_b, dg = _rowwise(norm_bwd, [(sv["x1"], D_MODEL, 0), (dh2, D_MODEL, 0), (dx2, D_MODEL, 0)], [p["ffn_g"]],
                              [(D_MODEL, F32), (D_MODEL, BF16)], [(D_MODEL, F32)], name=f"d_ffn_norm_{li}")
    g["ffn_norm_g"] = dg.reshape(D_MODEL)
    dmerged = _mm_nt(dx1_b, p["w_o"], name=f"d_o_x_{li}")
    g["w_o"] = _mm_tn(sv["merged"], dx1_b, name=f"d_o_w_{li}")

    def merge_bwd(dm, g0, g1, ts, tmv, b):
        s0, s1 = _sigmoid(g0 + b[0:1]), _sigmoid(g1 + b[1:2])
        dg0, dg1 = dm * ts * s0 * (1.0 - s0), dm * tmv * s1 * (1.0 - s1)
        return (dm * s0, dm * s1, jnp.concatenate([dg0, dg1], axis=1),
                jnp.sum(dg0, axis=0, keepdims=True), jnp.sum(dg1, axis=0, keepdims=True))

    proj = sv["proj"]
    dt_ssm, dt_mla, dgate, db0, db1 = _rowwise(
        merge_bwd, [(dmerged, D_MODEL, 0), (proj, D_MODEL, 0), (proj, D_MODEL, 1), (sv["t_ssm"], D_MODEL, 0),
                    (sv["t_mla"], D_MODEL, 0)], [p["b_gate"]],
        [(D_MODEL, BF16), (D_MODEL, BF16), (2 * D_MODEL, BF16)], [(D_MODEL, F32), (D_MODEL, F32)], tm=256,
        name=f"d_merge_{li}")
    g["b_gate"] = jnp.concatenate([db0, db1], axis=0)
    dy_ssm = _mm_nt(dt_ssm, p["w_out_ssm"], name=f"d_out_ssm_x_{li}")
    g["w_out_ssm"] = _mm_tn(sv["y_ssm"], dt_ssm, name=f"d_out_ssm_w_{li}")
    do = _mm_nt(dt_mla, p["w_out_mla_p"], name=f"d_out_mla_x_{li}")
    g["w_out_mla"] = _unpad_out_mla(_mm_tn(sv["o"], dt_mla, name=f"d_out_mla_w_{li}"))

    def glu_bwd(dyv, yr, zv, b):
        yg = _gelu(yr)
        sg = _sigmoid(zv + b)
        dz = dyv * yg * sg * (1.0 - sg)
        return dz, dyv * sg, jnp.sum(dz, axis=0, keepdims=True)

    dz, dyg_direct, dbglu = _rowwise(glu_bwd, [(dy_ssm, SSM_WIDTH, 0), (sv["y_raw"], SSM_WIDTH, 0), (sv["z"], SSM_WIDTH, 0)],
                                     [p["b_glu"]], [(SSM_WIDTH, BF16), (SSM_WIDTH, F32)], [(SSM_WIDTH, F32)],
                                     name=f"d_glu_{li}")
    g["b_glu"] = dbglu.reshape(SSM_WIDTH)
    dyg_mm = _mm_nt(dz, p["w_glu"], name=f"d_glu_x_{li}")
    g["w_glu"] = _mm_tn(sv["yg"], dz, name=f"d_glu_w_{li}")

    def gelu_bwd(d1, d2, yr, u, d):
        dyr = (d1 + d2) * _gelu_grad(yr)
        return dyr, dyr * d, jnp.sum(dyr * u, axis=0, keepdims=True)

    dy_raw, du_d, dd = _rowwise(gelu_bwd, [(dyg_direct, SSM_WIDTH, 0), (dyg_mm, SSM_WIDTH, 0), (sv["y_raw"], SSM_WIDTH, 0),
                                           (proj, SSM_WIDTH, P_U // SSM_WIDTH)], [p["d"]],
                                [(SSM_WIDTH, BF16), (SSM_WIDTH, F32)], [(SSM_WIDTH, F32)], name=f"d_gelu_{li}")
    g["ssm_d"] = dd.reshape(SSM_GROUPS, SSM_GROUP)
    du_parts, dc_sum = [du_d], None
    zoh_grads = []
    for dr_i in range(2):
        xr, xi = sv["states"][dr_i]
        du_i, da_i, db_i, dc_i = _s5_scan_bwd(dy_raw, proj, xr, xi, p["b_blk"][dr_i], p["c_blk"], p["coef_adj"][dr_i],
                                              dr_i == 0, f"s5_bwd_{'fb'[dr_i]}_{li}")
        du_parts.append(du_i)
        dc_sum = dc_i if dc_sum is None else dc_sum + dc_i
        da_i = jnp.sum(da_i, axis=1)
        dar = da_i[:, :CHUNK_STATE].reshape(SSM_GROUPS, SSM_STATE)
        dai = da_i[:, CHUNK_STATE:].reshape(SSM_GROUPS, SSM_STATE)
        dbr, dbi = _b_unblock(db_i)
        zoh_grads.append(p["zoh_vjp"][dr_i]((dar, dai, dbr, dbi)))
    for k_i, nm in enumerate(("ssm_lam_re", "ssm_lam_im", "ssm_log_step", "ssm_b_re", "ssm_b_im")):
        g[nm] = jnp.stack([zoh_grads[0][k_i], zoh_grads[1][k_i]], axis=0)
    g["ssm_c_re"], g["ssm_c_im"] = _c_unblock(dc_sum)
    du = _rowwise(lambda a, b, c: a + b + c, [(d_, SSM_WIDTH, 0) for d_ in du_parts], [], [(SSM_WIDTH, BF16)],
                  name=f"d_u_sum_{li}")[0]
    dq, dk, dv, *scattered = _attn_bwd(sv["q"], sv["k"], sv["v"], sv["o"], do, sv["lse"], f"attn_bwd_{li}", exchange)
    dq_raw, dkv_raw, dkr, dgq, dgk = _head_prep_bwd(dq, dk, dv, sv["q_raw"], sv["kv_raw"], proj, tabs, p, li)
    g["q_head_g"] = dgq.reshape(HEAD_PAD)[:QK_HEAD]
    g["k_head_g"] = dgk.reshape(HEAD_PAD)[:QK_HEAD]
    dcqn = _mm_nt(dq_raw, p["w_q_p"], name=f"d_q_up_x_{li}")
    g["w_q_up"] = _unpad_heads_cols(_mm_tn(sv["cqn"], dq_raw, name=f"d_q_up_w_{li}"), QK_HEAD)
    dckvn = _mm_nt(dkv_raw, p["w_kv_p"], name=f"d_kv_up_x_{li}")
    g["w_kv_up"] = _unpad_kv(_mm_tn(sv["ckvn"], dkv_raw, name=f"d_kv_up_w_{li}"))

    def lora_bwd(cq, ckv, d1, d2, gq, gkv):
        dx1_, dg1 = _rms_bwd(cq, gq, d1, Q_LORA)
        dx2_, dg2 = _rms_bwd(ckv, gkv, d2, KV_LORA)
        return dx1_, dx2_, dg1, dg2

    dcq, dckv, dgqn, dgkvn = _rowwise(
        lora_bwd, [(proj, Q_LORA, P_CQ // Q_LORA), (proj, KV_LORA, P_CKV // KV_LORA), (dcqn, Q_LORA, 0), (dckvn, KV_LORA, 0)],
        [p["q_g"], p["kv_g"]], [(Q_LORA, BF16), (KV_LORA, BF16)], [(Q_LORA, F32), (KV_LORA, F32)], name=f"d_lora_norm_{li}")
    g["q_norm_g"], g["kv_norm_g"] = dgqn.reshape(Q_LORA), dgkvn.reshape(KV_LORA)
    gap = jnp.zeros((dx2.shape[0], P_CQ - P_KR - HEAD_PAD), BF16)
    tail = jnp.zeros((dx2.shape[0], P_COLS - P_CQ - Q_LORA), BF16)
    dproj = jnp.concatenate([dgate, du, dckv, dkr, gap, dcq, tail], axis=1)
    dh = _mm_nt(dproj, p["w_in_p"], name=f"d_in_x_{li}")
    g["w_in"] = _unpad_w_in(_mm_tn(sv["h"], dproj, name=f"d_in_w_{li}"))
    dx, dx_b, dg = _rowwise(norm_bwd, [(sv["x"], D_MODEL, 0), (dh, D_MODEL, 0), (dx1, D_MODEL, 0)], [p["mix_g"]],
                            [(D_MODEL, F32), (D_MODEL, BF16)], [(D_MODEL, F32)], name=f"d_mix_norm_{li}")
    g["mix_norm_g"] = dg.reshape(D_MODEL)
    return dx, dx_b, g, scattered


def _local_step(x, target, layer_weights, send_weights=None, send_grads=None):
    tabs = _rope_tables(x.shape[0])
    saved, preps = [], []
    gathered = None
    for li in range(DEPTH):
        p = _prep_layer(layer_weights(li, gathered))
        nxt = send_weights(li + 1) if send_weights is not None and li + 1 < DEPTH else None
        x, sv, gathered = _layer_fwd(x, p, tabs, li, nxt)
        saved.append(sv)
        preps.append(p)

    def loss_fn(y, t):
        err = y - t
        d = err * (1.0 / D_MODEL)
        return d, d, jnp.sum(jnp.sum(err * err, axis=1, keepdims=True), axis=0, keepdims=True) * jnp.ones((1, LANES), F32)

    dx, dx_b, lsum = _rowwise(loss_fn, [(x, D_MODEL, 0), (target, D_MODEL, 0)], [], [(D_MODEL, F32), (D_MODEL, BF16)],
                              [(LANES, F32)], name="loss")
    loss = 0.5 * lsum[0, 0] * (1.0 / D_MODEL)
    grads, scattered = [None] * DEPTH, [None] * DEPTH
    pending = None
    for li in reversed(range(DEPTH)):
        dx, dx_b, grads[li], got = _layer_bwd(dx, dx_b, saved[li], preps[li], tabs, li, pending)
        if pending is not None:
            scattered[li + 1] = got
        pending = send_grads(grads[li]) if send_grads is not None else None
    return loss, dx, grads, scattered, pending


def _exchange(bufs, scatter, name):
    n = len(bufs)

    def body(*refs):
        copies = _exchange_copies(refs[:n], refs[n:2 * n], *refs[2 * n:], scatter)
        for cp in copies:
            cp.start()
        for cp in copies:
            cp.wait()

    shapes = [tuple(b.shape[1:]) if scatter else tuple(b.shape) for b in bufs]
    return pl.pallas_call(
        body, out_shape=[jax.ShapeDtypeStruct((N_DEV,) + s, b.dtype) for s, b in zip(shapes, bufs)],
        in_specs=[pl.BlockSpec(memory_space=pl.ANY)] * n, out_specs=[pl.BlockSpec(memory_space=pl.ANY)] * n,
        scratch_shapes=[pltpu.SemaphoreType.DMA((n, N_DEV)), pltpu.SemaphoreType.DMA((n, N_DEV)),
                        pltpu.SemaphoreType.DMA((n,))],
        name=name, compiler_params=pltpu.CompilerParams(has_side_effects=True),
    )(*bufs)


def _adamw(parts, w, m, v, name):
    shape = w.shape
    cols = shape[-1]
    r = math.prod(shape[:-1])
    parts, w, m, v = parts.reshape(N_DEV, r, cols), w.reshape(r, cols), m.reshape(r, cols), v.reshape(r, cols)
    tm = _pick_rows(r, cols)

    def body(p_ref, w_ref, m_ref, v_ref, g_ref, d_ref, nm_ref, nv_ref):
        g = p_ref[0].astype(F32)
        for j in range(1, N_DEV):
            g = g + p_ref[j].astype(F32)
        m_new = ADAM_B1 * m_ref[...] + (1.0 - ADAM_B1) * g
        v_new = ADAM_B2 * v_ref[...] + (1.0 - ADAM_B2) * (g * g)
        m_hat = m_new / (1.0 - ADAM_B1 ** ADAM_STEP)
        v_hat = v_new / (1.0 - ADAM_B2 ** ADAM_STEP)
        g_ref[...] = g
        d_ref[...] = -ADAM_LR * (m_hat / (jnp.sqrt(v_hat) + ADAM_EPS) + ADAM_WD * w_ref[...])
        nm_ref[...] = m_new
        nv_ref[...] = v_new

    spec = pl.BlockSpec((tm, cols), lambda i: (i, 0))
    res = pl.pallas_call(
        body, grid=(r // tm,), in_specs=[pl.BlockSpec((N_DEV, tm, cols), lambda i: (0, i, 0)), spec, spec, spec],
        out_specs=[spec] * 4, out_shape=[jax.ShapeDtypeStruct((r, cols), F32)] * 4, name=name,
        compiler_params=_cparams(("parallel",)),
    )(parts, w, m, v)
    return [a.reshape(shape) for a in res]


def _pick_rows(r, cols):
    for t in (512, 256, 128, 64, 32, 16):
        if r % t == 0 and t * cols <= 512 * 512:
            return t
    return r


def _pack(arrs, dtype, row_mult):
    flat = jnp.concatenate([a.reshape(-1).astype(dtype) for a in arrs])
    n = flat.shape[0]
    per = row_mult * D_MODEL
    total = -(-n // per) * per
    return jnp.pad(flat, (0, total - n)).reshape(total // D_MODEL, D_MODEL)


def _unpack(flat, shapes):
    lead = flat.shape[:-2]
    flat = flat.reshape(lead + (-1,))
    out, off = [], 0
    for shp in shapes:
        n = math.prod(shp)
        out.append(flat[..., off:off + n].reshape(lead + tuple(shp)))
        off += n
    return out


def _to_shards(gfull, axis):
    shp = gfull.shape
    gfull = gfull.reshape(shp[:axis] + (N_DEV, shp[axis] // N_DEV) + shp[axis + 1:])
    return jnp.moveaxis(gfull, axis, 0)


def _from_shards(parts, axis):
    parts = jnp.moveaxis(parts, 0, axis)
    shp = parts.shape
    return parts.reshape(shp[:axis] + (shp[axis] * shp[axis + 1],) + shp[axis + 2:])


def kernel(x, mix_norm_g, w_in, b_gate, ssm_lam_re, ssm_lam_im, ssm_log_step, ssm_b_re, ssm_b_im, ssm_c_re, ssm_c_im, ssm_d, w_glu, b_glu, w_out_ssm, q_norm_g, kv_norm_g, w_q_up, w_kv_up, q_head_g, k_head_g, w_out_mla, w_o, ffn_norm_g, w_ff1, w_ff2, loss_target, m_mix_norm_g, m_w_in, m_b_gate, m_ssm_lam_re, m_ssm_lam_im, m_ssm_log_step, m_ssm_b_re, m_ssm_b_im, m_ssm_c_re, m_ssm_c_im, m_ssm_d, m_w_glu, m_b_glu, m_w_out_ssm, m_q_norm_g, m_kv_norm_g, m_w_q_up, m_w_kv_up, m_q_head_g, m_k_head_g, m_w_out_mla, m_w_o, m_ffn_norm_g, m_w_ff1, m_w_ff2, v_mix_norm_g, v_w_in, v_b_gate, v_ssm_lam_re, v_ssm_lam_im, v_ssm_log_step, v_ssm_b_re, v_ssm_b_im, v_ssm_c_re, v_ssm_c_im, v_ssm_d, v_w_glu, v_b_glu, v_w_out_ssm, v_q_norm_g, v_kv_norm_g, v_w_q_up, v_w_kv_up, v_q_head_g, v_k_head_g, v_w_out_mla, v_w_o, v_ffn_norm_g, v_w_ff1, v_w_ff2):
    args = dict(locals())
    w = {n: args[n] for n in WEIGHTS}
    m = {n: args["m_" + n] for n in WEIGHTS}
    v = {n: args["v_" + n] for n in WEIGHTS}

    def send_weights(li):
        return [w[n][li] if n == "b_gate" else w[n][li].astype(BF16) for n in SHARDED]

    first = _exchange(send_weights(0), False, "weight_all_gather_0")

    def layer_weights(li, gathered):
        full = {n: _from_shards(pt, SHARD_AXIS[n] - 1) for n, pt in zip(SHARDED, first if li == 0 else gathered)}
        for n in REPLICATED:
            full[n] = w[n][li]
        return full

    def send_grads(g):
        return [_to_shards(g[n], SHARD_AXIS[n] - 1).astype(BF16) for n in SHARDED]

    loss_part, dx, grads, scattered, pending = _local_step(x[0], loss_target[0], layer_weights, send_weights, send_grads)
    scattered[0] = _exchange(pending, True, "grad_reduce_scatter_0")
    loss = lax.psum(loss_part, ("x", "y", "c"))
    rep = [jnp.stack([grads[li][n] for li in range(DEPTH)], axis=0) for n in REPLICATED]
    r_parts = _exchange([_pack(rep, F32, 8)], False, "grad_all_gather")[0]

    outs = {}
    for t, n in enumerate(SHARDED):
        parts = jnp.stack([scattered[li][t] for li in range(DEPTH)], axis=1)
        for kind, a in zip(("grad", "delta", "new_m", "new_v"), _adamw(parts, w[n], m[n], v[n], "adamw_" + n)):
            outs[kind + "_" + n] = a
    rep_shapes = [w[n].shape for n in REPLICATED]
    res = _adamw(r_parts, _pack([w[n] for n in REPLICATED], F32, 8), _pack([m[n] for n in REPLICATED], F32, 8),
                 _pack([v[n] for n in REPLICATED], F32, 8), "adamw_replicated")
    for kind, flat in zip(("grad", "delta", "new_m", "new_v"), res):
        for n, a in zip(REPLICATED, _unpack(flat, rep_shapes)):
            outs[kind + "_" + n] = a
    return (loss, dx[None], *[outs[k + "_" + n] for k in ("grad", "delta", "new_m", "new_v") for n in WEIGHTS])
```

```python
import functools
import math

import jax
import jax.numpy as jnp
from jax import lax
from jax.experimental import pallas as pl
from jax.experimental.pallas import tpu as pltpu

F32 = jnp.float32
BF16 = jnp.bfloat16
_MXU = jnp.bfloat16

D_MODEL = 1024
DEPTH = 4
SSM_WIDTH = 512
SSM_GROUP = 16
SSM_GROUPS = 32
SSM_STATE = 64
MLA_HEADS = 8
QK_NOPE = 64
QK_ROPE = 32
QK_HEAD = 96
V_HEAD = 64
Q_LORA = 384
KV_LORA = 256
ROPE_THETA = 10000.0
D_FF = 4096
EPS = 1e-6
HEAD_PAD = 128
N_DEV = 8
LANES = 128
SUBLANES = 8
CHUNK_GROUPS = 8
N_CHUNKS = SSM_GROUPS // CHUNK_GROUPS
CHUNK_STATE = CHUNK_GROUPS * SSM_STATE

P_GATE, P_U, P_CKV, P_KR, P_CQ = 0, 2048, 2560, 2816, 3072
P_COLS = 3584
IN_U, IN_CQ, IN_CKV, IN_KR, IN_GATE = 0, 512, 896, 1152, 1184
IN_COLS = 3232

ADAM_LR = 0.001
ADAM_B1 = 0.9
ADAM_B2 = 0.999
ADAM_EPS = 1e-08
ADAM_WD = 0.01
ADAM_STEP = 10

VMEM_LIMIT = 56 * 1024 * 1024
TN_OUT_BLOCK_BYTES = 8 * 1024 * 1024
MXU_WIDTH = 256
WIDE_BLOCK_MAX_K = 1024

SHARDED = ("w_in", "b_gate", "w_glu", "w_out_ssm", "w_q_up", "w_kv_up", "w_out_mla", "w_o", "w_ff1", "w_ff2")
SHARD_AXIS = {"w_in": 2, "b_gate": 2, "w_glu": 1, "w_out_ssm": 2, "w_q_up": 2, "w_kv_up": 2, "w_out_mla": 2,
              "w_o": 1, "w_ff1": 2, "w_ff2": 1}
EARLY = ("b_gate", "w_glu", "w_out_ssm", "w_out_mla", "w_o", "w_ff1", "w_ff2")
LATE = ("w_in", "w_q_up", "w_kv_up")
REPLICATED = ("mix_norm_g", "ssm_lam_re", "ssm_lam_im", "ssm_log_step", "ssm_b_re", "ssm_b_im", "ssm_c_re",
              "ssm_c_im", "ssm_d", "b_glu", "q_norm_g", "kv_norm_g", "q_head_g", "k_head_g", "ffn_norm_g")
WEIGHTS = ("mix_norm_g", "w_in", "b_gate", "ssm_lam_re", "ssm_lam_im", "ssm_log_step", "ssm_b_re", "ssm_b_im",
           "ssm_c_re", "ssm_c_im", "ssm_d", "w_glu", "b_glu", "w_out_ssm", "q_norm_g", "kv_norm_g", "w_q_up",
           "w_kv_up", "q_head_g", "k_head_g", "w_out_mla", "w_o", "ffn_norm_g", "w_ff1", "w_ff2")


def _cparams(sem):
    return pltpu.CompilerParams(dimension_semantics=sem, vmem_limit_bytes=VMEM_LIMIT)


def _dot(a, b, dims):
    return lax.dot_general(a.astype(_MXU), b.astype(_MXU), (dims, ((), ())), preferred_element_type=F32)


def _dot_nn(a, b):
    return _dot(a, b, ((1,), (0,)))


def _dot_nt(a, b):
    return _dot(a, b, ((1,), (1,)))


def _dot_tn(a, b):
    return _dot(a, b, ((0,), (0,)))


def _rowwise(fn, rows, consts, outs, accs=(), *, tm=512, name):
    n_rows = rows[0][0].shape[0]
    tm = min(tm, n_rows)
    n_in = len(rows) + len(consts)
    n_o, n_a = len(outs), len(accs)

    def body(*refs):
        res = fn(*[r[...] for r in refs[:n_in]])
        if not isinstance(res, (tuple, list)):
            res = (res,)
        orefs = refs[n_in:]
        for k in range(n_o):
            orefs[k][...] = res[k].astype(orefs[k].dtype)
        if n_a:
            @pl.when(pl.program_id(0) == 0)
            def _():
                for k in range(n_a):
                    orefs[n_o + k][...] = jnp.zeros_like(orefs[n_o + k])
            for k in range(n_a):
                orefs[n_o + k][...] += res[n_o + k]

    in_specs = [pl.BlockSpec((tm, w), functools.partial(lambda i, j: (i, j), j=j)) for (_, w, j) in rows]
    in_specs += [pl.BlockSpec(c.shape, functools.partial(lambda i, nd: (0,) * nd, nd=c.ndim)) for c in consts]
    out_specs = [pl.BlockSpec((tm, w), lambda i: (i, 0)) for (w, _) in outs]
    out_specs += [pl.BlockSpec((1, w), lambda i: (0, 0)) for (w, _) in accs]
    out_shape = [jax.ShapeDtypeStruct((n_rows, w), dt) for (w, dt) in outs]
    out_shape += [jax.ShapeDtypeStruct((1, w), dt) for (w, dt) in accs]
    res = pl.pallas_call(
        body, grid=(n_rows // tm,), in_specs=in_specs, out_specs=out_specs, out_shape=out_shape, name=name,
        compiler_params=_cparams(("arbitrary",) if n_a else ("parallel",)),
    )(*[r[0] for r in rows], *consts)
    return res


def _pick(n, cap):
    if n <= cap:
        return n
    for unit in (MXU_WIDTH, LANES):
        best = 0
        for t in range(unit, cap + 1, unit):
            if n % t == 0:
                best = t
        if best:
            return best
    return n


def _mm(a, b, transpose_b, extras, epilogue, out_dtypes, name):
    m, k = a.shape
    n = b.shape[0] if transpose_b else b.shape[1]
    tm, tn = min(512, m), _pick(n, 2048 if k <= WIDE_BLOCK_MAX_K else 1024)
    n_in = 2 + len(extras)

    def body(*refs):
        acc = (_dot_nt if transpose_b else _dot_nn)(refs[0][...], refs[1][...])
        res = epilogue(acc, *[r[...] for r in refs[2:n_in]]) if epilogue is not None else acc
        if not isinstance(res, (tuple, list)):
            res = (res,)
        for o_ref, val in zip(refs[n_in:], res):
            o_ref[...] = val.astype(o_ref.dtype)

    blk = pl.BlockSpec((tm, tn), lambda j, i: (i, j))
    b_spec = pl.BlockSpec((tn, k), lambda j, i: (j, 0)) if transpose_b else pl.BlockSpec((k, tn), lambda j, i: (0, j))
    res = pl.pallas_call(
        body, grid=(n // tn, m // tm),
        in_specs=[pl.BlockSpec((tm, k), lambda j, i: (i, 0)), b_spec] + [blk] * len(extras),
        out_specs=[blk] * len(out_dtypes), out_shape=[jax.ShapeDtypeStruct((m, n), dt) for dt in out_dtypes],
        name=name, compiler_params=_cparams(("parallel", "parallel")),
    )(a, b, *extras)
    return res[0] if len(out_dtypes) == 1 else res


def _mm_nn(a, b, *, add=None, extras=(), epilogue=None, out_dtypes=(F32,), name):
    if add is not None:
        extras, epilogue = (add,), (lambda acc, r: acc + r)
    return _mm(a, b, False, tuple(extras), epilogue, out_dtypes, name)


def _mm_nt(a, b, *, extras=(), epilogue=None, out_dtypes=(F32,), name):
    return _mm(a, b, True, tuple(extras), epilogue, out_dtypes, name)


def _mm_tn(a, b, *, a_cols=None, name):
    s = a.shape[0]
    n = b.shape[1]
    mw, mj = (a.shape[1], 0) if a_cols is None else a_cols
    ts = min(512, s)
    tm = _pick(mw, 1024)
    tn = _pick(n, max(1024, TN_OUT_BLOCK_BYTES // (4 * tm)))
    n_mb = mw // tm

    def body(a_ref, b_ref, o_ref):
        @pl.when(pl.program_id(2) == 0)
        def _():
            o_ref[...] = jnp.zeros_like(o_ref)
        o_ref[...] += _dot_tn(a_ref[...], b_ref[...])

    return pl.pallas_call(
        body, grid=(n_mb, n // tn, s // ts),
        in_specs=[pl.BlockSpec((ts, tm), lambda i, j, t: (t, mj * n_mb + i)), pl.BlockSpec((ts, tn), lambda i, j, t: (t, j))],
        out_specs=pl.BlockSpec((tm, tn), lambda i, j, t: (i, j)),
        out_shape=jax.ShapeDtypeStruct((mw, n), F32), name=name,
        compiler_params=_cparams(("parallel", "parallel", "arbitrary")),
    )(a, b)


def _rms(x, g, n):
    r = lax.rsqrt(jnp.sum(x * x, axis=-1, keepdims=True) * (1.0 / n) + EPS)
    return x * r * g


def _rms_bwd(x, g, dy, n):
    r = lax.rsqrt(jnp.sum(x * x, axis=-1, keepdims=True) * (1.0 / n) + EPS)
    xr = x * r
    dyg = dy * g
    dx = r * dyg - xr * (r * r) * (jnp.sum(dyg * x, axis=-1, keepdims=True) * (1.0 / n))
    return dx, jnp.sum(dy * xr, axis=0, keepdims=True)


def _gelu(x):
    c = math.sqrt(2.0 / math.pi)
    return 0.5 * x * (1.0 + jnp.tanh(c * (x + 0.044715 * (x * x * x))))


def _gelu_grad(x):
    c = math.sqrt(2.0 / math.pi)
    t = jnp.tanh(c * (x + 0.044715 * (x * x * x)))
    return 0.5 * (1.0 + t) + 0.5 * x * (1.0 - t * t) * (c * (1.0 + 3.0 * 0.044715 * (x * x)))


def _sigmoid(x):
    return 1.0 / (1.0 + jnp.exp(-x))


def _rope(x, cf, sa, sb):
    return x * cf + pltpu.roll(x, HEAD_PAD - QK_ROPE // 2, 1) * sa + pltpu.roll(x, QK_ROPE // 2, 1) * sb


def _rope_t(d, cf, sa, sb):
    return d * cf + pltpu.roll(d * sa, QK_ROPE // 2, 1) + pltpu.roll(d * sb, HEAD_PAD - QK_ROPE // 2, 1)


def _scan_tables(ar, ai, reverse):
    ar = ar.reshape(N_CHUNKS, CHUNK_STATE)
    ai = ai.reshape(N_CHUNKS, CHUNK_STATE)
    pr, pi = [ar], [ai]
    for _ in range(SUBLANES - 1):
        pr, pi = pr + [pr[-1] * ar - pi[-1] * ai], pi + [pr[-1] * ai + pi[-1] * ar]
    row = jnp.arange(SUBLANES)[None, :, None]
    tiles = []
    for k in (1, 2, 4):
        mask = (row <= SUBLANES - 1 - k) if reverse else (row >= k)
        tiles.append(jnp.where(mask, pr[k - 1][:, None, :], 0.0))
        tiles.append(jnp.where(mask, pi[k - 1][:, None, :], 0.0))
    order = list(range(SUBLANES))[::-1] if reverse else list(range(SUBLANES))
    tiles.append(jnp.stack([pr[j] for j in order], axis=1))
    tiles.append(jnp.stack([pi[j] for j in order], axis=1))
    return jnp.stack(tiles, axis=1).astype(F32)


def _slab_scan(xr, xi, coef, carry_r, carry_i, reverse):
    for idx, k in enumerate((1, 2, 4)):
        sh = SUBLANES - k if reverse else k
        sr, si = pltpu.roll(xr, sh, 0), pltpu.roll(xi, sh, 0)
        cr, ci = coef[2 * idx], coef[2 * idx + 1]
        xr, xi = xr + cr * sr - ci * si, xi + cr * si + ci * sr
    pr, pi = coef[6], coef[7]
    xr = xr + pr * carry_r - pi * carry_i
    xi = xi + pr * carry_i + pi * carry_r
    return xr, xi


def _s5_scan_fwd(proj, b_blk, c_blk, coef, reverse, name):
    s = proj.shape[0]
    t_blk = min(512, s)
    n_t = s // t_blk
    n_slab = t_blk // SUBLANES
    last = 0 if reverse else SUBLANES - 1

    def tmap(t):
        return n_t - 1 - t if reverse else t

    def body(u_ref, b_ref, c_ref, coef_ref, y_ref, xr_ref, xi_ref, carry_ref):
        @pl.when(pl.program_id(1) == 0)
        def _():
            carry_ref[...] = jnp.zeros_like(carry_ref)
        bu = _dot_nn(u_ref[...], b_ref[0])
        xr_ref[...] = bu[:, :CHUNK_STATE]
        xi_ref[...] = bu[:, CHUNK_STATE:]
        coef_v = [coef_ref[0, k] for k in range(8)]

        def slab(i, carry):
            sl = (n_slab - 1 - i) if reverse else i
            rows = pl.ds(pl.multiple_of(sl * SUBLANES, SUBLANES), SUBLANES)
            xr, xi = _slab_scan(xr_ref[rows, :], xi_ref[rows, :], coef_v, carry[0], carry[1], reverse)
            xr_ref[rows, :] = xr
            xi_ref[rows, :] = xi
            return (jnp.broadcast_to(xr[last:last + 1, :], xr.shape), jnp.broadcast_to(xi[last:last + 1, :], xi.shape))

        cr, ci = lax.fori_loop(0, n_slab, slab, (carry_ref[0], carry_ref[1]))
        carry_ref[0] = cr
        carry_ref[1] = ci
        y_ref[...] = _dot_nn(xr_ref[...], c_ref[0, :CHUNK_STATE, :]) + _dot_nn(xi_ref[...], c_ref[0, CHUNK_STATE:, :])

    u_blk0 = P_U // LANES
    return pl.pallas_call(
        body, grid=(N_CHUNKS, n_t),
        in_specs=[pl.BlockSpec((t_blk, LANES), lambda c, t: (tmap(t), u_blk0 + c)),
                  pl.BlockSpec((1, LANES, 2 * CHUNK_STATE), lambda c, t: (c, 0, 0)),
                  pl.BlockSpec((1, 2 * CHUNK_STATE, LANES), lambda c, t: (c, 0, 0)),
                  pl.BlockSpec((1, 8, SUBLANES, CHUNK_STATE), lambda c, t: (c, 0, 0, 0))],
        out_specs=[pl.BlockSpec((t_blk, LANES), lambda c, t: (tmap(t), c)),
                   pl.BlockSpec((t_blk, CHUNK_STATE), lambda c, t: (tmap(t), c)),
                   pl.BlockSpec((t_blk, CHUNK_STATE), lambda c, t: (tmap(t), c))],
        out_shape=[jax.ShapeDtypeStruct((s, SSM_WIDTH), F32),
                   jax.ShapeDtypeStruct((s, N_CHUNKS * CHUNK_STATE), F32),
                   jax.ShapeDtypeStruct((s, N_CHUNKS * CHUNK_STATE), F32)],
        scratch_shapes=[pltpu.VMEM((2, SUBLANES, CHUNK_STATE), F32)],
        name=name, compiler_params=_cparams(("parallel", "arbitrary")),
    )(proj, b_blk, c_blk, coef)


def _s5_scan_bwd(dy, proj, x_re, x_im, b_blk, c_blk, coef, reverse, name):
    s = dy.shape[0]
    t_blk = min(512, s)
    n_t = s // t_blk
    n_slab = t_blk // SUBLANES
    last = 0 if reverse else SUBLANES - 1
    first = SUBLANES - 1 if reverse else 0

    def tmap(t):
        return n_t - 1 - t if reverse else t

    def body(dy_ref, u_ref, xr_ref, xi_ref, b_ref, c_ref, coef_ref, du_ref, da_ref, db_ref, dc_ref,
             carry_ref, lr_ref, li_ref):
        @pl.when(pl.program_id(1) == 0)
        def _():
            carry_ref[...] = jnp.zeros_like(carry_ref)
            da_ref[...] = jnp.zeros_like(da_ref)
            db_ref[...] = jnp.zeros_like(db_ref)
            dc_ref[...] = jnp.zeros_like(dc_ref)
        g = _dot_nt(dy_ref[...], c_ref[0])
        lr_ref[...] = g[:, :CHUNK_STATE]
        li_ref[...] = g[:, CHUNK_STATE:]
        coef_v = [coef_ref[0, k] for k in range(8)]
        row = lax.broadcasted_iota(jnp.int32, (SUBLANES, CHUNK_STATE), 0)
        sh_prev = SUBLANES - 1 if reverse else 1

        def slab(i, carry):
            cr, ci, ar_acc, ai_acc = carry
            sl = (n_slab - 1 - i) if reverse else i
            rows = pl.ds(pl.multiple_of(sl * SUBLANES, SUBLANES), SUBLANES)
            lr, li = _slab_scan(lr_ref[rows, :], li_ref[rows, :], coef_v, cr, ci, reverse)
            lr_ref[rows, :] = lr
            li_ref[rows, :] = li
            pr = jnp.where(row == first, cr, pltpu.roll(lr, sh_prev, 0))
            pi = jnp.where(row == first, ci, pltpu.roll(li, sh_prev, 0))
            xr, xi = xr_ref[rows, :], xi_ref[rows, :]
            ar_acc = ar_acc + xr * pr + xi * pi
            ai_acc = ai_acc + xr * pi - xi * pr
            return (jnp.broadcast_to(lr[last:last + 1, :], lr.shape), jnp.broadcast_to(li[last:last + 1, :], li.shape),
                    ar_acc, ai_acc)

        zero = jnp.zeros((SUBLANES, CHUNK_STATE), F32)
        cr, ci, ar_acc, ai_acc = lax.fori_loop(0, n_slab, slab, (carry_ref[0], carry_ref[1], zero, zero))
        carry_ref[0] = cr
        carry_ref[1] = ci
        da_ref[0, :, :CHUNK_STATE] += ar_acc
        da_ref[0, :, CHUNK_STATE:] += ai_acc
        lam_r, lam_i = lr_ref[...], li_ref[...]
        u = u_ref[...]
        du_ref[...] = _dot_nt(lam_r, b_ref[0, :, :CHUNK_STATE]) + _dot_nt(lam_i, b_ref[0, :, CHUNK_STATE:])
        db_ref[0, :, :CHUNK_STATE] += _dot_tn(u, lam_r)
        db_ref[0, :, CHUNK_STATE:] += _dot_tn(u, lam_i)
        dyv = dy_ref[...]
        dc_ref[0, :CHUNK_STATE, :] += _dot_tn(xr_ref[...], dyv)
        dc_ref[0, CHUNK_STATE:, :] += _dot_tn(xi_ref[...], dyv)

    u_blk0 = P_U // LANES
    return pl.pallas_call(
        body, grid=(N_CHUNKS, n_t),
        in_specs=[pl.BlockSpec((t_blk, LANES), lambda c, t: (tmap(t), c)),
                  pl.BlockSpec((t_blk, LANES), lambda c, t: (tmap(t), u_blk0 + c)),
                  pl.BlockSpec((t_blk, CHUNK_STATE), lambda c, t: (tmap(t), c)),
                  pl.BlockSpec((t_blk, CHUNK_STATE), lambda c, t: (tmap(t), c)),
                  pl.BlockSpec((1, LANES, 2 * CHUNK_STATE), lambda c, t: (c, 0, 0)),
                  pl.BlockSpec((1, 2 * CHUNK_STATE, LANES), lambda c, t: (c, 0, 0)),
                  pl.BlockSpec((1, 8, SUBLANES, CHUNK_STATE), lambda c, t: (c, 0, 0, 0))],
        out_specs=[pl.BlockSpec((t_blk, LANES), lambda c, t: (tmap(t), c)),
                   pl.BlockSpec((1, SUBLANES, 2 * CHUNK_STATE), lambda c, t: (c, 0, 0)),
                   pl.BlockSpec((1, LANES, 2 * CHUNK_STATE), lambda c, t: (c, 0, 0)),
                   pl.BlockSpec((1, 2 * CHUNK_STATE, LANES), lambda c, t: (c, 0, 0))],
        out_shape=[jax.ShapeDtypeStruct((s, SSM_WIDTH), F32),
                   jax.ShapeDtypeStruct((N_CHUNKS, SUBLANES, 2 * CHUNK_STATE), F32),
                   jax.ShapeDtypeStruct((N_CHUNKS, LANES, 2 * CHUNK_STATE), F32),
                   jax.ShapeDtypeStruct((N_CHUNKS, 2 * CHUNK_STATE, LANES), F32)],
        scratch_shapes=[pltpu.VMEM((2, SUBLANES, CHUNK_STATE), F32), pltpu.VMEM((t_blk, CHUNK_STATE), F32),
                        pltpu.VMEM((t_blk, CHUNK_STATE), F32)],
        name=name, compiler_params=_cparams(("parallel", "arbitrary")),
    )(dy, proj, x_re, x_im, b_blk, c_blk, coef)


def _zoh(lam_re, lam_im, log_step, b_re, b_im):
    step = jnp.exp(log_step)[:, None]
    mag = jnp.exp(lam_re * step)
    abar_r = mag * jnp.cos(lam_im * step)
    abar_i = mag * jnp.sin(lam_im * step)
    nr = abar_r - 1.0
    ni = abar_i
    den = lam_re * lam_re + lam_im * lam_im
    fr = (nr * lam_re + ni * lam_im) / den
    fi = (ni * lam_re - nr * lam_im) / den
    bbar_r = fr[..., None] * b_re - fi[..., None] * b_im
    bbar_i = fr[..., None] * b_im + fi[..., None] * b_re
    return abar_r, abar_i, bbar_r, bbar_i


def _b_block(bbar_r, bbar_i):
    eye = jnp.eye(CHUNK_GROUPS, dtype=F32)

    def one(b):
        b = b.reshape(N_CHUNKS, CHUNK_GROUPS, SSM_STATE, SSM_GROUP)
        return jnp.einsum("cgnp,gh->cgphn", b, eye).reshape(N_CHUNKS, LANES, CHUNK_STATE)

    return jnp.concatenate([one(bbar_r), one(bbar_i)], axis=2)


def _b_unblock(db):
    eye = jnp.eye(CHUNK_GROUPS, dtype=F32)

    def one(d):
        d = d.reshape(N_CHUNKS, CHUNK_GROUPS, SSM_GROUP, CHUNK_GROUPS, SSM_STATE)
        return jnp.einsum("cgphn,gh->cgnp", d, eye).reshape(SSM_GROUPS, SSM_STATE, SSM_GROUP)

    return one(db[:, :, :CHUNK_STATE]), one(db[:, :, CHUNK_STATE:])


def _c_block(c_re, c_im):
    eye = jnp.eye(CHUNK_GROUPS, dtype=F32)

    def one(c):
        c = c.reshape(N_CHUNKS, CHUNK_GROUPS, SSM_GROUP, SSM_STATE)
        return jnp.einsum("cgpn,gh->cgnhp", c, eye).reshape(N_CHUNKS, CHUNK_STATE, LANES)

    return jnp.concatenate([one(c_re), -one(c_im)], axis=1)


def _c_unblock(dc):
    eye = jnp.eye(CHUNK_GROUPS, dtype=F32)

    def one(d):
        d = d.reshape(N_CHUNKS, CHUNK_GROUPS, SSM_STATE, CHUNK_GROUPS, SSM_GROUP)
        return jnp.einsum("cgnhp,gh->cgpn", d, eye).reshape(SSM_GROUPS, SSM_GROUP, SSM_STATE)

    return one(dc[:, :CHUNK_STATE, :]), -one(dc[:, CHUNK_STATE:, :])


def _exchange_copies(ins, outs, send_sems, recv_sems, local_sems, scatter):
    x, y, c = lax.axis_index("x"), lax.axis_index("y"), lax.axis_index("c")
    me = 4 * x + 2 * y + c
    copies = [pltpu.make_async_copy(ins[t].at[me] if scatter[t] else ins[t], outs[t].at[me], local_sems.at[t])
              for t in range(len(ins))]
    for k in range(1, N_DEV):
        peer = (x ^ ((k >> 2) & 1), y ^ ((k >> 1) & 1), c ^ (k & 1))
        peer_idx = 4 * peer[0] + 2 * peer[1] + peer[2]
        for t in range(len(ins)):
            copies.append(pltpu.make_async_remote_copy(
                src_ref=ins[t].at[peer_idx] if scatter[t] else ins[t], dst_ref=outs[t].at[me],
                send_sem=send_sems.at[t, k], recv_sem=recv_sems.at[t, k], device_id=peer,
                device_id_type=pl.DeviceIdType.MESH))
    return copies


def _call_with_exchange(body, *, grid, in_specs, out_specs, out_shape, scratch_shapes, args, semantics, name,
                        exchange=None, scatter=()):
    if exchange is None:
        return pl.pallas_call(body, grid=grid, in_specs=in_specs, out_specs=out_specs, out_shape=out_shape,
                              scratch_shapes=scratch_shapes, name=name, compiler_params=_cparams(semantics))(*args)
    n, n_in, n_out, n_scr = len(exchange), len(in_specs), len(out_specs), len(scratch_shapes)

    def wrapped(*refs):
        ins, refs = refs[:n_in], refs[n_in:]
        c_ins, refs = refs[:n], refs[n:]
        outs, refs = refs[:n_out], refs[n_out:]
        c_outs, refs = refs[:n], refs[n:]
        scr, sems = refs[:n_scr], refs[n_scr:]
        ids = [pl.program_id(a) for a in range(len(grid))]
        first = functools.reduce(jnp.logical_and, [i == 0 for i in ids])
        last = functools.reduce(jnp.logical_and, [i == g - 1 for i, g in zip(ids, grid)])

        @pl.when(first)
        def _():
            for cp in _exchange_copies(c_ins, c_outs, *sems, scatter):
                cp.start()

        body(*ins, *outs, *scr)

        @pl.when(last)
        def _():
            for cp in _exchange_copies(c_ins, c_outs, *sems, scatter):
                cp.wait()

    shapes = [tuple(b.shape[1:]) if sc else tuple(b.shape) for b, sc in zip(exchange, scatter)]
    hbm = pl.BlockSpec(memory_space=pl.ANY)
    res = pl.pallas_call(
        wrapped, grid=grid, in_specs=list(in_specs) + [hbm] * n, out_specs=list(out_specs) + [hbm] * n,
        out_shape=list(out_shape) + [jax.ShapeDtypeStruct((N_DEV,) + s, b.dtype) for s, b in zip(shapes, exchange)],
        scratch_shapes=list(scratch_shapes) + [pltpu.SemaphoreType.DMA((n, N_DEV)), pltpu.SemaphoreType.DMA((n, N_DEV)),
                                              pltpu.SemaphoreType.DMA((n,))],
        name=name, compiler_params=pltpu.CompilerParams(dimension_semantics=("arbitrary",) * len(grid),
                                                        vmem_limit_bytes=VMEM_LIMIT, has_side_effects=True),
    )(*args, *exchange)
    return res


def _attn_fwd(q, k, v, name, exchange=None):
    s = q.shape[0]
    tq = min(1024, s)
    tk = min(1024, s)
    n_k = s // tk

    def body(q_ref, k_ref, v_ref, o_ref, lse_ref, m_ref, acc_ref):
        m_ref[...] = jnp.full_like(m_ref, -jnp.inf)
        acc_ref[...] = jnp.zeros_like(acc_ref)
        qv = q_ref[...]

        def step(j, _):
            rows = pl.ds(pl.multiple_of(j * tk, tk), tk)
            sc = _dot_nt(qv, k_ref[rows, :])
            m_old = m_ref[...]
            m_new = jnp.maximum(m_old, jnp.max(sc, axis=1, keepdims=True))
            p = jnp.exp(sc - m_new)
            alpha = jnp.exp(m_old - m_new)
            acc_ref[...] = alpha * acc_ref[...] + _dot_nn(p, v_ref[rows, :])
            m_ref[...] = m_new
            return 0

        lax.fori_loop(0, n_k, step, 0, unroll=min(4, n_k))
        acc = acc_ref[...]
        l = acc[:, V_HEAD:V_HEAD + 1]
        o_ref[...] = acc / l
        lse = m_ref[...] + jnp.log(l)
        lse_ref[0] = jnp.broadcast_to(lse, (tq, LANES)).T[:SUBLANES, :]

    return _call_with_exchange(
        body, grid=(MLA_HEADS, s // tq),
        in_specs=[pl.BlockSpec((tq, HEAD_PAD), lambda h, i: (i, h)),
                  pl.BlockSpec((s, HEAD_PAD), lambda h, i: (0, h)),
                  pl.BlockSpec((s, HEAD_PAD), lambda h, i: (0, h))],
        out_specs=[pl.BlockSpec((tq, HEAD_PAD), lambda h, i: (i, h)),
                   pl.BlockSpec((1, SUBLANES, tq), lambda h, i: (h, 0, i))],
        out_shape=[jax.ShapeDtypeStruct((s, MLA_HEADS * HEAD_PAD), F32),
                   jax.ShapeDtypeStruct((MLA_HEADS, SUBLANES, s), F32)],
        scratch_shapes=[pltpu.VMEM((tq, 1), F32), pltpu.VMEM((tq, HEAD_PAD), F32)],
        args=(q, k, v), semantics=("parallel", "parallel"), name=name, exchange=exchange,
        scatter=(False,) * len(exchange or ()))


def _attn_bwd(q, k, v, o, do, lse, name, exchange=None, scatter=()):
    s = q.shape[0]
    tq = min(1024, s)
    tk = min(1024, s)

    def body(q_ref, k_ref, v_ref, o_ref, do_ref, lse_ref, dq_ref, dk_ref, dv_ref):
        j, i = pl.program_id(1), pl.program_id(2)

        @pl.when(jnp.logical_and(j == 0, i == 0))
        def _():
            dq_ref[...] = jnp.zeros_like(dq_ref)

        @pl.when(i == 0)
        def _():
            dk_ref[...] = jnp.zeros_like(dk_ref)
            dv_ref[...] = jnp.zeros_like(dv_ref)

        qv, kv, vv, dov = q_ref[...], k_ref[...], v_ref[...], do_ref[...]
        delta_col = jnp.sum(dov * o_ref[...], axis=1, keepdims=True)
        delta = jnp.broadcast_to(delta_col, (tq, LANES)).T[:1, :]
        st = _dot_nt(kv, qv)
        pt = jnp.exp(st - lse_ref[0, :1, :])
        dv_ref[...] += _dot_nn(pt, dov)
        dpt = _dot_nt(vv, dov)
        dst = pt * (dpt - delta)
        dk_ref[...] += _dot_nn(dst, qv)
        rows = pl.ds(pl.multiple_of(i * tq, tq), tq)
        dq_ref[rows, :] += _dot_tn(dst, kv)

    return _call_with_exchange(
        body, grid=(MLA_HEADS, s // tk, s // tq),
        in_specs=[pl.BlockSpec((tq, HEAD_PAD), lambda h, j, i: (i, h)),
                  pl.BlockSpec((tk, HEAD_PAD), lambda h, j, i: (j, h)),
                  pl.BlockSpec((tk, HEAD_PAD), lambda h, j, i: (j, h)),
                  pl.BlockSpec((tq, HEAD_PAD), lambda h, j, i: (i, h)),
                  pl.BlockSpec((tq, HEAD_PAD), lambda h, j, i: (i, h)),
                  pl.BlockSpec((1, SUBLANES, tq), lambda h, j, i: (h, 0, i))],
        out_specs=[pl.BlockSpec((s, HEAD_PAD), lambda h, j, i: (0, h)),
                   pl.BlockSpec((tk, HEAD_PAD), lambda h, j, i: (j, h)),
                   pl.BlockSpec((tk, HEAD_PAD), lambda h, j, i: (j, h))],
        out_shape=[jax.ShapeDtypeStruct((s, MLA_HEADS * HEAD_PAD), F32)] * 3, scratch_shapes=[],
        args=(q, k, v, o, do, lse), semantics=("parallel", "arbitrary", "arbitrary"), name=name,
        exchange=exchange, scatter=scatter)


def _pad_w_in(w):
    z = functools.partial(jnp.zeros, dtype=w.dtype)
    return jnp.concatenate([
        w[:, IN_GATE:IN_COLS], w[:, IN_U:IN_CQ], w[:, IN_CKV:IN_KR],
        z((D_MODEL, QK_NOPE)), w[:, IN_KR:IN_GATE], z((D_MODEL, HEAD_PAD - QK_HEAD)),
        z((D_MODEL, P_CQ - P_KR - HEAD_PAD)), w[:, IN_CQ:IN_CKV], z((D_MODEL, P_COLS - P_CQ - Q_LORA))], axis=1)


def _unpad_w_in(d):
    return jnp.concatenate([d[:, P_U:P_CKV], d[:, P_CQ:P_CQ + Q_LORA], d[:, P_CKV:P_KR],
                            d[:, P_KR + QK_NOPE:P_KR + QK_HEAD], d[:, P_GATE:P_U]], axis=1)


def _pad_heads_cols(w, real):
    k = w.shape[0]
    w = w.reshape(k, MLA_HEADS, real)
    return jnp.pad(w, ((0, 0), (0, 0), (0, HEAD_PAD - real))).reshape(k, MLA_HEADS * HEAD_PAD)


def _unpad_heads_cols(d, real):
    k = d.shape[0]
    return d.reshape(k, MLA_HEADS, HEAD_PAD)[:, :, :real].reshape(k, MLA_HEADS * real)


def _pad_kv(w):
    w = w.reshape(KV_LORA, MLA_HEADS, QK_NOPE + V_HEAD)
    kn = jnp.pad(w[:, :, :QK_NOPE], ((0, 0), (0, 0), (0, HEAD_PAD - QK_NOPE)))
    vv = jnp.pad(w[:, :, QK_NOPE:], ((0, 0), (0, 0), (0, HEAD_PAD - V_HEAD)))
    return jnp.concatenate([kn.reshape(KV_LORA, -1), vv.reshape(KV_LORA, -1)], axis=1)


def _unpad_kv(d):
    n = MLA_HEADS * HEAD_PAD
    kn = d[:, :n].reshape(KV_LORA, MLA_HEADS, HEAD_PAD)[:, :, :QK_NOPE]
    vv = d[:, n:].reshape(KV_LORA, MLA_HEADS, HEAD_PAD)[:, :, :V_HEAD]
    return jnp.concatenate([kn, vv], axis=2).reshape(KV_LORA, MLA_HEADS * (QK_NOPE + V_HEAD))


def _pad_out_mla(w):
    w = w.reshape(MLA_HEADS, V_HEAD, D_MODEL)
    return jnp.pad(w, ((0, 0), (0, HEAD_PAD - V_HEAD), (0, 0))).reshape(MLA_HEADS * HEAD_PAD, D_MODEL)


def _unpad_out_mla(d):
    return d.reshape(MLA_HEADS, HEAD_PAD, D_MODEL)[:, :V_HEAD, :].reshape(MLA_HEADS * V_HEAD, D_MODEL)


def _rope_tables(seq):
    half = QK_ROPE // 2
    inv_freq = ROPE_THETA ** (-jnp.arange(half, dtype=F32) / half)
    ang = jnp.arange(seq, dtype=F32)[:, None] * inv_freq[None, :]
    cos, sin = jnp.cos(ang), jnp.sin(ang)
    one, zero = jnp.ones((seq, QK_NOPE), F32), jnp.zeros((seq, half), F32)
    tail1, tail0 = jnp.ones((seq, HEAD_PAD - QK_HEAD), F32), jnp.zeros((seq, HEAD_PAD - QK_HEAD), F32)
    cf = jnp.concatenate([one, cos, cos, tail1], axis=1)
    sa = jnp.concatenate([0.0 * one, -sin, zero, tail0], axis=1)
    sb = jnp.concatenate([0.0 * one, zero, sin, tail0], axis=1)
    return cf, sa, sb


def _prep_layer(w):
    p = {}
    p["w_in_p"] = _pad_w_in(w["w_in"])
    p["w_glu"] = w["w_glu"]
    p["w_out_ssm"] = w["w_out_ssm"]
    p["w_q_p"] = _pad_heads_cols(w["w_q_up"], QK_HEAD)
    p["w_kv_p"] = _pad_kv(w["w_kv_up"])
    p["w_out_mla_p"] = _pad_out_mla(w["w_out_mla"])
    p["w_o"] = w["w_o"]
    p["w_ff1"] = w["w_ff1"]
    p["w_ff2"] = w["w_ff2"]
    p["mix_g"] = w["mix_norm_g"].reshape(1, D_MODEL)
    p["ffn_g"] = w["ffn_norm_g"].reshape(1, D_MODEL)
    p["b_gate"] = w["b_gate"]
    p["b_glu"] = w["b_glu"].reshape(1, SSM_WIDTH)
    p["d"] = w["ssm_d"].reshape(1, SSM_WIDTH)
    p["q_g"] = w["q_norm_g"].reshape(1, Q_LORA)
    p["kv_g"] = w["kv_norm_g"].reshape(1, KV_LORA)
    p["qh_g"] = jnp.pad(w["q_head_g"], (0, HEAD_PAD - QK_HEAD)).reshape(1, HEAD_PAD)
    p["kh_g"] = jnp.pad(w["k_head_g"], (0, HEAD_PAD - QK_HEAD)).reshape(1, HEAD_PAD)
    p["c_blk"] = _c_block(w["ssm_c_re"], w["ssm_c_im"]).astype(BF16)
    zoh, p["zoh_vjp"] = [], []
    for dr in range(2):
        out, vjp = jax.vjp(_zoh, w["ssm_lam_re"][dr], w["ssm_lam_im"][dr], w["ssm_log_step"][dr],
                           w["ssm_b_re"][dr], w["ssm_b_im"][dr])
        zoh.append(out)
        p["zoh_vjp"].append(vjp)
    p["b_blk"] = [_b_block(z[2], z[3]).astype(BF16) for z in zoh]
    p["coef_fwd"] = [_scan_tables(zoh[0][0], zoh[0][1], False), _scan_tables(zoh[1][0], zoh[1][1], True)]
    p["coef_adj"] = [_scan_tables(zoh[0][0], -zoh[0][1], True), _scan_tables(zoh[1][0], -zoh[1][1], False)]
    return p


def _head_prep_fwd(q_raw, kv_raw, proj, tabs, p, li):
    cf, sa, sb = tabs
    scale = QK_HEAD ** -0.5

    def fn(qr, kn, vv, kr, cfv, sav, sbv, gq, gk):
        qo, ko = [], []
        for h in range(MLA_HEADS):
            sl = slice(h * HEAD_PAD, (h + 1) * HEAD_PAD)
            qo.append(_rope(_rms(qr[:, sl], gq, QK_HEAD), cfv, sav, sbv) * scale)
            ko.append(_rope(_rms(kn[:, sl] + kr, gk, QK_HEAD), cfv, sav, sbv))
        lane = lax.broadcasted_iota(jnp.int32, vv.shape, 1)
        return jnp.concatenate(qo, axis=1), jnp.concatenate(ko, axis=1), jnp.where(lane % HEAD_PAD == V_HEAD, 1.0, vv)

    n = MLA_HEADS * HEAD_PAD
    return _rowwise(fn, [(q_raw, n, 0), (kv_raw, n, 0), (kv_raw, n, 1), (proj, HEAD_PAD, P_KR // HEAD_PAD),
                         (cf, HEAD_PAD, 0), (sa, HEAD_PAD, 0), (sb, HEAD_PAD, 0)], [p["qh_g"], p["kh_g"]],
                    [(n, BF16), (n, BF16), (n, BF16)], tm=256, name=f"head_prep_fwd_{li}")


def _head_prep_bwd(dq, dk, dv, q_raw, kv_raw, proj, tabs, p, li):
    cf, sa, sb = tabs
    scale = QK_HEAD ** -0.5

    def fn(dqv, dkv, dvv, qr, kn, kr, cfv, sav, sbv, gq, gk):
        dqo, dko = [], []
        dkr = jnp.zeros_like(kr)
        dgq = jnp.zeros((1, HEAD_PAD), F32)
        dgk = jnp.zeros((1, HEAD_PAD), F32)
        for h in range(MLA_HEADS):
            sl = slice(h * HEAD_PAD, (h + 1) * HEAD_PAD)
            dx, dg = _rms_bwd(qr[:, sl], gq, _rope_t(dqv[:, sl] * scale, cfv, sav, sbv), QK_HEAD)
            dqo.append(dx)
            dgq = dgq + dg
            dx, dg = _rms_bwd(kn[:, sl] + kr, gk, _rope_t(dkv[:, sl], cfv, sav, sbv), QK_HEAD)
            dko.append(dx)
            dkr = dkr + dx
            dgk = dgk + dg
        return jnp.concatenate(dqo, axis=1), jnp.concatenate(dko + [dvv], axis=1), dkr, dgq, dgk

    n = MLA_HEADS * HEAD_PAD
    return _rowwise(fn, [(dq, n, 0), (dk, n, 0), (dv, n, 0), (q_raw, n, 0), (kv_raw, n, 0),
                         (proj, HEAD_PAD, P_KR // HEAD_PAD), (cf, HEAD_PAD, 0), (sa, HEAD_PAD, 0), (sb, HEAD_PAD, 0)],
                    [p["qh_g"], p["kh_g"]], [(n, BF16), (2 * n, BF16), (HEAD_PAD, BF16)],
                    [(HEAD_PAD, F32), (HEAD_PAD, F32)], tm=256, name=f"head_prep_bwd_{li}")


def _layer_fwd(x, p, tabs, li, exchange=None):
    sv = {"x": x}
    h = _rowwise(lambda xv, g: _rms(xv, g, D_MODEL), [(x, D_MODEL, 0)], [p["mix_g"]], [(D_MODEL, BF16)],
                 name=f"mix_norm_{li}")[0]
    proj = _mm_nn(h, p["w_in_p"], name=f"in_proj_{li}")
    sv["h"], sv["proj"] = h, proj
    y_f, xr_f, xi_f = _s5_scan_fwd(proj, p["b_blk"][0], p["c_blk"], p["coef_fwd"][0], False, f"s5_fwd_f_{li}")
    y_b, xr_b, xi_b = _s5_scan_fwd(proj, p["b_blk"][1], p["c_blk"], p["coef_fwd"][1], True, f"s5_fwd_b_{li}")
    sv["states"] = [(xr_f, xi_f), (xr_b, xi_b)]
    y_raw, yg = _rowwise(lambda a, b, u, d: (a + b + d * u, _gelu(a + b + d * u)),
                         [(y_f, SSM_WIDTH, 0), (y_b, SSM_WIDTH, 0), (proj, SSM_WIDTH, P_U // SSM_WIDTH)], [p["d"]],
                         [(SSM_WIDTH, F32), (SSM_WIDTH, BF16)], name=f"s5_gelu_{li}")
    z = _mm_nn(yg, p["w_glu"], name=f"glu_proj_{li}")
    y_ssm = _rowwise(lambda yr, zv, b: _gelu(yr) * _sigmoid(zv + b), [(y_raw, SSM_WIDTH, 0), (z, SSM_WIDTH, 0)],
                     [p["b_glu"]], [(SSM_WIDTH, BF16)], name=f"glu_{li}")[0]
    sv.update(y_raw=y_raw, yg=yg, z=z, y_ssm=y_ssm)
    cqn, ckvn = _rowwise(lambda cq, ckv, gq, gkv: (_rms(cq, gq, Q_LORA), _rms(ckv, gkv, KV_LORA)),
                         [(proj, Q_LORA, P_CQ // Q_LORA), (proj, KV_LORA, P_CKV // KV_LORA)], [p["q_g"], p["kv_g"]],
                         [(Q_LORA, BF16), (KV_LORA, BF16)], name=f"lora_norm_{li}")
    q_raw = _mm_nn(cqn, p["w_q_p"], name=f"q_up_{li}")
    kv_raw = _mm_nn(ckvn, p["w_kv_p"], name=f"kv_up_{li}")
    q, k, v = _head_prep_fwd(q_raw, kv_raw, proj, tabs, p, li)
    o, lse, *gathered = _attn_fwd(q, k, v, f"attn_fwd_{li}", exchange)
    sv.update(cqn=cqn, ckvn=ckvn, q_raw=q_raw, kv_raw=kv_raw, q=q, k=k, v=v, o=o, lse=lse)
    t_ssm = _mm_nn(y_ssm, p["w_out_ssm"], name=f"out_ssm_{li}")
    t_mla = _mm_nn(o, p["w_out_mla_p"], name=f"out_mla_{li}")
    merged = _rowwise(lambda g0, g1, ts, tmv, b: _sigmoid(g0 + b[0:1]) * ts + _sigmoid(g1 + b[1:2]) * tmv,
                      [(proj, D_MODEL, 0), (proj, D_MODEL, 1), (t_ssm, D_MODEL, 0), (t_mla, D_MODEL, 0)],
                      [p["b_gate"]], [(D_MODEL, BF16)], name=f"merge_{li}")[0]
    x1 = _mm_nn(merged, p["w_o"], add=x, name=f"o_proj_{li}")
    sv.update(t_ssm=t_ssm, t_mla=t_mla, merged=merged, x1=x1)
    h2 = _rowwise(lambda xv, g: _rms(xv, g, D_MODEL), [(x1, D_MODEL, 0)], [p["ffn_g"]], [(D_MODEL, BF16)],
                  name=f"ffn_norm_{li}")[0]
    a, r = _mm_nn(h2, p["w_ff1"], epilogue=lambda acc: (acc, jnp.square(jnp.maximum(acc, 0.0))),
                  out_dtypes=(F32, BF16), name=f"ff1_{li}")
    x2 = _mm_nn(r, p["w_ff2"], add=x1, name=f"ff2_{li}")
    sv.update(h2=h2, a=a, r=r)
    return x2, sv, gathered


def _layer_bwd(dx2, dx2_b, sv, p, tabs, li, pending=None, send_early=None):
    g = {}
    da = _mm_nt(dx2_b, p["w_ff2"], extras=(sv["a"],), epilogue=lambda acc, av: acc * (2.0 * jnp.maximum(av, 0.0)),
                out_dtypes=(BF16,), name=f"d_ff2_x_{li}")
    g["w_ff2"] = _mm_tn(sv["r"], dx2_b, name=f"d_ff2_w_{li}")
    dh2 = _mm_nt(da, p["w_ff1"], name=f"d_ff1_x_{li}")
    g["w_ff1"] = _mm_tn(sv["h2"], da, name=f"d_ff1_w_{li}")

    def norm_bwd(xv, dyv, dres, gg):
        dx, dg = _rms_bwd(xv, gg, dyv, D_MODEL)
        return dres + dx, dres + dx, dg

    dx1, dx1_b, dg = _rowwise(norm_bwd, [(sv["x1"], D_MODEL, 0), (dh2, D_MODEL, 0), (dx2, D_MODEL, 0)], [p["ffn_g"]],
                              [(D_MODEL, F32), (D_MODEL, BF16)], [(D_MODEL, F32)], name=f"d_ffn_norm_{li}")
    g["ffn_norm_g"] = dg.reshape(D_MODEL)
    dmerged = _mm_nt(dx1_b, p["w_o"], name=f"d_o_x_{li}")
    g["w_o"] = _mm_tn(sv["merged"], dx1_b, name=f"d_o_w_{li}")

    def merge_bwd(dm, g0, g1, ts, tmv, b):
        s0, s1 = _sigmoid(g0 + b[0:1]), _sigmoid(g1 + b[1:2])
        dg0, dg1 = dm * ts * s0 * (1.0 - s0), dm * tmv * s1 * (1.0 - s1)
        return (dm * s0, dm * s1, jnp.concatenate([dg0, dg1], axis=1),
                jnp.sum(dg0, axis=0, keepdims=True), jnp.sum(dg1, axis=0, keepdims=True))

    proj = sv["proj"]
    dt_ssm, dt_mla, dgate, db0, db1 = _rowwise(
        merge_bwd, [(dmerged, D_MODEL, 0), (proj, D_MODEL, 0), (proj, D_MODEL, 1), (sv["t_ssm"], D_MODEL, 0),
                    (sv["t_mla"], D_MODEL, 0)], [p["b_gate"]],
        [(D_MODEL, BF16), (D_MODEL, BF16), (2 * D_MODEL, BF16)], [(D_MODEL, F32), (D_MODEL, F32)], tm=256,
        name=f"d_merge_{li}")
    g["b_gate"] = jnp.concatenate([db0, db1], axis=0)
    dy_ssm = _mm_nt(dt_ssm, p["w_out_ssm"], name=f"d_out_ssm_x_{li}")
    g["w_out_ssm"] = _mm_tn(sv["y_ssm"], dt_ssm, name=f"d_out_ssm_w_{li}")
    do = _mm_nt(dt_mla, p["w_out_mla_p"], name=f"d_out_mla_x_{li}")
    g["w_out_mla"] = _unpad_out_mla(_mm_tn(sv["o"], dt_mla, name=f"d_out_mla_w_{li}"))

    def glu_bwd(dyv, yr, zv, b):
        yg = _gelu(yr)
        sg = _sigmoid(zv + b)
        dz = dyv * yg * sg * (1.0 - sg)
        return dz, dyv * sg, jnp.sum(dz, axis=0, keepdims=True)

    dz, dyg_direct, dbglu = _rowwise(glu_bwd, [(dy_ssm, SSM_WIDTH, 0), (sv["y_raw"], SSM_WIDTH, 0), (sv["z"], SSM_WIDTH, 0)],
                                     [p["b_glu"]], [(SSM_WIDTH, BF16), (SSM_WIDTH, F32)], [(SSM_WIDTH, F32)],
                                     name=f"d_glu_{li}")
    g["b_glu"] = dbglu.reshape(SSM_WIDTH)
    dyg_mm = _mm_nt(dz, p["w_glu"], name=f"d_glu_x_{li}")
    g["w_glu"] = _mm_tn(sv["yg"], dz, name=f"d_glu_w_{li}")

    def gelu_bwd(d1, d2, yr, u, d):
        dyr = (d1 + d2) * _gelu_grad(yr)
        return dyr, dyr * d, jnp.sum(dyr * u, axis=0, keepdims=True)

    dy_raw, du_d, dd = _rowwise(gelu_bwd, [(dyg_direct, SSM_WIDTH, 0), (dyg_mm, SSM_WIDTH, 0), (sv["y_raw"], SSM_WIDTH, 0),
                                           (proj, SSM_WIDTH, P_U // SSM_WIDTH)], [p["d"]],
                                [(SSM_WIDTH, BF16), (SSM_WIDTH, F32)], [(SSM_WIDTH, F32)], name=f"d_gelu_{li}")
    g["ssm_d"] = dd.reshape(SSM_GROUPS, SSM_GROUP)
    du_parts, dc_sum = [du_d], None
    zoh_grads = []
    for dr_i in range(2):
        xr, xi = sv["states"][dr_i]
        du_i, da_i, db_i, dc_i = _s5_scan_bwd(dy_raw, proj, xr, xi, p["b_blk"][dr_i], p["c_blk"], p["coef_adj"][dr_i],
                                              dr_i == 0, f"s5_bwd_{'fb'[dr_i]}_{li}")
        du_parts.append(du_i)
        dc_sum = dc_i if dc_sum is None else dc_sum + dc_i
        da_i = jnp.sum(da_i, axis=1)
        dar = da_i[:, :CHUNK_STATE].reshape(SSM_GROUPS, SSM_STATE)
        dai = da_i[:, CHUNK_STATE:].reshape(SSM_GROUPS, SSM_STATE)
        dbr, dbi = _b_unblock(db_i)
        zoh_grads.append(p["zoh_vjp"][dr_i]((dar, dai, dbr, dbi)))
    for k_i, nm in enumerate(("ssm_lam_re", "ssm_lam_im", "ssm_log_step", "ssm_b_re", "ssm_b_im")):
        g[nm] = jnp.stack([zoh_grads[0][k_i], zoh_grads[1][k_i]], axis=0)
    g["ssm_c_re"], g["ssm_c_im"] = _c_unblock(dc_sum)
    du = _rowwise(lambda a, b, c: a + b + c, [(d_, SSM_WIDTH, 0) for d_ in du_parts], [], [(SSM_WIDTH, BF16)],
                  name=f"d_u_sum_{li}")[0]
    bufs, flags = pending if pending is not None else ([], [])
    n_pending = len(bufs)
    if send_early is not None:
        early_bufs, early_flags = send_early(g)
        bufs, flags = list(bufs) + list(early_bufs), list(flags) + list(early_flags)
    dq, dk, dv, *got = _attn_bwd(sv["q"], sv["k"], sv["v"], sv["o"], do, sv["lse"], f"attn_bwd_{li}",
                                 bufs or None, tuple(flags))
    dq_raw, dkv_raw, dkr, dgq, dgk = _head_prep_bwd(dq, dk, dv, sv["q_raw"], sv["kv_raw"], proj, tabs, p, li)
    g["q_head_g"] = dgq.reshape(HEAD_PAD)[:QK_HEAD]
    g["k_head_g"] = dgk.reshape(HEAD_PAD)[:QK_HEAD]
    dcqn = _mm_nt(dq_raw, p["w_q_p"], name=f"d_q_up_x_{li}")
    g["w_q_up"] = _unpad_heads_cols(_mm_tn(sv["cqn"], dq_raw, name=f"d_q_up_w_{li}"), QK_HEAD)
    dckvn = _mm_nt(dkv_raw, p["w_kv_p"], name=f"d_kv_up_x_{li}")
    g["w_kv_up"] = _unpad_kv(_mm_tn(sv["ckvn"], dkv_raw, name=f"d_kv_up_w_{li}"))

    def lora_bwd(cq, ckv, d1, d2, gq, gkv):
        dx1_, dg1 = _rms_bwd(cq, gq, d1, Q_LORA)
        dx2_, dg2 = _rms_bwd(ckv, gkv, d2, KV_LORA)
        return dx1_, dx2_, dg1, dg2

    dcq, dckv, dgqn, dgkvn = _rowwise(
        lora_bwd, [(proj, Q_LORA, P_CQ // Q_LORA), (proj, KV_LORA, P_CKV // KV_LORA), (dcqn, Q_LORA, 0), (dckvn, KV_LORA, 0)],
        [p["q_g"], p["kv_g"]], [(Q_LORA, BF16), (KV_LORA, BF16)], [(Q_LORA, F32), (KV_LORA, F32)], name=f"d_lora_norm_{li}")
    g["q_norm_g"], g["kv_norm_g"] = dgqn.reshape(Q_LORA), dgkvn.reshape(KV_LORA)
    gap = jnp.zeros((dx2.shape[0], P_CQ - P_KR - HEAD_PAD), BF16)
    tail = jnp.zeros((dx2.shape[0], P_COLS - P_CQ - Q_LORA), BF16)
    dproj = jnp.concatenate([dgate, du, dckv, dkr, gap, dcq, tail], axis=1)
    dh = _mm_nt(dproj, p["w_in_p"], name=f"d_in_x_{li}")
    g["w_in"] = _unpad_w_in(_mm_tn(sv["h"], dproj, name=f"d_in_w_{li}"))
    dx, dx_b, dg = _rowwise(norm_bwd, [(sv["x"], D_MODEL, 0), (dh, D_MODEL, 0), (dx1, D_MODEL, 0)], [p["mix_g"]],
                            [(D_MODEL, F32), (D_MODEL, BF16)], [(D_MODEL, F32)], name=f"d_mix_norm_{li}")
    g["mix_norm_g"] = dg.reshape(D_MODEL)
    return dx, dx_b, g, got[:n_pending], got[n_pending:]


def _local_step(x, target, layer_weights, send_weights=None, send_early=None, send_late=None):
    tabs = _rope_tables(x.shape[0])
    saved, preps = [], []
    gathered = None
    for li in range(DEPTH):
        p = _prep_layer(layer_weights(li, gathered))
        nxt = send_weights(li + 1) if send_weights is not None and li + 1 < DEPTH else None
        x, sv, gathered = _layer_fwd(x, p, tabs, li, nxt)
        saved.append(sv)
        preps.append(p)

    def loss_fn(y, t):
        err = y - t
        d = err * (1.0 / D_MODEL)
        return d, d, jnp.sum(jnp.sum(err * err, axis=1, keepdims=True), axis=0, keepdims=True) * jnp.ones((1, LANES), F32)

    dx, dx_b, lsum = _rowwise(loss_fn, [(x, D_MODEL, 0), (target, D_MODEL, 0)], [], [(D_MODEL, F32), (D_MODEL, BF16)],
                              [(LANES, F32)], name="loss")
    loss = 0.5 * lsum[0, 0] * (1.0 / D_MODEL)
    grads, early, late = [None] * DEPTH, [None] * DEPTH, [None] * DEPTH
    pending = None
    for li in reversed(range(DEPTH)):
        dx, dx_b, grads[li], got_late, early[li] = _layer_bwd(dx, dx_b, saved[li], preps[li], tabs, li, pending, send_early)
        if pending is not None:
            late[li + 1] = got_late
        pending = send_late(grads[li]) if send_late is not None else None
    return loss, dx, grads, early, late, pending


def _exchange(bufs, scatter, name):
    n = len(bufs)

    def body(*refs):
        copies = _exchange_copies(refs[:n], refs[n:2 * n], *refs[2 * n:], scatter)
        for cp in copies:
            cp.start()
        for cp in copies:
            cp.wait()

    shapes = [tuple(b.shape[1:]) if sc else tuple(b.shape) for b, sc in zip(bufs, scatter)]
    return pl.pallas_call(
        body, out_shape=[jax.ShapeDtypeStruct((N_DEV,) + s, b.dtype) for s, b in zip(shapes, bufs)],
        in_specs=[pl.BlockSpec(memory_space=pl.ANY)] * n, out_specs=[pl.BlockSpec(memory_space=pl.ANY)] * n,
        scratch_shapes=[pltpu.SemaphoreType.DMA((n, N_DEV)), pltpu.SemaphoreType.DMA((n, N_DEV)),
                        pltpu.SemaphoreType.DMA((n,))],
        name=name, compiler_params=pltpu.CompilerParams(has_side_effects=True),
    )(*bufs)


def _adamw(parts, w, m, v, name):
    shape = w.shape
    cols = shape[-1]
    r = math.prod(shape[:-1])
    parts, w, m, v = parts.reshape(N_DEV, r, cols), w.reshape(r, cols), m.reshape(r, cols), v.reshape(r, cols)
    tm = _pick_rows(r, cols)

    def body(p_ref, w_ref, m_ref, v_ref, g_ref, d_ref, nm_ref, nv_ref):
        g = p_ref[0].astype(F32)
        for j in range(1, N_DEV):
            g = g + p_ref[j].astype(F32)
        m_new = ADAM_B1 * m_ref[...] + (1.0 - ADAM_B1) * g
        v_new = ADAM_B2 * v_ref[...] + (1.0 - ADAM_B2) * (g * g)
        m_hat = m_new / (1.0 - ADAM_B1 ** ADAM_STEP)
        v_hat = v_new / (1.0 - ADAM_B2 ** ADAM_STEP)
        g_ref[...] = g
        d_ref[...] = -ADAM_LR * (m_hat / (jnp.sqrt(v_hat) + ADAM_EPS) + ADAM_WD * w_ref[...])
        nm_ref[...] = m_new
        nv_ref[...] = v_new

    spec = pl.BlockSpec((tm, cols), lambda i: (i, 0))
    res = pl.pallas_call(
        body, grid=(r // tm,), in_specs=[pl.BlockSpec((N_DEV, tm, cols), lambda i: (0, i, 0)), spec, spec, spec],
        out_specs=[spec] * 4, out_shape=[jax.ShapeDtypeStruct((r, cols), F32)] * 4, name=name,
        compiler_params=_cparams(("parallel",)),
    )(parts, w, m, v)
    return [a.reshape(shape) for a in res]


def _pick_rows(r, cols):
    for t in (512, 256, 128, 64, 32, 16):
        if r % t == 0 and t * cols <= 512 * 512:
            return t
    return r


def _pack(arrs, dtype, row_mult):
    flat = jnp.concatenate([a.reshape(-1).astype(dtype) for a in arrs])
    n = flat.shape[0]
    per = row_mult * D_MODEL
    total = -(-n // per) * per
    return jnp.pad(flat, (0, total - n)).reshape(total // D_MODEL, D_MODEL)


def _unpack(flat, shapes):
    lead = flat.shape[:-2]
    flat = flat.reshape(lead + (-1,))
    out, off = [], 0
    for shp in shapes:
        n = math.prod(shp)
        out.append(flat[..., off:off + n].reshape(lead + tuple(shp)))
        off += n
    return out


def _to_shards(gfull, axis):
    shp = gfull.shape
    gfull = gfull.reshape(shp[:axis] + (N_DEV, shp[axis] // N_DEV) + shp[axis + 1:])
    return jnp.moveaxis(gfull, axis, 0)


def _from_shards(parts, axis):
    parts = jnp.moveaxis(parts, 0, axis)
    shp = parts.shape
    return parts.reshape(shp[:axis] + (shp[axis] * shp[axis + 1],) + shp[axis + 2:])


def kernel(x, mix_norm_g, w_in, b_gate, ssm_lam_re, ssm_lam_im, ssm_log_step, ssm_b_re, ssm_b_im, ssm_c_re, ssm_c_im, ssm_d, w_glu, b_glu, w_out_ssm, q_norm_g, kv_norm_g, w_q_up, w_kv_up, q_head_g, k_head_g, w_out_mla, w_o, ffn_norm_g, w_ff1, w_ff2, loss_target, m_mix_norm_g, m_w_in, m_b_gate, m_ssm_lam_re, m_ssm_lam_im, m_ssm_log_step, m_ssm_b_re, m_ssm_b_im, m_ssm_c_re, m_ssm_c_im, m_ssm_d, m_w_glu, m_b_glu, m_w_out_ssm, m_q_norm_g, m_kv_norm_g, m_w_q_up, m_w_kv_up, m_q_head_g, m_k_head_g, m_w_out_mla, m_w_o, m_ffn_norm_g, m_w_ff1, m_w_ff2, v_mix_norm_g, v_w_in, v_b_gate, v_ssm_lam_re, v_ssm_lam_im, v_ssm_log_step, v_ssm_b_re, v_ssm_b_im, v_ssm_c_re, v_ssm_c_im, v_ssm_d, v_w_glu, v_b_glu, v_w_out_ssm, v_q_norm_g, v_kv_norm_g, v_w_q_up, v_w_kv_up, v_q_head_g, v_k_head_g, v_w_out_mla, v_w_o, v_ffn_norm_g, v_w_ff1, v_w_ff2):
    args = dict(locals())
    w = {n: args[n] for n in WEIGHTS}
    m = {n: args["m_" + n] for n in WEIGHTS}
    v = {n: args["v_" + n] for n in WEIGHTS}

    def send_weights(li):
        return [w[n][li] if n == "b_gate" else w[n][li].astype(BF16) for n in SHARDED]

    first = _exchange(send_weights(0), [False] * len(SHARDED), "weight_all_gather_0")

    def layer_weights(li, gathered):
        full = {n: _from_shards(pt, SHARD_AXIS[n] - 1) for n, pt in zip(SHARDED, first if li == 0 else gathered)}
        for n in REPLICATED:
            full[n] = w[n][li]
        return full

    def shards(g, names):
        return [_to_shards(g[n], SHARD_AXIS[n] - 1).astype(BF16) for n in names]

    def send_early(g):
        return shards(g, EARLY), [True] * len(EARLY)

    def send_late(g):
        return shards(g, LATE) + [_pack([g[n] for n in REPLICATED], F32, 8)], [True] * len(LATE) + [False]

    loss_part, dx, grads, early, late, pending = _local_step(x[0], loss_target[0], layer_weights, send_weights,
                                                             send_early, send_late)
    late[0] = _exchange(*pending, "grad_exchange_0")
    loss = lax.psum(loss_part, ("x", "y", "c"))

    outs = {}
    for n in SHARDED:
        src, t = (early, EARLY.index(n)) if n in EARLY else (late, LATE.index(n))
        parts = jnp.stack([src[li][t] for li in range(DEPTH)], axis=1)
        for kind, a in zip(("grad", "delta", "new_m", "new_v"), _adamw(parts, w[n], m[n], v[n], "adamw_" + n)):
            outs[kind + "_" + n] = a
    r_parts = jnp.stack([late[li][len(LATE)] for li in range(DEPTH)], axis=1)

    def pack_layers(d):
        return jnp.stack([_pack([d[n][li] for n in REPLICATED], F32, 8) for li in range(DEPTH)], axis=0)

    res = _adamw(r_parts, pack_layers(w), pack_layers(m), pack_layers(v), "adamw_replicated")
    rep_shapes = [w[n].shape[1:] for n in REPLICATED]
    for kind, flat in zip(("grad", "delta", "new_m", "new_v"), res):
        for n, a in zip(REPLICATED, _unpack(flat, rep_shapes)):
            outs[kind + "_" + n] = a
    return (loss, dx[None], *[outs[k + "_" + n] for k in ("grad", "delta", "new_m", "new_v") for n in WEIGHTS])
```

```python
import functools
import math

import jax
import jax.numpy as jnp
from jax import lax
from jax.experimental import pallas as pl
from jax.experimental.pallas import tpu as pltpu

F32 = jnp.float32
BF16 = jnp.bfloat16
_MXU = jnp.bfloat16

D_MODEL = 1024
DEPTH = 4
SSM_WIDTH = 512
SSM_GROUP = 16
SSM_GROUPS = 32
SSM_STATE = 64
MLA_HEADS = 8
QK_NOPE = 64
QK_ROPE = 32
QK_HEAD = 96
V_HEAD = 64
Q_LORA = 384
KV_LORA = 256
ROPE_THETA = 10000.0
D_FF = 4096
EPS = 1e-6
HEAD_PAD = 128
N_DEV = 8
LANES = 128
SUBLANES = 8
CHUNK_GROUPS = 8
N_CHUNKS = SSM_GROUPS // CHUNK_GROUPS
CHUNK_STATE = CHUNK_GROUPS * SSM_STATE

P_GATE, P_U, P_CKV, P_KR, P_CQ = 0, 2048, 2560, 2816, 3072
P_COLS = 3584
IN_U, IN_CQ, IN_CKV, IN_KR, IN_GATE = 0, 512, 896, 1152, 1184
IN_COLS = 3232

ADAM_LR = 0.001
ADAM_B1 = 0.9
ADAM_B2 = 0.999
ADAM_EPS = 1e-08
ADAM_WD = 0.01
ADAM_STEP = 10

VMEM_LIMIT = 56 * 1024 * 1024
TN_OUT_BLOCK_BYTES = 8 * 1024 * 1024
MXU_WIDTH = 256
WIDE_BLOCK_MAX_K = 1024

SHARDED = ("w_in", "b_gate", "w_glu", "w_out_ssm", "w_q_up", "w_kv_up", "w_out_mla", "w_o", "w_ff1", "w_ff2")
SHARD_AXIS = {"w_in": 2, "b_gate": 2, "w_glu": 1, "w_out_ssm": 2, "w_q_up": 2, "w_kv_up": 2, "w_out_mla": 2,
              "w_o": 1, "w_ff1": 2, "w_ff2": 1}
EARLY = ("b_gate", "w_glu", "w_out_ssm", "w_out_mla", "w_o", "w_ff1", "w_ff2")
LATE = ("w_in", "w_q_up", "w_kv_up")
REPLICATED = ("mix_norm_g", "ssm_lam_re", "ssm_lam_im", "ssm_log_step", "ssm_b_re", "ssm_b_im", "ssm_c_re",
              "ssm_c_im", "ssm_d", "b_glu", "q_norm_g", "kv_norm_g", "q_head_g", "k_head_g", "ffn_norm_g")
WEIGHTS = ("mix_norm_g", "w_in", "b_gate", "ssm_lam_re", "ssm_lam_im", "ssm_log_step", "ssm_b_re", "ssm_b_im",
           "ssm_c_re", "ssm_c_im", "ssm_d", "w_glu", "b_glu", "w_out_ssm", "q_norm_g", "kv_norm_g", "w_q_up",
           "w_kv_up", "q_head_g", "k_head_g", "w_out_mla", "w_o", "ffn_norm_g", "w_ff1", "w_ff2")


def _cparams(sem):
    return pltpu.CompilerParams(dimension_semantics=sem, vmem_limit_bytes=VMEM_LIMIT)


def _dot(a, b, dims):
    return lax.dot_general(a.astype(_MXU), b.astype(_MXU), (dims, ((), ())), preferred_element_type=F32)


def _dot_nn(a, b):
    return _dot(a, b, ((1,), (0,)))


def _dot_nt(a, b):
    return _dot(a, b, ((1,), (1,)))


def _dot_tn(a, b):
    return _dot(a, b, ((0,), (0,)))


def _rowwise(fn, rows, consts, outs, accs=(), *, tm=512, name):
    n_rows = rows[0][0].shape[0]
    tm = min(tm, n_rows)
    n_in = len(rows) + len(consts)
    n_o, n_a = len(outs), len(accs)

    def body(*refs):
        res = fn(*[r[...] for r in refs[:n_in]])
        if not isinstance(res, (tuple, list)):
            res = (res,)
        orefs = refs[n_in:]
        for k in range(n_o):
            orefs[k][...] = res[k].astype(orefs[k].dtype)
        if n_a:
            @pl.when(pl.program_id(0) == 0)
            def _():
                for k in range(n_a):
                    orefs[n_o + k][...] = jnp.zeros_like(orefs[n_o + k])
            for k in range(n_a):
                orefs[n_o + k][...] += res[n_o + k]

    in_specs = [pl.BlockSpec((tm, w), functools.partial(lambda i, j: (i, j), j=j)) for (_, w, j) in rows]
    in_specs += [pl.BlockSpec(c.shape, functools.partial(lambda i, nd: (0,) * nd, nd=c.ndim)) for c in consts]
    out_specs = [pl.BlockSpec((tm, w), lambda i: (i, 0)) for (w, _) in outs]
    out_specs += [pl.BlockSpec((1, w), lambda i: (0, 0)) for (w, _) in accs]
    out_shape = [jax.ShapeDtypeStruct((n_rows, w), dt) for (w, dt) in outs]
    out_shape += [jax.ShapeDtypeStruct((1, w), dt) for (w, dt) in accs]
    res = pl.pallas_call(
        body, grid=(n_rows // tm,), in_specs=in_specs, out_specs=out_specs, out_shape=out_shape, name=name,
        compiler_params=_cparams(("arbitrary",) if n_a else ("parallel",)),
    )(*[r[0] for r in rows], *consts)
    return res


def _pick(n, cap):
    if n <= cap:
        return n
    for unit in (MXU_WIDTH, LANES):
        best = 0
        for t in range(unit, cap + 1, unit):
            if n % t == 0:
                best = t
        if best:
            return best
    return n


def _mm(a, b, transpose_b, extras, epilogue, out_dtypes, name):
    m, k = a.shape
    n = b.shape[0] if transpose_b else b.shape[1]
    tm, tn = min(512, m), _pick(n, 2048 if k <= WIDE_BLOCK_MAX_K else 1024)
    n_in = 2 + len(extras)

    def body(*refs):
        acc = (_dot_nt if transpose_b else _dot_nn)(refs[0][...], refs[1][...])
        res = epilogue(acc, *[r[...] for r in refs[2:n_in]]) if epilogue is not None else acc
        if not isinstance(res, (tuple, list)):
            res = (res,)
        for o_ref, val in zip(refs[n_in:], res):
            o_ref[...] = val.astype(o_ref.dtype)

    blk = pl.BlockSpec((tm, tn), lambda j, i: (i, j))
    b_spec = pl.BlockSpec((tn, k), lambda j, i: (j, 0)) if transpose_b else pl.BlockSpec((k, tn), lambda j, i: (0, j))
    res = pl.pallas_call(
        body, grid=(n // tn, m // tm),
        in_specs=[pl.BlockSpec((tm, k), lambda j, i: (i, 0)), b_spec] + [blk] * len(extras),
        out_specs=[blk] * len(out_dtypes), out_shape=[jax.ShapeDtypeStruct((m, n), dt) for dt in out_dtypes],
        name=name, compiler_params=_cparams(("parallel", "parallel")),
    )(a, b, *extras)
    return res[0] if len(out_dtypes) == 1 else res


def _mm_nn(a, b, *, add=None, extras=(), epilogue=None, out_dtypes=(F32,), name):
    if add is not None:
        extras, epilogue = (add,), (lambda acc, r: acc + r)
    return _mm(a, b, False, tuple(extras), epilogue, out_dtypes, name)


def _mm_nt(a, b, *, extras=(), epilogue=None, out_dtypes=(F32,), name):
    return _mm(a, b, True, tuple(extras), epilogue, out_dtypes, name)


def _mm_tn(a, b, *, a_cols=None, name):
    s = a.shape[0]
    n = b.shape[1]
    mw, mj = (a.shape[1], 0) if a_cols is None else a_cols
    ts = min(512, s)
    tm = _pick(mw, 1024)
    tn = _pick(n, max(1024, TN_OUT_BLOCK_BYTES // (4 * tm)))
    n_mb = mw // tm

    def body(a_ref, b_ref, o_ref):
        @pl.when(pl.program_id(2) == 0)
        def _():
            o_ref[...] = jnp.zeros_like(o_ref)
        o_ref[...] += _dot_tn(a_ref[...], b_ref[...])

    return pl.pallas_call(
        body, grid=(n_mb, n // tn, s // ts),
        in_specs=[pl.BlockSpec((ts, tm), lambda i, j, t: (t, mj * n_mb + i)), pl.BlockSpec((ts, tn), lambda i, j, t: (t, j))],
        out_specs=pl.BlockSpec((tm, tn), lambda i, j, t: (i, j)),
        out_shape=jax.ShapeDtypeStruct((mw, n), F32), name=name,
        compiler_params=_cparams(("parallel", "parallel", "arbitrary")),
    )(a, b)


def _rms(x, g, n):
    r = lax.rsqrt(jnp.sum(x * x, axis=-1, keepdims=True) * (1.0 / n) + EPS)
    return x * r * g


def _rms_bwd(x, g, dy, n):
    r = lax.rsqrt(jnp.sum(x * x, axis=-1, keepdims=True) * (1.0 / n) + EPS)
    xr = x * r
    dyg = dy * g
    dx = r * dyg - xr * (r * r) * (jnp.sum(dyg * x, axis=-1, keepdims=True) * (1.0 / n))
    return dx, jnp.sum(dy * xr, axis=0, keepdims=True)


def _gelu(x):
    c = math.sqrt(2.0 / math.pi)
    return 0.5 * x * (1.0 + jnp.tanh(c * (x + 0.044715 * (x * x * x))))


def _gelu_grad(x):
    c = math.sqrt(2.0 / math.pi)
    t = jnp.tanh(c * (x + 0.044715 * (x * x * x)))
    return 0.5 * (1.0 + t) + 0.5 * x * (1.0 - t * t) * (c * (1.0 + 3.0 * 0.044715 * (x * x)))


def _sigmoid(x):
    return 1.0 / (1.0 + jnp.exp(-x))


def _rope(x, cf, sa, sb):
    return x * cf + pltpu.roll(x, HEAD_PAD - QK_ROPE // 2, 1) * sa + pltpu.roll(x, QK_ROPE // 2, 1) * sb


def _rope_t(d, cf, sa, sb):
    return d * cf + pltpu.roll(d * sa, QK_ROPE // 2, 1) + pltpu.roll(d * sb, HEAD_PAD - QK_ROPE // 2, 1)


def _scan_tables(ar, ai, reverse):
    ar = ar.reshape(N_CHUNKS, CHUNK_STATE)
    ai = ai.reshape(N_CHUNKS, CHUNK_STATE)
    pr, pi = [ar], [ai]
    for _ in range(SUBLANES - 1):
        pr, pi = pr + [pr[-1] * ar - pi[-1] * ai], pi + [pr[-1] * ai + pi[-1] * ar]
    row = jnp.arange(SUBLANES)[None, :, None]
    tiles = []
    for k in (1, 2, 4):
        mask = (row <= SUBLANES - 1 - k) if reverse else (row >= k)
        tiles.append(jnp.where(mask, pr[k - 1][:, None, :], 0.0))
        tiles.append(jnp.where(mask, pi[k - 1][:, None, :], 0.0))
    order = list(range(SUBLANES))[::-1] if reverse else list(range(SUBLANES))
    tiles.append(jnp.stack([pr[j] for j in order], axis=1))
    tiles.append(jnp.stack([pi[j] for j in order], axis=1))
    return jnp.stack(tiles, axis=1).astype(F32)


def _slab_scan(xr, xi, coef, carry_r, carry_i, reverse):
    for idx, k in enumerate((1, 2, 4)):
        sh = SUBLANES - k if reverse else k
        sr, si = pltpu.roll(xr, sh, 0), pltpu.roll(xi, sh, 0)
        cr, ci = coef[2 * idx], coef[2 * idx + 1]
        xr, xi = xr + cr * sr - ci * si, xi + cr * si + ci * sr
    pr, pi = coef[6], coef[7]
    xr = xr + pr * carry_r - pi * carry_i
    xi = xi + pr * carry_i + pi * carry_r
    return xr, xi


def _s5_scan_fwd(proj, b_blk, c_blk, coef, reverse, name):
    s = proj.shape[0]
    t_blk = min(512, s)
    n_t = s // t_blk
    n_slab = t_blk // SUBLANES
    last = 0 if reverse else SUBLANES - 1

    def tmap(t):
        return n_t - 1 - t if reverse else t

    def body(u_ref, b_ref, c_ref, coef_ref, y_ref, xr_ref, xi_ref, carry_ref):
        @pl.when(pl.program_id(1) == 0)
        def _():
            carry_ref[...] = jnp.zeros_like(carry_ref)
        bu = _dot_nn(u_ref[...], b_ref[0])
        xr_ref[...] = bu[:, :CHUNK_STATE]
        xi_ref[...] = bu[:, CHUNK_STATE:]
        coef_v = [coef_ref[0, k] for k in range(8)]

        def slab(i, carry):
            sl = (n_slab - 1 - i) if reverse else i
            rows = pl.ds(pl.multiple_of(sl * SUBLANES, SUBLANES), SUBLANES)
            xr, xi = _slab_scan(xr_ref[rows, :], xi_ref[rows, :], coef_v, carry[0], carry[1], reverse)
            xr_ref[rows, :] = xr
            xi_ref[rows, :] = xi
            return (jnp.broadcast_to(xr[last:last + 1, :], xr.shape), jnp.broadcast_to(xi[last:last + 1, :], xi.shape))

        cr, ci = lax.fori_loop(0, n_slab, slab, (carry_ref[0], carry_ref[1]))
        carry_ref[0] = cr
        carry_ref[1] = ci
        y_ref[...] = _dot_nn(xr_ref[...], c_ref[0, :CHUNK_STATE, :]) + _dot_nn(xi_ref[...], c_ref[0, CHUNK_STATE:, :])

    u_blk0 = P_U // LANES
    return pl.pallas_call(
        body, grid=(N_CHUNKS, n_t),
        in_specs=[pl.BlockSpec((t_blk, LANES), lambda c, t: (tmap(t), u_blk0 + c)),
                  pl.BlockSpec((1, LANES, 2 * CHUNK_STATE), lambda c, t: (c, 0, 0)),
                  pl.BlockSpec((1, 2 * CHUNK_STATE, LANES), lambda c, t: (c, 0, 0)),
                  pl.BlockSpec((1, 8, SUBLANES, CHUNK_STATE), lambda c, t: (c, 0, 0, 0))],
        out_specs=[pl.BlockSpec((t_blk, LANES), lambda c, t: (tmap(t), c)),
                   pl.BlockSpec((t_blk, CHUNK_STATE), lambda c, t: (tmap(t), c)),
                   pl.BlockSpec((t_blk, CHUNK_STATE), lambda c, t: (tmap(t), c))],
        out_shape=[jax.ShapeDtypeStruct((s, SSM_WIDTH), F32),
                   jax.ShapeDtypeStruct((s, N_CHUNKS * CHUNK_STATE), F32),
                   jax.ShapeDtypeStruct((s, N_CHUNKS * CHUNK_STATE), F32)],
        scratch_shapes=[pltpu.VMEM((2, SUBLANES, CHUNK_STATE), F32)],
        name=name, compiler_params=_cparams(("parallel", "arbitrary")),
    )(proj, b_blk, c_blk, coef)


def _s5_scan_bwd(dy, proj, x_re, x_im, b_blk, c_blk, coef, reverse, name):
    s = dy.shape[0]
    t_blk = min(512, s)
    n_t = s // t_blk
    n_slab = t_blk // SUBLANES
    last = 0 if reverse else SUBLANES - 1
    first = SUBLANES - 1 if reverse else 0

    def tmap(t):
        return n_t - 1 - t if reverse else t

    def body(dy_ref, u_ref, xr_ref, xi_ref, b_ref, c_ref, coef_ref, du_ref, da_ref, db_ref, dc_ref,
             carry_ref, lr_ref, li_ref):
        @pl.when(pl.program_id(1) == 0)
        def _():
            carry_ref[...] = jnp.zeros_like(carry_ref)
            da_ref[...] = jnp.zeros_like(da_ref)
            db_ref[...] = jnp.zeros_like(db_ref)
            dc_ref[...] = jnp.zeros_like(dc_ref)
        g = _dot_nt(dy_ref[...], c_ref[0])
        lr_ref[...] = g[:, :CHUNK_STATE]
        li_ref[...] = g[:, CHUNK_STATE:]
        coef_v = [coef_ref[0, k] for k in range(8)]
        row = lax.broadcasted_iota(jnp.int32, (SUBLANES, CHUNK_STATE), 0)
        sh_prev = SUBLANES - 1 if reverse else 1

        def slab(i, carry):
            cr, ci, ar_acc, ai_acc = carry
            sl = (n_slab - 1 - i) if reverse else i
            rows = pl.ds(pl.multiple_of(sl * SUBLANES, SUBLANES), SUBLANES)
            lr, li = _slab_scan(lr_ref[rows, :], li_ref[rows, :], coef_v, cr, ci, reverse)
            lr_ref[rows, :] = lr
            li_ref[rows, :] = li
            pr = jnp.where(row == first, cr, pltpu.roll(lr, sh_prev, 0))
            pi = jnp.where(row == first, ci, pltpu.roll(li, sh_prev, 0))
            xr, xi = xr_ref[rows, :], xi_ref[rows, :]
            ar_acc = ar_acc + xr * pr + xi * pi
            ai_acc = ai_acc + xr * pi - xi * pr
            return (jnp.broadcast_to(lr[last:last + 1, :], lr.shape), jnp.broadcast_to(li[last:last + 1, :], li.shape),
                    ar_acc, ai_acc)

        zero = jnp.zeros((SUBLANES, CHUNK_STATE), F32)
        cr, ci, ar_acc, ai_acc = lax.fori_loop(0, n_slab, slab, (carry_ref[0], carry_ref[1], zero, zero))
        carry_ref[0] = cr
        carry_ref[1] = ci
        da_ref[0, :, :CHUNK_STATE] += ar_acc
        da_ref[0, :, CHUNK_STATE:] += ai_acc
        lam_r, lam_i = lr_ref[...], li_ref[...]
        u = u_ref[...]
        du_ref[...] = _dot_nt(lam_r, b_ref[0, :, :CHUNK_STATE]) + _dot_nt(lam_i, b_ref[0, :, CHUNK_STATE:])
        db_ref[0, :, :CHUNK_STATE] += _dot_tn(u, lam_r)
        db_ref[0, :, CHUNK_STATE:] += _dot_tn(u, lam_i)
        dyv = dy_ref[...]
        dc_ref[0, :CHUNK_STATE, :] += _dot_tn(xr_ref[...], dyv)
        dc_ref[0, CHUNK_STATE:, :] += _dot_tn(xi_ref[...], dyv)

    u_blk0 = P_U // LANES
    return pl.pallas_call(
        body, grid=(N_CHUNKS, n_t),
        in_specs=[pl.BlockSpec((t_blk, LANES), lambda c, t: (tmap(t), c)),
                  pl.BlockSpec((t_blk, LANES), lambda c, t: (tmap(t), u_blk0 + c)),
                  pl.BlockSpec((t_blk, CHUNK_STATE), lambda c, t: (tmap(t), c)),
                  pl.BlockSpec((t_blk, CHUNK_STATE), lambda c, t: (tmap(t), c)),
                  pl.BlockSpec((1, LANES, 2 * CHUNK_STATE), lambda c, t: (c, 0, 0)),
                  pl.BlockSpec((1, 2 * CHUNK_STATE, LANES), lambda c, t: (c, 0, 0)),
                  pl.BlockSpec((1, 8, SUBLANES, CHUNK_STATE), lambda c, t: (c, 0, 0, 0))],
        out_specs=[pl.BlockSpec((t_blk, LANES), lambda c, t: (tmap(t), c)),
                   pl.BlockSpec((1, SUBLANES, 2 * CHUNK_STATE), lambda c, t: (c, 0, 0)),
                   pl.BlockSpec((1, LANES, 2 * CHUNK_STATE), lambda c, t: (c, 0, 0)),
                   pl.BlockSpec((1, 2 * CHUNK_STATE, LANES), lambda c, t: (c, 0, 0))],
        out_shape=[jax.ShapeDtypeStruct((s, SSM_WIDTH), F32),
                   jax.ShapeDtypeStruct((N_CHUNKS, SUBLANES, 2 * CHUNK_STATE), F32),
                   jax.ShapeDtypeStruct((N_CHUNKS, LANES, 2 * CHUNK_STATE), F32),
                   jax.ShapeDtypeStruct((N_CHUNKS, 2 * CHUNK_STATE, LANES), F32)],
        scratch_shapes=[pltpu.VMEM((2, SUBLANES, CHUNK_STATE), F32), pltpu.VMEM((t_blk, CHUNK_STATE), F32),
                        pltpu.VMEM((t_blk, CHUNK_STATE), F32)],
        name=name, compiler_params=_cparams(("parallel", "arbitrary")),
    )(dy, proj, x_re, x_im, b_blk, c_blk, coef)


def _zoh(lam_re, lam_im, log_step, b_re, b_im):
    step = jnp.exp(log_step)[:, None]
    mag = jnp.exp(lam_re * step)
    abar_r = mag * jnp.cos(lam_im * step)
    abar_i = mag * jnp.sin(lam_im * step)
    nr = abar_r - 1.0
    ni = abar_i
    den = lam_re * lam_re + lam_im * lam_im
    fr = (nr * lam_re + ni * lam_im) / den
    fi = (ni * lam_re - nr * lam_im) / den
    bbar_r = fr[..., None] * b_re - fi[..., None] * b_im
    bbar_i = fr[..., None] * b_im + fi[..., None] * b_re
    return abar_r, abar_i, bbar_r, bbar_i


def _b_block(bbar_r, bbar_i):
    eye = jnp.eye(CHUNK_GROUPS, dtype=F32)

    def one(b):
        b = b.reshape(N_CHUNKS, CHUNK_GROUPS, SSM_STATE, SSM_GROUP)
        return jnp.einsum("cgnp,gh->cgphn", b, eye).reshape(N_CHUNKS, LANES, CHUNK_STATE)

    return jnp.concatenate([one(bbar_r), one(bbar_i)], axis=2)


def _b_unblock(db):
    eye = jnp.eye(CHUNK_GROUPS, dtype=F32)

    def one(d):
        d = d.reshape(N_CHUNKS, CHUNK_GROUPS, SSM_GROUP, CHUNK_GROUPS, SSM_STATE)
        return jnp.einsum("cgphn,gh->cgnp", d, eye).reshape(SSM_GROUPS, SSM_STATE, SSM_GROUP)

    return one(db[:, :, :CHUNK_STATE]), one(db[:, :, CHUNK_STATE:])


def _c_block(c_re, c_im):
    eye = jnp.eye(CHUNK_GROUPS, dtype=F32)

    def one(c):
        c = c.reshape(N_CHUNKS, CHUNK_GROUPS, SSM_GROUP, SSM_STATE)
        return jnp.einsum("cgpn,gh->cgnhp", c, eye).reshape(N_CHUNKS, CHUNK_STATE, LANES)

    return jnp.concatenate([one(c_re), -one(c_im)], axis=1)


def _c_unblock(dc):
    eye = jnp.eye(CHUNK_GROUPS, dtype=F32)

    def one(d):
        d = d.reshape(N_CHUNKS, CHUNK_GROUPS, SSM_STATE, CHUNK_GROUPS, SSM_GROUP)
        return jnp.einsum("cgnhp,gh->cgpn", d, eye).reshape(SSM_GROUPS, SSM_GROUP, SSM_STATE)

    return one(dc[:, :CHUNK_STATE, :]), -one(dc[:, CHUNK_STATE:, :])


def _exchange_copies(ins, outs, send_sems, recv_sems, local_sems, scatter):
    x, y, c = lax.axis_index("x"), lax.axis_index("y"), lax.axis_index("c")
    me = 4 * x + 2 * y + c
    copies = [pltpu.make_async_copy(ins[t].at[me] if scatter[t] else ins[t], outs[t].at[me], local_sems.at[t])
              for t in range(len(ins))]
    for k in range(1, N_DEV):
        peer = (x ^ ((k >> 2) & 1), y ^ ((k >> 1) & 1), c ^ (k & 1))
        peer_idx = 4 * peer[0] + 2 * peer[1] + peer[2]
        for t in range(len(ins)):
            copies.append(pltpu.make_async_remote_copy(
                src_ref=ins[t].at[peer_idx] if scatter[t] else ins[t], dst_ref=outs[t].at[me],
                send_sem=send_sems.at[t, k], recv_sem=recv_sems.at[t, k], device_id=peer,
                device_id_type=pl.DeviceIdType.MESH))
    return copies


def _call_with_exchange(body, *, grid, in_specs, out_specs, out_shape, scratch_shapes, args, semantics, name,
                        exchange=None, scatter=()):
    if exchange is None:
        return pl.pallas_call(body, grid=grid, in_specs=in_specs, out_specs=out_specs, out_shape=out_shape,
                              scratch_shapes=scratch_shapes, name=name, compiler_params=_cparams(semantics))(*args)
    n, n_in, n_out, n_scr = len(exchange), len(in_specs), len(out_specs), len(scratch_shapes)

    def wrapped(*refs):
        ins, refs = refs[:n_in], refs[n_in:]
        c_ins, refs = refs[:n], refs[n:]
        outs, refs = refs[:n_out], refs[n_out:]
        c_outs, refs = refs[:n], refs[n:]
        scr, sems = refs[:n_scr], refs[n_scr:]
        ids = [pl.program_id(a) for a in range(len(grid))]
        first = functools.reduce(jnp.logical_and, [i == 0 for i in ids])
        last = functools.reduce(jnp.logical_and, [i == g - 1 for i, g in zip(ids, grid)])

        @pl.when(first)
        def _():
            for cp in _exchange_copies(c_ins, c_outs, *sems, scatter):
                cp.start()

        body(*ins, *outs, *scr)

        @pl.when(last)
        def _():
            for cp in _exchange_copies(c_ins, c_outs, *sems, scatter):
                cp.wait()

    shapes = [tuple(b.shape[1:]) if sc else tuple(b.shape) for b, sc in zip(exchange, scatter)]
    hbm = pl.BlockSpec(memory_space=pl.ANY)
    res = pl.pallas_call(
        wrapped, grid=grid, in_specs=list(in_specs) + [hbm] * n, out_specs=list(out_specs) + [hbm] * n,
        out_shape=list(out_shape) + [jax.ShapeDtypeStruct((N_DEV,) + s, b.dtype) for s, b in zip(shapes, exchange)],
        scratch_shapes=list(scratch_shapes) + [pltpu.SemaphoreType.DMA((n, N_DEV)), pltpu.SemaphoreType.DMA((n, N_DEV)),
                                              pltpu.SemaphoreType.DMA((n,))],
        name=name, compiler_params=pltpu.CompilerParams(dimension_semantics=("arbitrary",) * len(grid),
                                                        vmem_limit_bytes=VMEM_LIMIT, has_side_effects=True),
    )(*args, *exchange)
    return res


def _attn_fwd(q, k, v, name, exchange=None):
    s = q.shape[0]
    tq = min(2048, s)
    tk = min(1024, s)
    n_k = s // tk

    def body(q_ref, k_ref, v_ref, o_ref, lse_ref, m_ref, acc_ref):
        m_ref[...] = jnp.full_like(m_ref, -jnp.inf)
        acc_ref[...] = jnp.zeros_like(acc_ref)
        qv = q_ref[...]

        def step(j, _):
            rows = pl.ds(pl.multiple_of(j * tk, tk), tk)
            sc = _dot_nt(qv, k_ref[rows, :])
            m_old = m_ref[...]
            m_new = jnp.maximum(m_old, jnp.max(sc, axis=1, keepdims=True))
            p = jnp.exp(sc - m_new)
            alpha = jnp.exp(m_old - m_new)
            acc_ref[...] = alpha * acc_ref[...] + _dot_nn(p, v_ref[rows, :])
            m_ref[...] = m_new
            return 0

        lax.fori_loop(0, n_k, step, 0, unroll=min(8, n_k))
        acc = acc_ref[...]
        l = acc[:, V_HEAD:V_HEAD + 1]
        o_ref[...] = acc / l
        lse = m_ref[...] + jnp.log(l)
        lse_ref[0] = jnp.broadcast_to(lse, (tq, LANES)).T[:SUBLANES, :]

    return _call_with_exchange(
        body, grid=(MLA_HEADS, s // tq),
        in_specs=[pl.BlockSpec((tq, HEAD_PAD), lambda h, i: (i, h)),
                  pl.BlockSpec((s, HEAD_PAD), lambda h, i: (0, h)),
                  pl.BlockSpec((s, HEAD_PAD), lambda h, i: (0, h))],
        out_specs=[pl.BlockSpec((tq, HEAD_PAD), lambda h, i: (i, h)),
                   pl.BlockSpec((1, SUBLANES, tq), lambda h, i: (h, 0, i))],
        out_shape=[jax.ShapeDtypeStruct((s, MLA_HEADS * HEAD_PAD), F32),
                   jax.ShapeDtypeStruct((MLA_HEADS, SUBLANES, s), F32)],
        scratch_shapes=[pltpu.VMEM((tq, 1), F32), pltpu.VMEM((tq, HEAD_PAD), F32)],
        args=(q, k, v), semantics=("parallel", "parallel"), name=name, exchange=exchange,
        scatter=(False,) * len(exchange or ()))


def _attn_bwd(q, k, v, o, do, lse, name, exchange=None, scatter=()):
    s = q.shape[0]
    tq = min(2048, s)
    tk = min(2048, s)

    def body(q_ref, k_ref, v_ref, o_ref, do_ref, lse_ref, dq_ref, dk_ref, dv_ref):
        j, i = pl.program_id(1), pl.program_id(2)

        @pl.when(jnp.logical_and(j == 0, i == 0))
        def _():
            dq_ref[...] = jnp.zeros_like(dq_ref)

        @pl.when(i == 0)
        def _():
            dk_ref[...] = jnp.zeros_like(dk_ref)
            dv_ref[...] = jnp.zeros_like(dv_ref)

        qv, kv, vv, dov = q_ref[...], k_ref[...], v_ref[...], do_ref[...]
        delta_col = jnp.sum(dov * o_ref[...], axis=1, keepdims=True)
        delta = jnp.broadcast_to(delta_col, (tq, LANES)).T[:1, :]
        st = _dot_nt(kv, qv)
        pt = jnp.exp(st - lse_ref[0, :1, :])
        dv_ref[...] += _dot_nn(pt, dov)
        dpt = _dot_nt(vv, dov)
        dst = pt * (dpt - delta)
        dk_ref[...] += _dot_nn(dst, qv)
        rows = pl.ds(pl.multiple_of(i * tq, tq), tq)
        dq_ref[rows, :] += _dot_tn(dst, kv)

    return _call_with_exchange(
        body, grid=(MLA_HEADS, s // tk, s // tq),
        in_specs=[pl.BlockSpec((tq, HEAD_PAD), lambda h, j, i: (i, h)),
                  pl.BlockSpec((tk, HEAD_PAD), lambda h, j, i: (j, h)),
                  pl.BlockSpec((tk, HEAD_PAD), lambda h, j, i: (j, h)),
                  pl.BlockSpec((tq, HEAD_PAD), lambda h, j, i: (i, h)),
                  pl.BlockSpec((tq, HEAD_PAD), lambda h, j, i: (i, h)),
                  pl.BlockSpec((1, SUBLANES, tq), lambda h, j, i: (h, 0, i))],
        out_specs=[pl.BlockSpec((s, HEAD_PAD), lambda h, j, i: (0, h)),
                   pl.BlockSpec((tk, HEAD_PAD), lambda h, j, i: (j, h)),
                   pl.BlockSpec((tk, HEAD_PAD), lambda h, j, i: (j, h))],
        out_shape=[jax.ShapeDtypeStruct((s, MLA_HEADS * HEAD_PAD), F32)] * 3, scratch_shapes=[],
        args=(q, k, v, o, do, lse), semantics=("parallel", "arbitrary", "arbitrary"), name=name,
        exchange=exchange, scatter=scatter)


def _pad_w_in(w):
    z = functools.partial(jnp.zeros, dtype=w.dtype)
    return jnp.concatenate([
        w[:, IN_GATE:IN_COLS], w[:, IN_U:IN_CQ], w[:, IN_CKV:IN_KR],
        z((D_MODEL, QK_NOPE)), w[:, IN_KR:IN_GATE], z((D_MODEL, HEAD_PAD - QK_HEAD)),
        z((D_MODEL, P_CQ - P_KR - HEAD_PAD)), w[:, IN_CQ:IN_CKV], z((D_MODEL, P_COLS - P_CQ - Q_LORA))], axis=1)


def _unpad_w_in(d):
    return jnp.concatenate([d[:, P_U:P_CKV], d[:, P_CQ:P_CQ + Q_LORA], d[:, P_CKV:P_KR],
                            d[:, P_KR + QK_NOPE:P_KR + QK_HEAD], d[:, P_GATE:P_U]], axis=1)


def _pad_heads_cols(w, real):
    k = w.shape[0]
    w = w.reshape(k, MLA_HEADS, real)
    return jnp.pad(w, ((0, 0), (0, 0), (0, HEAD_PAD - real))).reshape(k, MLA_HEADS * HEAD_PAD)


def _unpad_heads_cols(d, real):
    k = d.shape[0]
    return d.reshape(k, MLA_HEADS, HEAD_PAD)[:, :, :real].reshape(k, MLA_HEADS * real)


def _pad_kv(w):
    w = w.reshape(KV_LORA, MLA_HEADS, QK_NOPE + V_HEAD)
    kn = jnp.pad(w[:, :, :QK_NOPE], ((0, 0), (0, 0), (0, HEAD_PAD - QK_NOPE)))
    vv = jnp.pad(w[:, :, QK_NOPE:], ((0, 0), (0, 0), (0, HEAD_PAD - V_HEAD)))
    return jnp.concatenate([kn.reshape(KV_LORA, -1), vv.reshape(KV_LORA, -1)], axis=1)


def _unpad_kv(d):
    n = MLA_HEADS * HEAD_PAD
    kn = d[:, :n].reshape(KV_LORA, MLA_HEADS, HEAD_PAD)[:, :, :QK_NOPE]
    vv = d[:, n:].reshape(KV_LORA, MLA_HEADS, HEAD_PAD)[:, :, :V_HEAD]
    return jnp.concatenate([kn, vv], axis=2).reshape(KV_LORA, MLA_HEADS * (QK_NOPE + V_HEAD))


def _pad_out_mla(w):
    w = w.reshape(MLA_HEADS, V_HEAD, D_MODEL)
    return jnp.pad(w, ((0, 0), (0, HEAD_PAD - V_HEAD), (0, 0))).reshape(MLA_HEADS * HEAD_PAD, D_MODEL)


def _unpad_out_mla(d):
    return d.reshape(MLA_HEADS, HEAD_PAD, D_MODEL)[:, :V_HEAD, :].reshape(MLA_HEADS * V_HEAD, D_MODEL)


def _rope_tables(seq):
    half = QK_ROPE // 2
    inv_freq = ROPE_THETA ** (-jnp.arange(half, dtype=F32) / half)
    ang = jnp.arange(seq, dtype=F32)[:, None] * inv_freq[None, :]
    cos, sin = jnp.cos(ang), jnp.sin(ang)
    one, zero = jnp.ones((seq, QK_NOPE), F32), jnp.zeros((seq, half), F32)
    tail1, tail0 = jnp.ones((seq, HEAD_PAD - QK_HEAD), F32), jnp.zeros((seq, HEAD_PAD - QK_HEAD), F32)
    cf = jnp.concatenate([one, cos, cos, tail1], axis=1)
    sa = jnp.concatenate([0.0 * one, -sin, zero, tail0], axis=1)
    sb = jnp.concatenate([0.0 * one, zero, sin, tail0], axis=1)
    return cf, sa, sb


def _prep_layer(w):
    p = {}
    p["w_in_p"] = _pad_w_in(w["w_in"])
    p["w_glu"] = w["w_glu"]
    p["w_out_ssm"] = w["w_out_ssm"]
    p["w_q_p"] = _pad_heads_cols(w["w_q_up"], QK_HEAD)
    p["w_kv_p"] = _pad_kv(w["w_kv_up"])
    p["w_out_mla_p"] = _pad_out_mla(w["w_out_mla"])
    p["w_o"] = w["w_o"]
    p["w_ff1"] = w["w_ff1"]
    p["w_ff2"] = w["w_ff2"]
    p["mix_g"] = w["mix_norm_g"].reshape(1, D_MODEL)
    p["ffn_g"] = w["ffn_norm_g"].reshape(1, D_MODEL)
    p["b_gate"] = w["b_gate"]
    p["b_glu"] = w["b_glu"].reshape(1, SSM_WIDTH)
    p["d"] = w["ssm_d"].reshape(1, SSM_WIDTH)
    p["q_g"] = w["q_norm_g"].reshape(1, Q_LORA)
    p["kv_g"] = w["kv_norm_g"].reshape(1, KV_LORA)
    p["qh_g"] = jnp.pad(w["q_head_g"], (0, HEAD_PAD - QK_HEAD)).reshape(1, HEAD_PAD)
    p["kh_g"] = jnp.pad(w["k_head_g"], (0, HEAD_PAD - QK_HEAD)).reshape(1, HEAD_PAD)
    p["c_blk"] = _c_block(w["ssm_c_re"], w["ssm_c_im"]).astype(BF16)
    zoh, p["zoh_vjp"] = [], []
    for dr in range(2):
        out, vjp = jax.vjp(_zoh, w["ssm_lam_re"][dr], w["ssm_lam_im"][dr], w["ssm_log_step"][dr],
                           w["ssm_b_re"][dr], w["ssm_b_im"][dr])
        zoh.append(out)
        p["zoh_vjp"].append(vjp)
    p["b_blk"] = [_b_block(z[2], z[3]).astype(BF16) for z in zoh]
    p["coef_fwd"] = [_scan_tables(zoh[0][0], zoh[0][1], False), _scan_tables(zoh[1][0], zoh[1][1], True)]
    p["coef_adj"] = [_scan_tables(zoh[0][0], -zoh[0][1], True), _scan_tables(zoh[1][0], -zoh[1][1], False)]
    return p


def _head_prep_fwd(q_raw, kv_raw, proj, tabs, p, li):
    cf, sa, sb = tabs
    scale = QK_HEAD ** -0.5

    def fn(qr, kn, vv, kr, cfv, sav, sbv, gq, gk):
        qo, ko = [], []
        for h in range(MLA_HEADS):
            sl = slice(h * HEAD_PAD, (h + 1) * HEAD_PAD)
            qo.append(_rope(_rms(qr[:, sl], gq, QK_HEAD), cfv, sav, sbv) * scale)
            ko.append(_rope(_rms(kn[:, sl] + kr, gk, QK_HEAD), cfv, sav, sbv))
        lane = lax.broadcasted_iota(jnp.int32, vv.shape, 1)
        return jnp.concatenate(qo, axis=1), jnp.concatenate(ko, axis=1), jnp.where(lane % HEAD_PAD == V_HEAD, 1.0, vv)

    n = MLA_HEADS * HEAD_PAD
    return _rowwise(fn, [(q_raw, n, 0), (kv_raw, n, 0), (kv_raw, n, 1), (proj, HEAD_PAD, P_KR // HEAD_PAD),
                         (cf, HEAD_PAD, 0), (sa, HEAD_PAD, 0), (sb, HEAD_PAD, 0)], [p["qh_g"], p["kh_g"]],
                    [(n, BF16), (n, BF16), (n, BF16)], tm=256, name=f"head_prep_fwd_{li}")


def _head_prep_bwd(dq, dk, dv, q_raw, kv_raw, proj, tabs, p, li):
    cf, sa, sb = tabs
    scale = QK_HEAD ** -0.5

    def fn(dqv, dkv, dvv, qr, kn, kr, cfv, sav, sbv, gq, gk):
        dqo, dko = [], []
        dkr = jnp.zeros_like(kr)
        dgq = jnp.zeros((1, HEAD_PAD), F32)
        dgk = jnp.zeros((1, HEAD_PAD), F32)
        for h in range(MLA_HEADS):
            sl = slice(h * HEAD_PAD, (h + 1) * HEAD_PAD)
            dx, dg = _rms_bwd(qr[:, sl], gq, _rope_t(dqv[:, sl] * scale, cfv, sav, sbv), QK_HEAD)
            dqo.append(dx)
            dgq = dgq + dg
            dx, dg = _rms_bwd(kn[:, sl] + kr, gk, _rope_t(dkv[:, sl], cfv, sav, sbv), QK_HEAD)
            dko.append(dx)
            dkr = dkr + dx
            dgk = dgk + dg
        return jnp.concatenate(dqo, axis=1), jnp.concatenate(dko + [dvv], axis=1), dkr, dgq, dgk

    n = MLA_HEADS * HEAD_PAD
    return _rowwise(fn, [(dq, n, 0), (dk, n, 0), (dv, n, 0), (q_raw, n, 0), (kv_raw, n, 0),
                         (proj, HEAD_PAD, P_KR // HEAD_PAD), (cf, HEAD_PAD, 0), (sa, HEAD_PAD, 0), (sb, HEAD_PAD, 0)],
                    [p["qh_g"], p["kh_g"]], [(n, BF16), (2 * n, BF16), (HEAD_PAD, BF16)],
                    [(HEAD_PAD, F32), (HEAD_PAD, F32)], tm=256, name=f"head_prep_bwd_{li}")


def _layer_fwd(x, p, tabs, li, exchange=None):
    sv = {"x": x}
    h = _rowwise(lambda xv, g: _rms(xv, g, D_MODEL), [(x, D_MODEL, 0)], [p["mix_g"]], [(D_MODEL, BF16)],
                 name=f"mix_norm_{li}")[0]
    proj = _mm_nn(h, p["w_in_p"], name=f"in_proj_{li}")
    sv["h"], sv["proj"] = h, proj
    y_f, xr_f, xi_f = _s5_scan_fwd(proj, p["b_blk"][0], p["c_blk"], p["coef_fwd"][0], False, f"s5_fwd_f_{li}")
    y_b, xr_b, xi_b = _s5_scan_fwd(proj, p["b_blk"][1], p["c_blk"], p["coef_fwd"][1], True, f"s5_fwd_b_{li}")
    sv["states"] = [(xr_f, xi_f), (xr_b, xi_b)]
    y_raw, yg = _rowwise(lambda a, b, u, d: (a + b + d * u, _gelu(a + b + d * u)),
                         [(y_f, SSM_WIDTH, 0), (y_b, SSM_WIDTH, 0), (proj, SSM_WIDTH, P_U // SSM_WIDTH)], [p["d"]],
                         [(SSM_WIDTH, F32), (SSM_WIDTH, BF16)], name=f"s5_gelu_{li}")
    z = _mm_nn(yg, p["w_glu"], name=f"glu_proj_{li}")
    y_ssm = _rowwise(lambda yr, zv, b: _gelu(yr) * _sigmoid(zv + b), [(y_raw, SSM_WIDTH, 0), (z, SSM_WIDTH, 0)],
                     [p["b_glu"]], [(SSM_WIDTH, BF16)], name=f"glu_{li}")[0]
    sv.update(y_raw=y_raw, yg=yg, z=z, y_ssm=y_ssm)
    cqn, ckvn = _rowwise(lambda cq, ckv, gq, gkv: (_rms(cq, gq, Q_LORA), _rms(ckv, gkv, KV_LORA)),
                         [(proj, Q_LORA, P_CQ // Q_LORA), (proj, KV_LORA, P_CKV // KV_LORA)], [p["q_g"], p["kv_g"]],
                         [(Q_LORA, BF16), (KV_LORA, BF16)], name=f"lora_norm_{li}")
    q_raw = _mm_nn(cqn, p["w_q_p"], name=f"q_up_{li}")
    kv_raw = _mm_nn(ckvn, p["w_kv_p"], name=f"kv_up_{li}")
    q, k, v = _head_prep_fwd(q_raw, kv_raw, proj, tabs, p, li)
    o, lse, *gathered = _attn_fwd(q, k, v, f"attn_fwd_{li}", exchange)
    sv.update(cqn=cqn, ckvn=ckvn, q_raw=q_raw, kv_raw=kv_raw, q=q, k=k, v=v, o=o, lse=lse)
    t_ssm = _mm_nn(y_ssm, p["w_out_ssm"], name=f"out_ssm_{li}")
    t_mla = _mm_nn(o, p["w_out_mla_p"], name=f"out_mla_{li}")
    merged = _rowwise(lambda g0, g1, ts, tmv, b: _sigmoid(g0 + b[0:1]) * ts + _sigmoid(g1 + b[1:2]) * tmv,
                      [(proj, D_MODEL, 0), (proj, D_MODEL, 1), (t_ssm, D_MODEL, 0), (t_mla, D_MODEL, 0)],
                      [p["b_gate"]], [(D_MODEL, BF16)], name=f"merge_{li}")[0]
    x1 = _mm_nn(merged, p["w_o"], add=x, name=f"o_proj_{li}")
    sv.update(t_ssm=t_ssm, t_mla=t_mla, merged=merged, x1=x1)
    h2 = _rowwise(lambda xv, g: _rms(xv, g, D_MODEL), [(x1, D_MODEL, 0)], [p["ffn_g"]], [(D_MODEL, BF16)],
                  name=f"ffn_norm_{li}")[0]
    a, r = _mm_nn(h2, p["w_ff1"], epilogue=lambda acc: (acc, jnp.square(jnp.maximum(acc, 0.0))),
                  out_dtypes=(F32, BF16), name=f"ff1_{li}")
    x2 = _mm_nn(r, p["w_ff2"], add=x1, name=f"ff2_{li}")
    sv.update(h2=h2, a=a, r=r)
    return x2, sv, gathered


def _layer_bwd(dx2, dx2_b, sv, p, tabs, li, pending=None, send_early=None):
    g = {}
    da = _mm_nt(dx2_b, p["w_ff2"], extras=(sv["a"],), epilogue=lambda acc, av: acc * (2.0 * jnp.maximum(av, 0.0)),
                out_dtypes=(BF16,), name=f"d_ff2_x_{li}")
    g["w_ff2"] = _mm_tn(sv["r"], dx2_b, name=f"d_ff2_w_{li}")
    dh2 = _mm_nt(da, p["w_ff1"], name=f"d_ff1_x_{li}")
    g["w_ff1"] = _mm_tn(sv["h2"], da, name=f"d_ff1_w_{li}")

    def norm_bwd(xv, dyv, dres, gg):
        dx, dg = _rms_bwd(xv, gg, dyv, D_MODEL)
        return dres + dx, dres + dx, dg

    dx1, dx1_b, dg = _rowwise(norm_bwd, [(sv["x1"], D_MODEL, 0), (dh2, D_MODEL, 0), (dx2, D_MODEL, 0)], [p["ffn_g"]],
                              [(D_MODEL, F32), (D_MODEL, BF16)], [(D_MODEL, F32)], name=f"d_ffn_norm_{li}")
    g["ffn_norm_g"] = dg.reshape(D_MODEL)
    dmerged = _mm_nt(dx1_b, p["w_o"], name=f"d_o_x_{li}")
    g["w_o"] = _mm_tn(sv["merged"], dx1_b, name=f"d_o_w_{li}")

    def merge_bwd(dm, g0, g1, ts, tmv, b):
        s0, s1 = _sigmoid(g0 + b[0:1]), _sigmoid(g1 + b[1:2])
        dg0, dg1 = dm * ts * s0 * (1.0 - s0), dm * tmv * s1 * (1.0 - s1)
        return (dm * s0, dm * s1, jnp.concatenate([dg0, dg1], axis=1),
                jnp.sum(dg0, axis=0, keepdims=True), jnp.sum(dg1, axis=0, keepdims=True))

    proj = sv["proj"]
    dt_ssm, dt_mla, dgate, db0, db1 = _rowwise(
        merge_bwd, [(dmerged, D_MODEL, 0), (proj, D_MODEL, 0), (proj, D_MODEL, 1), (sv["t_ssm"], D_MODEL, 0),
                    (sv["t_mla"], D_MODEL, 0)], [p["b_gate"]],
        [(D_MODEL, BF16), (D_MODEL, BF16), (2 * D_MODEL, BF16)], [(D_MODEL, F32), (D_MODEL, F32)], tm=256,
        name=f"d_merge_{li}")
    g["b_gate"] = jnp.concatenate([db0, db1], axis=0)
    dy_ssm = _mm_nt(dt_ssm, p["w_out_ssm"], name=f"d_out_ssm_x_{li}")
    g["w_out_ssm"] = _mm_tn(sv["y_ssm"], dt_ssm, name=f"d_out_ssm_w_{li}")
    do = _mm_nt(dt_mla, p["w_out_mla_p"], name=f"d_out_mla_x_{li}")
    g["w_out_mla"] = _unpad_out_mla(_mm_tn(sv["o"], dt_mla, name=f"d_out_mla_w_{li}"))

    def glu_bwd(dyv, yr, zv, b):
        yg = _gelu(yr)
        sg = _sigmoid(zv + b)
        dz = dyv * yg * sg * (1.0 - sg)
        return dz, dyv * sg, jnp.sum(dz, axis=0, keepdims=True)

    dz, dyg_direct, dbglu = _rowwise(glu_bwd, [(dy_ssm, SSM_WIDTH, 0), (sv["y_raw"], SSM_WIDTH, 0), (sv["z"], SSM_WIDTH, 0)],
                                     [p["b_glu"]], [(SSM_WIDTH, BF16), (SSM_WIDTH, F32)], [(SSM_WIDTH, F32)],
                                     name=f"d_glu_{li}")
    g["b_glu"] = dbglu.reshape(SSM_WIDTH)
    dyg_mm = _mm_nt(dz, p["w_glu"], name=f"d_glu_x_{li}")
    g["w_glu"] = _mm_tn(sv["yg"], dz, name=f"d_glu_w_{li}")

    def gelu_bwd(d1, d2, yr, u, d):
        dyr = (d1 + d2) * _gelu_grad(yr)
        return dyr, dyr * d, jnp.sum(dyr * u, axis=0, keepdims=True)

    dy_raw, du_d, dd = _rowwise(gelu_bwd, [(dyg_direct, SSM_WIDTH, 0), (dyg_mm, SSM_WIDTH, 0), (sv["y_raw"], SSM_WIDTH, 0),
                                           (proj, SSM_WIDTH, P_U // SSM_WIDTH)], [p["d"]],
                                [(SSM_WIDTH, BF16), (SSM_WIDTH, F32)], [(SSM_WIDTH, F32)], name=f"d_gelu_{li}")
    g["ssm_d"] = dd.reshape(SSM_GROUPS, SSM_GROUP)
    du_parts, dc_sum = [du_d], None
    zoh_grads = []
    for dr_i in range(2):
        xr, xi = sv["states"][dr_i]
        du_i, da_i, db_i, dc_i = _s5_scan_bwd(dy_raw, proj, xr, xi, p["b_blk"][dr_i], p["c_blk"], p["coef_adj"][dr_i],
                                              dr_i == 0, f"s5_bwd_{'fb'[dr_i]}_{li}")
        du_parts.append(du_i)
        dc_sum = dc_i if dc_sum is None else dc_sum + dc_i
        da_i = jnp.sum(da_i, axis=1)
        dar = da_i[:, :CHUNK_STATE].reshape(SSM_GROUPS, SSM_STATE)
        dai = da_i[:, CHUNK_STATE:].reshape(SSM_GROUPS, SSM_STATE)
        dbr, dbi = _b_unblock(db_i)
        zoh_grads.append(p["zoh_vjp"][dr_i]((dar, dai, dbr, dbi)))
    for k_i, nm in enumerate(("ssm_lam_re", "ssm_lam_im", "ssm_log_step", "ssm_b_re", "ssm_b_im")):
        g[nm] = jnp.stack([zoh_grads[0][k_i], zoh_grads[1][k_i]], axis=0)
    g["ssm_c_re"], g["ssm_c_im"] = _c_unblock(dc_sum)
    du = _rowwise(lambda a, b, c: a + b + c, [(d_, SSM_WIDTH, 0) for d_ in du_parts], [], [(SSM_WIDTH, BF16)],
                  name=f"d_u_sum_{li}")[0]
    bufs, flags = pending if pending is not None else ([], [])
    n_pending = len(bufs)
    if send_early is not None:
        early_bufs, early_flags = send_early(g)
        bufs, flags = list(bufs) + list(early_bufs), list(flags) + list(early_flags)
    dq, dk, dv, *got = _attn_bwd(sv["q"], sv["k"], sv["v"], sv["o"], do, sv["lse"], f"attn_bwd_{li}",
                                 bufs or None, tuple(flags))
    dq_raw, dkv_raw, dkr, dgq, dgk = _head_prep_bwd(dq, dk, dv, sv["q_raw"], sv["kv_raw"], proj, tabs, p, li)
    g["q_head_g"] = dgq.reshape(HEAD_PAD)[:QK_HEAD]
    g["k_head_g"] = dgk.reshape(HEAD_PAD)[:QK_HEAD]
    dcqn = _mm_nt(dq_raw, p["w_q_p"], name=f"d_q_up_x_{li}")
    g["w_q_up"] = _unpad_heads_cols(_mm_tn(sv["cqn"], dq_raw, name=f"d_q_up_w_{li}"), QK_HEAD)
    dckvn = _mm_nt(dkv_raw, p["w_kv_p"], name=f"d_kv_up_x_{li}")
    g["w_kv_up"] = _unpad_kv(_mm_tn(sv["ckvn"], dkv_raw, name=f"d_kv_up_w_{li}"))

    def lora_bwd(cq, ckv, d1, d2, gq, gkv):
        dx1_, dg1 = _rms_bwd(cq, gq, d1, Q_LORA)
        dx2_, dg2 = _rms_bwd(ckv, gkv, d2, KV_LORA)
        return dx1_, dx2_, dg1, dg2

    dcq, dckv, dgqn, dgkvn = _rowwise(
        lora_bwd, [(proj, Q_LORA, P_CQ // Q_LORA), (proj, KV_LORA, P_CKV // KV_LORA), (dcqn, Q_LORA, 0), (dckvn, KV_LORA, 0)],
        [p["q_g"], p["kv_g"]], [(Q_LORA, BF16), (KV_LORA, BF16)], [(Q_LORA, F32), (KV_LORA, F32)], name=f"d_lora_norm_{li}")
    g["q_norm_g"], g["kv_norm_g"] = dgqn.reshape(Q_LORA), dgkvn.reshape(KV_LORA)
    gap = jnp.zeros((dx2.shape[0], P_CQ - P_KR - HEAD_PAD), BF16)
    tail = jnp.zeros((dx2.shape[0], P_COLS - P_CQ - Q_LORA), BF16)
    dproj = jnp.concatenate([dgate, du, dckv, dkr, gap, dcq, tail], axis=1)
    dh = _mm_nt(dproj, p["w_in_p"], name=f"d_in_x_{li}")
    g["w_in"] = _unpad_w_in(_mm_tn(sv["h"], dproj, name=f"d_in_w_{li}"))
    dx, dx_b, dg = _rowwise(norm_bwd, [(sv["x"], D_MODEL, 0), (dh, D_MODEL, 0), (dx1, D_MODEL, 0)], [p["mix_g"]],
                            [(D_MODEL, F32), (D_MODEL, BF16)], [(D_MODEL, F32)], name=f"d_mix_norm_{li}")
    g["mix_norm_g"] = dg.reshape(D_MODEL)
    return dx, dx_b, g, got[:n_pending], got[n_pending:]


def _local_step(x, target, layer_weights, send_weights=None, send_early=None, send_late=None):
    tabs = _rope_tables(x.shape[0])
    saved, preps = [], []
    gathered = None
    for li in range(DEPTH):
        p = _prep_layer(layer_weights(li, gathered))
        nxt = send_weights(li + 1) if send_weights is not None and li + 1 < DEPTH else None
        x, sv, gathered = _layer_fwd(x, p, tabs, li, nxt)
        saved.append(sv)
        preps.append(p)

    def loss_fn(y, t):
        err = y - t
        d = err * (1.0 / D_MODEL)
        return d, d, jnp.sum(jnp.sum(err * err, axis=1, keepdims=True), axis=0, keepdims=True) * jnp.ones((1, LANES), F32)

    dx, dx_b, lsum = _rowwise(loss_fn, [(x, D_MODEL, 0), (target, D_MODEL, 0)], [], [(D_MODEL, F32), (D_MODEL, BF16)],
                              [(LANES, F32)], name="loss")
    loss = 0.5 * lsum[0, 0] * (1.0 / D_MODEL)
    grads, early, late = [None] * DEPTH, [None] * DEPTH, [None] * DEPTH
    pending = None
    for li in reversed(range(DEPTH)):
        dx, dx_b, grads[li], got_late, early[li] = _layer_bwd(dx, dx_b, saved[li], preps[li], tabs, li, pending, send_early)
        if pending is not None:
            late[li + 1] = got_late
        pending = send_late(grads[li]) if send_late is not None else None
    return loss, dx, grads, early, late, pending


def _exchange(bufs, scatter, name):
    n = len(bufs)

    def body(*refs):
        copies = _exchange_copies(refs[:n], refs[n:2 * n], *refs[2 * n:], scatter)
        for cp in copies:
            cp.start()
        for cp in copies:
            cp.wait()

    shapes = [tuple(b.shape[1:]) if sc else tuple(b.shape) for b, sc in zip(bufs, scatter)]
    return pl.pallas_call(
        body, out_shape=[jax.ShapeDtypeStruct((N_DEV,) + s, b.dtype) for s, b in zip(shapes, bufs)],
        in_specs=[pl.BlockSpec(memory_space=pl.ANY)] * n, out_specs=[pl.BlockSpec(memory_space=pl.ANY)] * n,
        scratch_shapes=[pltpu.SemaphoreType.DMA((n, N_DEV)), pltpu.SemaphoreType.DMA((n, N_DEV)),
                        pltpu.SemaphoreType.DMA((n,))],
        name=name, compiler_params=pltpu.CompilerParams(has_side_effects=True),
    )(*bufs)


def _adamw(parts, w, m, v, name):
    shape = w.shape
    cols = shape[-1]
    r = math.prod(shape[:-1])
    parts, w, m, v = parts.reshape(N_DEV, r, cols), w.reshape(r, cols), m.reshape(r, cols), v.reshape(r, cols)
    tm = _pick_rows(r, cols)

    def body(p_ref, w_ref, m_ref, v_ref, g_ref, d_ref, nm_ref, nv_ref):
        g = p_ref[0].astype(F32)
        for j in range(1, N_DEV):
            g = g + p_ref[j].astype(F32)
        m_new = ADAM_B1 * m_ref[...] + (1.0 - ADAM_B1) * g
        v_new = ADAM_B2 * v_ref[...] + (1.0 - ADAM_B2) * (g * g)
        m_hat = m_new / (1.0 - ADAM_B1 ** ADAM_STEP)
        v_hat = v_new / (1.0 - ADAM_B2 ** ADAM_STEP)
        g_ref[...] = g
        d_ref[...] = -ADAM_LR * (m_hat / (jnp.sqrt(v_hat) + ADAM_EPS) + ADAM_WD * w_ref[...])
        nm_ref[...] = m_new
        nv_ref[...] = v_new

    spec = pl.BlockSpec((tm, cols), lambda i: (i, 0))
    res = pl.pallas_call(
        body, grid=(r // tm,), in_specs=[pl.BlockSpec((N_DEV, tm, cols), lambda i: (0, i, 0)), spec, spec, spec],
        out_specs=[spec] * 4, out_shape=[jax.ShapeDtypeStruct((r, cols), F32)] * 4, name=name,
        compiler_params=_cparams(("parallel",)),
    )(parts, w, m, v)
    return [a.reshape(shape) for a in res]


def _pick_rows(r, cols):
    for t in (512, 256, 128, 64, 32, 16):
        if r % t == 0 and t * cols <= 512 * 512:
            return t
    return r


def _pack(arrs, dtype, row_mult):
    flat = jnp.concatenate([a.reshape(-1).astype(dtype) for a in arrs])
    n = flat.shape[0]
    per = row_mult * D_MODEL
    total = -(-n // per) * per
    return jnp.pad(flat, (0, total - n)).reshape(total // D_MODEL, D_MODEL)


def _unpack(flat, shapes):
    lead = flat.shape[:-2]
    flat = flat.reshape(lead + (-1,))
    out, off = [], 0
    for shp in shapes:
        n = math.prod(shp)
        out.append(flat[..., off:off + n].reshape(lead + tuple(shp)))
        off += n
    return out


def _to_shards(gfull, axis):
    shp = gfull.shape
    gfull = gfull.reshape(shp[:axis] + (N_DEV, shp[axis] // N_DEV) + shp[axis + 1:])
    return jnp.moveaxis(gfull, axis, 0)


def _from_shards(parts, axis):
    parts = jnp.moveaxis(parts, 0, axis)
    shp = parts.shape
    return parts.reshape(shp[:axis] + (shp[axis] * shp[axis + 1],) + shp[axis + 2:])


def kernel(x, mix_norm_g, w_in, b_gate, ssm_lam_re, ssm_lam_im, ssm_log_step, ssm_b_re, ssm_b_im, ssm_c_re, ssm_c_im, ssm_d, w_glu, b_glu, w_out_ssm, q_norm_g, kv_norm_g, w_q_up, w_kv_up, q_head_g, k_head_g, w_out_mla, w_o, ffn_norm_g, w_ff1, w_ff2, loss_target, m_mix_norm_g, m_w_in, m_b_gate, m_ssm_lam_re, m_ssm_lam_im, m_ssm_log_step, m_ssm_b_re, m_ssm_b_im, m_ssm_c_re, m_ssm_c_im, m_ssm_d, m_w_glu, m_b_glu, m_w_out_ssm, m_q_norm_g, m_kv_norm_g, m_w_q_up, m_w_kv_up, m_q_head_g, m_k_head_g, m_w_out_mla, m_w_o, m_ffn_norm_g, m_w_ff1, m_w_ff2, v_mix_norm_g, v_w_in, v_b_gate, v_ssm_lam_re, v_ssm_lam_im, v_ssm_log_step, v_ssm_b_re, v_ssm_b_im, v_ssm_c_re, v_ssm_c_im, v_ssm_d, v_w_glu, v_b_glu, v_w_out_ssm, v_q_norm_g, v_kv_norm_g, v_w_q_up, v_w_kv_up, v_q_head_g, v_k_head_g, v_w_out_mla, v_w_o, v_ffn_norm_g, v_w_ff1, v_w_ff2):
    args = dict(locals())
    w = {n: args[n] for n in WEIGHTS}
    m = {n: args["m_" + n] for n in WEIGHTS}
    v = {n: args["v_" + n] for n in WEIGHTS}

    def send_weights(li):
        return [w[n][li] if n == "b_gate" else w[n][li].astype(BF16) for n in SHARDED]

    first = _exchange(send_weights(0), [False] * len(SHARDED), "weight_all_gather_0")

    def layer_weights(li, gathered):
        full = {n: _from_shards(pt, SHARD_AXIS[n] - 1) for n, pt in zip(SHARDED, first if li == 0 else gathered)}
        for n in REPLICATED:
            full[n] = w[n][li]
        return full

    def shards(g, names):
        return [_to_shards(g[n], SHARD_AXIS[n] - 1).astype(BF16) for n in names]

    def send_early(g):
        return shards(g, EARLY), [True] * len(EARLY)

    def send_late(g):
        return shards(g, LATE) + [_pack([g[n] for n in REPLICATED], F32, 8)], [True] * len(LATE) + [False]

    loss_part, dx, grads, early, late, pending = _local_step(x[0], loss_target[0], layer_weights, send_weights,
                                                             send_early, send_late)
    late[0] = _exchange(*pending, "grad_exchange_0")
    loss = lax.psum(loss_part, ("x", "y", "c"))

    outs = {}
    for n in SHARDED:
        src, t = (early, EARLY.index(n)) if n in EARLY else (late, LATE.index(n))
        parts = jnp.stack([src[li][t] for li in range(DEPTH)], axis=1)
        for kind, a in zip(("grad", "delta", "new_m", "new_v"), _adamw(parts, w[n], m[n], v[n], "adamw_" + n)):
            outs[kind + "_" + n] = a
    r_parts = jnp.stack([late[li][len(LATE)] for li in range(DEPTH)], axis=1)

    def pack_layers(d):
        return jnp.stack([_pack([d[n][li] for n in REPLICATED], F32, 8) for li in range(DEPTH)], axis=0)

    res = _adamw(r_parts, pack_layers(w), pack_layers(m), pack_layers(v), "adamw_replicated")
    rep_shapes = [w[n].shape[1:] for n in REPLICATED]
    for kind, flat in zip(("grad", "delta", "new_m", "new_v"), res):
        for n, a in zip(REPLICATED, _unpack(flat, rep_shapes)):
            outs[kind + "_" + n] = a
    return (loss, dx[None], *[outs[k + "_" + n] for k in ("grad", "delta", "new_m", "new_v") for n in WEIGHTS])
```

```python
import functools
import math

import jax
import jax.numpy as jnp
from jax import lax
from jax.experimental import pallas as pl
from jax.experimental.pallas import tpu as pltpu

F32 = jnp.float32
BF16 = jnp.bfloat16
_MXU = jnp.bfloat16

D_MODEL = 1024
DEPTH = 4
SSM_WIDTH = 512
SSM_GROUP = 16
SSM_GROUPS = 32
SSM_STATE = 64
MLA_HEADS = 8
QK_NOPE = 64
QK_ROPE = 32
QK_HEAD = 96
V_HEAD = 64
Q_LORA = 384
KV_LORA = 256
ROPE_THETA = 10000.0
D_FF = 4096
EPS = 1e-6
HEAD_PAD = 128
N_DEV = 8
LANES = 128
SUBLANES = 8
CHUNK_GROUPS = 8
N_CHUNKS = SSM_GROUPS // CHUNK_GROUPS
CHUNK_STATE = CHUNK_GROUPS * SSM_STATE
S5_T_BLK = 2048

P_GATE, P_U, P_CKV, P_KR, P_CQ = 0, 2048, 2560, 2816, 3072
P_COLS = 3584
IN_U, IN_CQ, IN_CKV, IN_KR, IN_GATE = 0, 512, 896, 1152, 1184
IN_COLS = 3232

ADAM_LR = 0.001
ADAM_B1 = 0.9
ADAM_B2 = 0.999
ADAM_EPS = 1e-08
ADAM_WD = 0.01
ADAM_STEP = 10

VMEM_LIMIT = 56 * 1024 * 1024
TN_OUT_BLOCK_BYTES = 8 * 1024 * 1024
MXU_WIDTH = 256
WIDE_BLOCK_MAX_K = 1024

SHARDED = ("w_in", "b_gate", "w_glu", "w_out_ssm", "w_q_up", "w_kv_up", "w_out_mla", "w_o", "w_ff1", "w_ff2")
SHARD_AXIS = {"w_in": 2, "b_gate": 2, "w_glu": 1, "w_out_ssm": 2, "w_q_up": 2, "w_kv_up": 2, "w_out_mla": 2,
              "w_o": 1, "w_ff1": 2, "w_ff2": 1}
EARLY = ("b_gate", "w_glu", "w_out_ssm", "w_out_mla", "w_o", "w_ff1", "w_ff2")
LATE = ("w_in", "w_q_up", "w_kv_up")
REPLICATED = ("mix_norm_g", "ssm_lam_re", "ssm_lam_im", "ssm_log_step", "ssm_b_re", "ssm_b_im", "ssm_c_re",
              "ssm_c_im", "ssm_d", "b_glu", "q_norm_g", "kv_norm_g", "q_head_g", "k_head_g", "ffn_norm_g")
WEIGHTS = ("mix_norm_g", "w_in", "b_gate", "ssm_lam_re", "ssm_lam_im", "ssm_log_step", "ssm_b_re", "ssm_b_im",
           "ssm_c_re", "ssm_c_im", "ssm_d", "w_glu", "b_glu", "w_out_ssm", "q_norm_g", "kv_norm_g", "w_q_up",
           "w_kv_up", "q_head_g", "k_head_g", "w_out_mla", "w_o", "ffn_norm_g", "w_ff1", "w_ff2")


def _cparams(sem):
    return pltpu.CompilerParams(dimension_semantics=sem, vmem_limit_bytes=VMEM_LIMIT)


def _dot(a, b, dims):
    return lax.dot_general(a.astype(_MXU), b.astype(_MXU), (dims, ((), ())), preferred_element_type=F32)


def _dot_nn(a, b):
    return _dot(a, b, ((1,), (0,)))


def _dot_nt(a, b):
    return _dot(a, b, ((1,), (1,)))


def _dot_tn(a, b):
    return _dot(a, b, ((0,), (0,)))


def _rowwise(fn, rows, consts, outs, accs=(), *, tm=512, name):
    n_rows = rows[0][0].shape[0]
    tm = min(tm, n_rows)
    n_in = len(rows) + len(consts)
    n_o, n_a = len(outs), len(accs)

    def body(*refs):
        res = fn(*[r[...] for r in refs[:n_in]])
        if not isinstance(res, (tuple, list)):
            res = (res,)
        orefs = refs[n_in:]
        for k in range(n_o):
            orefs[k][...] = res[k].astype(orefs[k].dtype)
        if n_a:
            @pl.when(pl.program_id(0) == 0)
            def _():
                for k in range(n_a):
                    orefs[n_o + k][...] = jnp.zeros_like(orefs[n_o + k])
            for k in range(n_a):
                orefs[n_o + k][...] += res[n_o + k]

    in_specs = [pl.BlockSpec((tm, w), functools.partial(lambda i, j: (i, j), j=j)) for (_, w, j) in rows]
    in_specs += [pl.BlockSpec(c.shape, functools.partial(lambda i, nd: (0,) * nd, nd=c.ndim)) for c in consts]
    out_specs = [pl.BlockSpec((tm, w), lambda i: (i, 0)) for (w, _) in outs]
    out_specs += [pl.BlockSpec((1, w), lambda i: (0, 0)) for (w, _) in accs]
    out_shape = [jax.ShapeDtypeStruct((n_rows, w), dt) for (w, dt) in outs]
    out_shape += [jax.ShapeDtypeStruct((1, w), dt) for (w, dt) in accs]
    res = pl.pallas_call(
        body, grid=(n_rows // tm,), in_specs=in_specs, out_specs=out_specs, out_shape=out_shape, name=name,
        compiler_params=_cparams(("arbitrary",) if n_a else ("parallel",)),
    )(*[r[0] for r in rows], *consts)
    return res


def _pick(n, cap):
    if n <= cap:
        return n
    for unit in (MXU_WIDTH, LANES):
        best = 0
        for t in range(unit, cap + 1, unit):
            if n % t == 0:
                best = t
        if best:
            return best
    return n


def _mm(a, b, transpose_b, extras, epilogue, out_dtypes, name):
    m, k = a.shape
    n = b.shape[0] if transpose_b else b.shape[1]
    tm, tn = min(512, m), _pick(n, 2048 if k <= WIDE_BLOCK_MAX_K else 1024)
    n_in = 2 + len(extras)

    def body(*refs):
        acc = (_dot_nt if transpose_b else _dot_nn)(refs[0][...], refs[1][...])
        res = epilogue(acc, *[r[...] for r in refs[2:n_in]]) if epilogue is not None else acc
        if not isinstance(res, (tuple, list)):
            res = (res,)
        for o_ref, val in zip(refs[n_in:], res):
            o_ref[...] = val.astype(o_ref.dtype)

    blk = pl.BlockSpec((tm, tn), lambda j, i: (i, j))
    b_spec = pl.BlockSpec((tn, k), lambda j, i: (j, 0)) if transpose_b else pl.BlockSpec((k, tn), lambda j, i: (0, j))
    res = pl.pallas_call(
        body, grid=(n // tn, m // tm),
        in_specs=[pl.BlockSpec((tm, k), lambda j, i: (i, 0)), b_spec] + [blk] * len(extras),
        out_specs=[blk] * len(out_dtypes), out_shape=[jax.ShapeDtypeStruct((m, n), dt) for dt in out_dtypes],
        name=name, compiler_params=_cparams(("parallel", "parallel")),
    )(a, b, *extras)
    return res[0] if len(out_dtypes) == 1 else res


def _mm_nn(a, b, *, add=None, extras=(), epilogue=None, out_dtypes=(F32,), name):
    if add is not None:
        extras, epilogue = (add,), (lambda acc, r: acc + r)
    return _mm(a, b, False, tuple(extras), epilogue, out_dtypes, name)


def _mm_nt(a, b, *, extras=(), epilogue=None, out_dtypes=(F32,), name):
    return _mm(a, b, True, tuple(extras), epilogue, out_dtypes, name)


def _mm_tn(a, b, *, a_cols=None, name):
    s = a.shape[0]
    n = b.shape[1]
    mw, mj = (a.shape[1], 0) if a_cols is None else a_cols
    ts = min(512, s)
    tm = _pick(mw, 1024)
    tn = _pick(n, max(1024, TN_OUT_BLOCK_BYTES // (4 * tm)))
    n_mb = mw // tm

    def body(a_ref, b_ref, o_ref):
        @pl.when(pl.program_id(2) == 0)
        def _():
            o_ref[...] = jnp.zeros_like(o_ref)
        o_ref[...] += _dot_tn(a_ref[...], b_ref[...])

    return pl.pallas_call(
        body, grid=(n_mb, n // tn, s // ts),
        in_specs=[pl.BlockSpec((ts, tm), lambda i, j, t: (t, mj * n_mb + i)), pl.BlockSpec((ts, tn), lambda i, j, t: (t, j))],
        out_specs=pl.BlockSpec((tm, tn), lambda i, j, t: (i, j)),
        out_shape=jax.ShapeDtypeStruct((mw, n), F32), name=name,
        compiler_params=_cparams(("parallel", "parallel", "arbitrary")),
    )(a, b)


def _rms(x, g, n):
    r = lax.rsqrt(jnp.sum(x * x, axis=-1, keepdims=True) * (1.0 / n) + EPS)
    return x * r * g


def _rms_bwd(x, g, dy, n):
    r = lax.rsqrt(jnp.sum(x * x, axis=-1, keepdims=True) * (1.0 / n) + EPS)
    xr = x * r
    dyg = dy * g
    dx = r * dyg - xr * (r * r) * (jnp.sum(dyg * x, axis=-1, keepdims=True) * (1.0 / n))
    return dx, jnp.sum(dy * xr, axis=0, keepdims=True)


def _gelu(x):
    c = math.sqrt(2.0 / math.pi)
    return 0.5 * x * (1.0 + jnp.tanh(c * (x + 0.044715 * (x * x * x))))


def _gelu_grad(x):
    c = math.sqrt(2.0 / math.pi)
    t = jnp.tanh(c * (x + 0.044715 * (x * x * x)))
    return 0.5 * (1.0 + t) + 0.5 * x * (1.0 - t * t) * (c * (1.0 + 3.0 * 0.044715 * (x * x)))


def _sigmoid(x):
    return 1.0 / (1.0 + jnp.exp(-x))


def _rope(x, cf, sa, sb):
    return x * cf + pltpu.roll(x, HEAD_PAD - QK_ROPE // 2, 1) * sa + pltpu.roll(x, QK_ROPE // 2, 1) * sb


def _rope_t(d, cf, sa, sb):
    return d * cf + pltpu.roll(d * sa, QK_ROPE // 2, 1) + pltpu.roll(d * sb, HEAD_PAD - QK_ROPE // 2, 1)


def _scan_tables(ar, ai, reverse):
    ar = ar.reshape(N_CHUNKS, CHUNK_STATE)
    ai = ai.reshape(N_CHUNKS, CHUNK_STATE)
    pr, pi = [ar], [ai]
    for _ in range(SUBLANES - 1):
        pr, pi = pr + [pr[-1] * ar - pi[-1] * ai], pi + [pr[-1] * ai + pi[-1] * ar]
    row = jnp.arange(SUBLANES)[None, :, None]
    tiles = []
    for k in (1, 2, 4):
        mask = (row <= SUBLANES - 1 - k) if reverse else (row >= k)
        tiles.append(jnp.where(mask, pr[k - 1][:, None, :], 0.0))
        tiles.append(jnp.where(mask, pi[k - 1][:, None, :], 0.0))
    order = list(range(SUBLANES))[::-1] if reverse else list(range(SUBLANES))
    tiles.append(jnp.stack([pr[j] for j in order], axis=1))
    tiles.append(jnp.stack([pi[j] for j in order], axis=1))
    return jnp.stack(tiles, axis=1).astype(F32)


def _slab_scan(xr, xi, coef, carry_r, carry_i, reverse):
    for idx, k in enumerate((1, 2, 4)):
        sh = SUBLANES - k if reverse else k
        sr, si = pltpu.roll(xr, sh, 0), pltpu.roll(xi, sh, 0)
        cr, ci = coef[2 * idx], coef[2 * idx + 1]
        xr, xi = xr + cr * sr - ci * si, xi + cr * si + ci * sr
    pr, pi = coef[6], coef[7]
    xr = xr + pr * carry_r - pi * carry_i
    xi = xi + pr * carry_i + pi * carry_r
    return xr, xi


def _s5_scan_fwd(proj, b_blk, c_blk, coef, reverse, name):
    s = proj.shape[0]
    t_blk = min(S5_T_BLK, s)
    n_t = s // t_blk
    n_slab = t_blk // SUBLANES
    last = 0 if reverse else SUBLANES - 1

    def tmap(t):
        return n_t - 1 - t if reverse else t

    def body(u_ref, b_ref, c_ref, coef_ref, y_ref, xr_ref, xi_ref, carry_ref):
        @pl.when(pl.program_id(1) == 0)
        def _():
            carry_ref[...] = jnp.zeros_like(carry_ref)
        bu = _dot_nn(u_ref[...], b_ref[0])
        xr_ref[...] = bu[:, :CHUNK_STATE]
        xi_ref[...] = bu[:, CHUNK_STATE:]
        coef_v = [coef_ref[0, k] for k in range(8)]

        def slab(i, carry):
            sl = (n_slab - 1 - i) if reverse else i
            rows = pl.ds(pl.multiple_of(sl * SUBLANES, SUBLANES), SUBLANES)
            xr, xi = _slab_scan(xr_ref[rows, :], xi_ref[rows, :], coef_v, carry[0], carry[1], reverse)
            xr_ref[rows, :] = xr
            xi_ref[rows, :] = xi
            return (jnp.broadcast_to(xr[last:last + 1, :], xr.shape), jnp.broadcast_to(xi[last:last + 1, :], xi.shape))

        cr, ci = lax.fori_loop(0, n_slab, slab, (carry_ref[0], carry_ref[1]))
        carry_ref[0] = cr
        carry_ref[1] = ci
        y_ref[...] = _dot_nn(xr_ref[...], c_ref[0, :CHUNK_STATE, :]) + _dot_nn(xi_ref[...], c_ref[0, CHUNK_STATE:, :])

    u_blk0 = P_U // LANES
    return pl.pallas_call(
        body, grid=(N_CHUNKS, n_t),
        in_specs=[pl.BlockSpec((t_blk, LANES), lambda c, t: (tmap(t), u_blk0 + c)),
                  pl.BlockSpec((1, LANES, 2 * CHUNK_STATE), lambda c, t: (c, 0, 0)),
                  pl.BlockSpec((1, 2 * CHUNK_STATE, LANES), lambda c, t: (c, 0, 0)),
                  pl.BlockSpec((1, 8, SUBLANES, CHUNK_STATE), lambda c, t: (c, 0, 0, 0))],
        out_specs=[pl.BlockSpec((t_blk, LANES), lambda c, t: (tmap(t), c)),
                   pl.BlockSpec((t_blk, CHUNK_STATE), lambda c, t: (tmap(t), c)),
                   pl.BlockSpec((t_blk, CHUNK_STATE), lambda c, t: (tmap(t), c))],
        out_shape=[jax.ShapeDtypeStruct((s, SSM_WIDTH), F32),
                   jax.ShapeDtypeStruct((s, N_CHUNKS * CHUNK_STATE), F32),
                   jax.ShapeDtypeStruct((s, N_CHUNKS * CHUNK_STATE), F32)],
        scratch_shapes=[pltpu.VMEM((2, SUBLANES, CHUNK_STATE), F32)],
        name=name, compiler_params=_cparams(("parallel", "arbitrary")),
    )(proj, b_blk, c_blk, coef)


def _s5_scan_bwd(dy, proj, x_re, x_im, b_blk, c_blk, coef, reverse, name):
    s = dy.shape[0]
    t_blk = min(S5_T_BLK, s)
    n_t = s // t_blk
    n_slab = t_blk // SUBLANES
    last = 0 if reverse else SUBLANES - 1
    first = SUBLANES - 1 if reverse else 0

    def tmap(t):
        return n_t - 1 - t if reverse else t

    def body(dy_ref, u_ref, xr_ref, xi_ref, b_ref, c_ref, coef_ref, du_ref, da_ref, db_ref, dc_ref,
             carry_ref, lr_ref, li_ref):
        @pl.when(pl.program_id(1) == 0)
        def _():
            carry_ref[...] = jnp.zeros_like(carry_ref)
            da_ref[...] = jnp.zeros_like(da_ref)
            db_ref[...] = jnp.zeros_like(db_ref)
            dc_ref[...] = jnp.zeros_like(dc_ref)
        g = _dot_nt(dy_ref[...], c_ref[0])
        lr_ref[...] = g[:, :CHUNK_STATE]
        li_ref[...] = g[:, CHUNK_STATE:]
        coef_v = [coef_ref[0, k] for k in range(8)]
        row = lax.broadcasted_iota(jnp.int32, (SUBLANES, CHUNK_STATE), 0)
        sh_prev = SUBLANES - 1 if reverse else 1

        def slab(i, carry):
            cr, ci, ar_acc, ai_acc = carry
            sl = (n_slab - 1 - i) if reverse else i
            rows = pl.ds(pl.multiple_of(sl * SUBLANES, SUBLANES), SUBLANES)
            lr, li = _slab_scan(lr_ref[rows, :], li_ref[rows, :], coef_v, cr, ci, reverse)
            lr_ref[rows, :] = lr
            li_ref[rows, :] = li
            pr = jnp.where(row == first, cr, pltpu.roll(lr, sh_prev, 0))
            pi = jnp.where(row == first, ci, pltpu.roll(li, sh_prev, 0))
            xr, xi = xr_ref[rows, :], xi_ref[rows, :]
            ar_acc = ar_acc + xr * pr + xi * pi
            ai_acc = ai_acc + xr * pi - xi * pr
            return (jnp.broadcast_to(lr[last:last + 1, :], lr.shape), jnp.broadcast_to(li[last:last + 1, :], li.shape),
                    ar_acc, ai_acc)

        zero = jnp.zeros((SUBLANES, CHUNK_STATE), F32)
        cr, ci, ar_acc, ai_acc = lax.fori_loop(0, n_slab, slab, (carry_ref[0], carry_ref[1], zero, zero))
        carry_ref[0] = cr
        carry_ref[1] = ci
        da_ref[0, :, :CHUNK_STATE] += ar_acc
        da_ref[0, :, CHUNK_STATE:] += ai_acc
        lam_r, lam_i = lr_ref[...], li_ref[...]
        u = u_ref[...]
        du_ref[...] = _dot_nt(lam_r, b_ref[0, :, :CHUNK_STATE]) + _dot_nt(lam_i, b_ref[0, :, CHUNK_STATE:])
        db_ref[0, :, :CHUNK_STATE] += _dot_tn(u, lam_r)
        db_ref[0, :, CHUNK_STATE:] += _dot_tn(u, lam_i)
        dyv = dy_ref[...]
        dc_ref[0, :CHUNK_STATE, :] += _dot_tn(xr_ref[...], dyv)
        dc_ref[0, CHUNK_STATE:, :] += _dot_tn(xi_ref[...], dyv)

    u_blk0 = P_U // LANES
    return pl.pallas_call(
        body, grid=(N_CHUNKS, n_t),
        in_specs=[pl.BlockSpec((t_blk, LANES), lambda c, t: (tmap(t), c)),
                  pl.BlockSpec((t_blk, LANES), lambda c, t: (tmap(t), u_blk0 + c)),
                  pl.BlockSpec((t_blk, CHUNK_STATE), lambda c, t: (tmap(t), c)),
                  pl.BlockSpec((t_blk, CHUNK_STATE), lambda c, t: (tmap(t), c)),
                  pl.BlockSpec((1, LANES, 2 * CHUNK_STATE), lambda c, t: (c, 0, 0)),
                  pl.BlockSpec((1, 2 * CHUNK_STATE, LANES), lambda c, t: (c, 0, 0)),
                  pl.BlockSpec((1, 8, SUBLANES, CHUNK_STATE), lambda c, t: (c, 0, 0, 0))],
        out_specs=[pl.BlockSpec((t_blk, LANES), lambda c, t: (tmap(t), c)),
                   pl.BlockSpec((1, SUBLANES, 2 * CHUNK_STATE), lambda c, t: (c, 0, 0)),
                   pl.BlockSpec((1, LANES, 2 * CHUNK_STATE), lambda c, t: (c, 0, 0)),
                   pl.BlockSpec((1, 2 * CHUNK_STATE, LANES), lambda c, t: (c, 0, 0))],
        out_shape=[jax.ShapeDtypeStruct((s, SSM_WIDTH), F32),
                   jax.ShapeDtypeStruct((N_CHUNKS, SUBLANES, 2 * CHUNK_STATE), F32),
                   jax.ShapeDtypeStruct((N_CHUNKS, LANES, 2 * CHUNK_STATE), F32),
                   jax.ShapeDtypeStruct((N_CHUNKS, 2 * CHUNK_STATE, LANES), F32)],
        scratch_shapes=[pltpu.VMEM((2, SUBLANES, CHUNK_STATE), F32), pltpu.VMEM((t_blk, CHUNK_STATE), F32),
                        pltpu.VMEM((t_blk, CHUNK_STATE), F32)],
        name=name, compiler_params=_cparams(("parallel", "arbitrary")),
    )(dy, proj, x_re, x_im, b_blk, c_blk, coef)


def _zoh(lam_re, lam_im, log_step, b_re, b_im):
    step = jnp.exp(log_step)[:, None]
    mag = jnp.exp(lam_re * step)
    abar_r = mag * jnp.cos(lam_im * step)
    abar_i = mag * jnp.sin(lam_im * step)
    nr = abar_r - 1.0
    ni = abar_i
    den = lam_re * lam_re + lam_im * lam_im
    fr = (nr * lam_re + ni * lam_im) / den
    fi = (ni * lam_re - nr * lam_im) / den
    bbar_r = fr[..., None] * b_re - fi[..., None] * b_im
    bbar_i = fr[..., None] * b_im + fi[..., None] * b_re
    return abar_r, abar_i, bbar_r, bbar_i


def _b_block(bbar_r, bbar_i):
    eye = jnp.eye(CHUNK_GROUPS, dtype=F32)

    def one(b):
        b = b.reshape(N_CHUNKS, CHUNK_GROUPS, SSM_STATE, SSM_GROUP)
        return jnp.einsum("cgnp,gh->cgphn", b, eye).reshape(N_CHUNKS, LANES, CHUNK_STATE)

    return jnp.concatenate([one(bbar_r), one(bbar_i)], axis=2)


def _b_unblock(db):
    eye = jnp.eye(CHUNK_GROUPS, dtype=F32)

    def one(d):
        d = d.reshape(N_CHUNKS, CHUNK_GROUPS, SSM_GROUP, CHUNK_GROUPS, SSM_STATE)
        return jnp.einsum("cgphn,gh->cgnp", d, eye).reshape(SSM_GROUPS, SSM_STATE, SSM_GROUP)

    return one(db[:, :, :CHUNK_STATE]), one(db[:, :, CHUNK_STATE:])


def _c_block(c_re, c_im):
    eye = jnp.eye(CHUNK_GROUPS, dtype=F32)

    def one(c):
        c = c.reshape(N_CHUNKS, CHUNK_GROUPS, SSM_GROUP, SSM_STATE)
        return jnp.einsum("cgpn,gh->cgnhp", c, eye).reshape(N_CHUNKS, CHUNK_STATE, LANES)

    return jnp.concatenate([one(c_re), -one(c_im)], axis=1)


def _c_unblock(dc):
    eye = jnp.eye(CHUNK_GROUPS, dtype=F32)

    def one(d):
        d = d.reshape(N_CHUNKS, CHUNK_GROUPS, SSM_STATE, CHUNK_GROUPS, SSM_GROUP)
        return jnp.einsum("cgnhp,gh->cgpn", d, eye).reshape(SSM_GROUPS, SSM_GROUP, SSM_STATE)

    return one(dc[:, :CHUNK_STATE, :]), -one(dc[:, CHUNK_STATE:, :])


def _exchange_copies(ins, outs, send_sems, recv_sems, local_sems, scatter):
    x, y, c = lax.axis_index("x"), lax.axis_index("y"), lax.axis_index("c")
    me = 4 * x + 2 * y + c
    copies = [pltpu.make_async_copy(ins[t].at[me] if scatter[t] else ins[t], outs[t].at[me], local_sems.at[t])
              for t in range(len(ins))]
    for k in range(1, N_DEV):
        peer = (x ^ ((k >> 2) & 1), y ^ ((k >> 1) & 1), c ^ (k & 1))
        peer_idx = 4 * peer[0] + 2 * peer[1] + peer[2]
        for t in range(len(ins)):
            copies.append(pltpu.make_async_remote_copy(
                src_ref=ins[t].at[peer_idx] if scatter[t] else ins[t], dst_ref=outs[t].at[me],
                send_sem=send_sems.at[t, k], recv_sem=recv_sems.at[t, k], device_id=peer,
                device_id_type=pl.DeviceIdType.MESH))
    return copies


def _call_with_exchange(body, *, grid, in_specs, out_specs, out_shape, scratch_shapes, args, semantics, name,
                        exchange=None, scatter=()):
    if exchange is None:
        return pl.pallas_call(body, grid=grid, in_specs=in_specs, out_specs=out_specs, out_shape=out_shape,
                              scratch_shapes=scratch_shapes, name=name, compiler_params=_cparams(semantics))(*args)
    n, n_in, n_out, n_scr = len(exchange), len(in_specs), len(out_specs), len(scratch_shapes)

    def wrapped(*refs):
        ins, refs = refs[:n_in], refs[n_in:]
        c_ins, refs = refs[:n], refs[n:]
        outs, refs = refs[:n_out], refs[n_out:]
        c_outs, refs = refs[:n], refs[n:]
        scr, sems = refs[:n_scr], refs[n_scr:]
        ids = [pl.program_id(a) for a in range(len(grid))]
        first = functools.reduce(jnp.logical_and, [i == 0 for i in ids])
        last = functools.reduce(jnp.logical_and, [i == g - 1 for i, g in zip(ids, grid)])

        @pl.when(first)
        def _():
            for cp in _exchange_copies(c_ins, c_outs, *sems, scatter):
                cp.start()

        body(*ins, *outs, *scr)

        @pl.when(last)
        def _():
            for cp in _exchange_copies(c_ins, c_outs, *sems, scatter):
                cp.wait()

    shapes = [tuple(b.shape[1:]) if sc else tuple(b.shape) for b, sc in zip(exchange, scatter)]
    hbm = pl.BlockSpec(memory_space=pl.ANY)
    res = pl.pallas_call(
        wrapped, grid=grid, in_specs=list(in_specs) + [hbm] * n, out_specs=list(out_specs) + [hbm] * n,
        out_shape=list(out_shape) + [jax.ShapeDtypeStruct((N_DEV,) + s, b.dtype) for s, b in zip(shapes, exchange)],
        scratch_shapes=list(scratch_shapes) + [pltpu.SemaphoreType.DMA((n, N_DEV)), pltpu.SemaphoreType.DMA((n, N_DEV)),
                                              pltpu.SemaphoreType.DMA((n,))],
        name=name, compiler_params=pltpu.CompilerParams(dimension_semantics=("arbitrary",) * len(grid),
                                                        vmem_limit_bytes=VMEM_LIMIT, has_side_effects=True),
    )(*args, *exchange)
    return res


def _attn_fwd(q, k, v, name, exchange=None):
    s = q.shape[0]
    tq = min(2048, s)
    tk = min(1024, s)
    n_k = s // tk

    def body(q_ref, k_ref, v_ref, o_ref, lse_ref, m_ref, acc_ref):
        m_ref[...] = jnp.full_like(m_ref, -jnp.inf)
        acc_ref[...] = jnp.zeros_like(acc_ref)
        qv = q_ref[...]

        def step(j, _):
            rows = pl.ds(pl.multiple_of(j * tk, tk), tk)
            sc = _dot_nt(qv, k_ref[rows, :])
            m_old = m_ref[...]
            m_new = jnp.maximum(m_old, jnp.max(sc, axis=1, keepdims=True))
            p = jnp.exp(sc - m_new)
            alpha = jnp.exp(m_old - m_new)
            acc_ref[...] = alpha * acc_ref[...] + _dot_nn(p, v_ref[rows, :])
            m_ref[...] = m_new
            return 0

        lax.fori_loop(0, n_k, step, 0, unroll=min(8, n_k))
        acc = acc_ref[...]
        l = acc[:, V_HEAD:V_HEAD + 1]
        o_ref[...] = acc / l
        lse = m_ref[...] + jnp.log(l)
        lse_ref[0] = jnp.broadcast_to(lse, (tq, LANES)).T[:SUBLANES, :]

    return _call_with_exchange(
        body, grid=(MLA_HEADS, s // tq),
        in_specs=[pl.BlockSpec((tq, HEAD_PAD), lambda h, i: (i, h)),
                  pl.BlockSpec((s, HEAD_PAD), lambda h, i: (0, h)),
                  pl.BlockSpec((s, HEAD_PAD), lambda h, i: (0, h))],
        out_specs=[pl.BlockSpec((tq, HEAD_PAD), lambda h, i: (i, h)),
                   pl.BlockSpec((1, SUBLANES, tq), lambda h, i: (h, 0, i))],
        out_shape=[jax.ShapeDtypeStruct((s, MLA_HEADS * HEAD_PAD), F32),
                   jax.ShapeDtypeStruct((MLA_HEADS, SUBLANES, s), F32)],
        scratch_shapes=[pltpu.VMEM((tq, 1), F32), pltpu.VMEM((tq, HEAD_PAD), F32)],
        args=(q, k, v), semantics=("parallel", "parallel"), name=name, exchange=exchange,
        scatter=(False,) * len(exchange or ()))


def _attn_bwd(q, k, v, o, do, lse, name, exchange=None, scatter=()):
    s = q.shape[0]
    tq = min(2048, s)
    tk = min(2048, s)

    def body(q_ref, k_ref, v_ref, o_ref, do_ref, lse_ref, dq_ref, dk_ref, dv_ref):
        j, i = pl.program_id(1), pl.program_id(2)

        @pl.when(jnp.logical_and(j == 0, i == 0))
        def _():
            dq_ref[...] = jnp.zeros_like(dq_ref)

        @pl.when(i == 0)
        def _():
            dk_ref[...] = jnp.zeros_like(dk_ref)
            dv_ref[...] = jnp.zeros_like(dv_ref)

        qv, kv, vv, dov = q_ref[...], k_ref[...], v_ref[...], do_ref[...]
        delta_col = jnp.sum(dov * o_ref[...], axis=1, keepdims=True)
        delta = jnp.broadcast_to(delta_col, (tq, LANES)).T[:1, :]
        st = _dot_nt(kv, qv)
        pt = jnp.exp(st - lse_ref[0, :1, :])
        dv_ref[...] += _dot_nn(pt, dov)
        dpt = _dot_nt(vv, dov)
        dst = pt * (dpt - delta)
        dk_ref[...] += _dot_nn(dst, qv)
        rows = pl.ds(pl.multiple_of(i * tq, tq), tq)
        dq_ref[rows, :] += _dot_tn(dst, kv)

    return _call_with_exchange(
        body, grid=(MLA_HEADS, s // tk, s // tq),
        in_specs=[pl.BlockSpec((tq, HEAD_PAD), lambda h, j, i: (i, h)),
                  pl.BlockSpec((tk, HEAD_PAD), lambda h, j, i: (j, h)),
                  pl.BlockSpec((tk, HEAD_PAD), lambda h, j, i: (j, h)),
                  pl.BlockSpec((tq, HEAD_PAD), lambda h, j, i: (i, h)),
                  pl.BlockSpec((tq, HEAD_PAD), lambda h, j, i: (i, h)),
                  pl.BlockSpec((1, SUBLANES, tq), lambda h, j, i: (h, 0, i))],
        out_specs=[pl.BlockSpec((s, HEAD_PAD), lambda h, j, i: (0, h)),
                   pl.BlockSpec((tk, HEAD_PAD), lambda h, j, i: (j, h)),
                   pl.BlockSpec((tk, HEAD_PAD), lambda h, j, i: (j, h))],
        out_shape=[jax.ShapeDtypeStruct((s, MLA_HEADS * HEAD_PAD), F32)] * 3, scratch_shapes=[],
        args=(q, k, v, o, do, lse), semantics=("parallel", "arbitrary", "arbitrary"), name=name,
        exchange=exchange, scatter=scatter)


def _pad_w_in(w):
    z = functools.partial(jnp.zeros, dtype=w.dtype)
    return jnp.concatenate([
        w[:, IN_GATE:IN_COLS], w[:, IN_U:IN_CQ], w[:, IN_CKV:IN_KR],
        z((D_MODEL, QK_NOPE)), w[:, IN_KR:IN_GATE], z((D_MODEL, HEAD_PAD - QK_HEAD)),
        z((D_MODEL, P_CQ - P_KR - HEAD_PAD)), w[:, IN_CQ:IN_CKV], z((D_MODEL, P_COLS - P_CQ - Q_LORA))], axis=1)


def _unpad_w_in(d):
    return jnp.concatenate([d[:, P_U:P_CKV], d[:, P_CQ:P_CQ + Q_LORA], d[:, P_CKV:P_KR],
                            d[:, P_KR + QK_NOPE:P_KR + QK_HEAD], d[:, P_GATE:P_U]], axis=1)


def _pad_heads_cols(w, real):
    k = w.shape[0]
    w = w.reshape(k, MLA_HEADS, real)
    return jnp.pad(w, ((0, 0), (0, 0), (0, HEAD_PAD - real))).reshape(k, MLA_HEADS * HEAD_PAD)


def _unpad_heads_cols(d, real):
    k = d.shape[0]
    return d.reshape(k, MLA_HEADS, HEAD_PAD)[:, :, :real].reshape(k, MLA_HEADS * real)


def _pad_kv(w):
    w = w.reshape(KV_LORA, MLA_HEADS, QK_NOPE + V_HEAD)
    kn = jnp.pad(w[:, :, :QK_NOPE], ((0, 0), (0, 0), (0, HEAD_PAD - QK_NOPE)))
    vv = jnp.pad(w[:, :, QK_NOPE:], ((0, 0), (0, 0), (0, HEAD_PAD - V_HEAD)))
    return jnp.concatenate([kn.reshape(KV_LORA, -1), vv.reshape(KV_LORA, -1)], axis=1)


def _unpad_kv(d):
    n = MLA_HEADS * HEAD_PAD
    kn = d[:, :n].reshape(KV_LORA, MLA_HEADS, HEAD_PAD)[:, :, :QK_NOPE]
    vv = d[:, n:].reshape(KV_LORA, MLA_HEADS, HEAD_PAD)[:, :, :V_HEAD]
    return jnp.concatenate([kn, vv], axis=2).reshape(KV_LORA, MLA_HEADS * (QK_NOPE + V_HEAD))


def _pad_out_mla(w):
    w = w.reshape(MLA_HEADS, V_HEAD, D_MODEL)
    return jnp.pad(w, ((0, 0), (0, HEAD_PAD - V_HEAD), (0, 0))).reshape(MLA_HEADS * HEAD_PAD, D_MODEL)


def _unpad_out_mla(d):
    return d.reshape(MLA_HEADS, HEAD_PAD, D_MODEL)[:, :V_HEAD, :].reshape(MLA_HEADS * V_HEAD, D_MODEL)


def _rope_tables(seq):
    half = QK_ROPE // 2
    inv_freq = ROPE_THETA ** (-jnp.arange(half, dtype=F32) / half)
    ang = jnp.arange(seq, dtype=F32)[:, None] * inv_freq[None, :]
    cos, sin = jnp.cos(ang), jnp.sin(ang)
    one, zero = jnp.ones((seq, QK_NOPE), F32), jnp.zeros((seq, half), F32)
    tail1, tail0 = jnp.ones((seq, HEAD_PAD - QK_HEAD), F32), jnp.zeros((seq, HEAD_PAD - QK_HEAD), F32)
    cf = jnp.concatenate([one, cos, cos, tail1], axis=1)
    sa = jnp.concatenate([0.0 * one, -sin, zero, tail0], axis=1)
    sb = jnp.concatenate([0.0 * one, zero, sin, tail0], axis=1)
    return cf, sa, sb


def _prep_layer(w):
    p = {}
    p["w_in_p"] = _pad_w_in(w["w_in"])
    p["w_glu"] = w["w_glu"]
    p["w_out_ssm"] = w["w_out_ssm"]
    p["w_q_p"] = _pad_heads_cols(w["w_q_up"], QK_HEAD)
    p["w_kv_p"] = _pad_kv(w["w_kv_up"])
    p["w_out_mla_p"] = _pad_out_mla(w["w_out_mla"])
    p["w_o"] = w["w_o"]
    p["w_ff1"] = w["w_ff1"]
    p["w_ff2"] = w["w_ff2"]
    p["mix_g"] = w["mix_norm_g"].reshape(1, D_MODEL)
    p["ffn_g"] = w["ffn_norm_g"].reshape(1, D_MODEL)
    p["b_gate"] = w["b_gate"]
    p["b_glu"] = w["b_glu"].reshape(1, SSM_WIDTH)
    p["d"] = w["ssm_d"].reshape(1, SSM_WIDTH)
    p["q_g"] = w["q_norm_g"].reshape(1, Q_LORA)
    p["kv_g"] = w["kv_norm_g"].reshape(1, KV_LORA)
    p["qh_g"] = jnp.pad(w["q_head_g"], (0, HEAD_PAD - QK_HEAD)).reshape(1, HEAD_PAD)
    p["kh_g"] = jnp.pad(w["k_head_g"], (0, HEAD_PAD - QK_HEAD)).reshape(1, HEAD_PAD)
    p["c_blk"] = _c_block(w["ssm_c_re"], w["ssm_c_im"]).astype(BF16)
    zoh, p["zoh_vjp"] = [], []
    for dr in range(2):
        out, vjp = jax.vjp(_zoh, w["ssm_lam_re"][dr], w["ssm_lam_im"][dr], w["ssm_log_step"][dr],
                           w["ssm_b_re"][dr], w["ssm_b_im"][dr])
        zoh.append(out)
        p["zoh_vjp"].append(vjp)
    p["b_blk"] = [_b_block(z[2], z[3]).astype(BF16) for z in zoh]
    p["coef_fwd"] = [_scan_tables(zoh[0][0], zoh[0][1], False), _scan_tables(zoh[1][0], zoh[1][1], True)]
    p["coef_adj"] = [_scan_tables(zoh[0][0], -zoh[0][1], True), _scan_tables(zoh[1][0], -zoh[1][1], False)]
    return p


def _head_prep_fwd(q_raw, kv_raw, proj, tabs, p, li):
    cf, sa, sb = tabs
    scale = QK_HEAD ** -0.5

    def fn(qr, kn, vv, kr, cfv, sav, sbv, gq, gk):
        qo, ko = [], []
        for h in range(MLA_HEADS):
            sl = slice(h * HEAD_PAD, (h + 1) * HEAD_PAD)
            qo.append(_rope(_rms(qr[:, sl], gq, QK_HEAD), cfv, sav, sbv) * scale)
            ko.append(_rope(_rms(kn[:, sl] + kr, gk, QK_HEAD), cfv, sav, sbv))
        lane = lax.broadcasted_iota(jnp.int32, vv.shape, 1)
        return jnp.concatenate(qo, axis=1), jnp.concatenate(ko, axis=1), jnp.where(lane % HEAD_PAD == V_HEAD, 1.0, vv)

    n = MLA_HEADS * HEAD_PAD
    return _rowwise(fn, [(q_raw, n, 0), (kv_raw, n, 0), (kv_raw, n, 1), (proj, HEAD_PAD, P_KR // HEAD_PAD),
                         (cf, HEAD_PAD, 0), (sa, HEAD_PAD, 0), (sb, HEAD_PAD, 0)], [p["qh_g"], p["kh_g"]],
                    [(n, BF16), (n, BF16), (n, BF16)], tm=256, name=f"head_prep_fwd_{li}")


def _head_prep_bwd(dq, dk, dv, q_raw, kv_raw, proj, tabs, p, li):
    cf, sa, sb = tabs
    scale = QK_HEAD ** -0.5

    def fn(dqv, dkv, dvv, qr, kn, kr, cfv, sav, sbv, gq, gk):
        dqo, dko = [], []
        dkr = jnp.zeros_like(kr)
        dgq = jnp.zeros((1, HEAD_PAD), F32)
        dgk = jnp.zeros((1, HEAD_PAD), F32)
        for h in range(MLA_HEADS):
            sl = slice(h * HEAD_PAD, (h + 1) * HEAD_PAD)
            dx, dg = _rms_bwd(qr[:, sl], gq, _rope_t(dqv[:, sl] * scale, cfv, sav, sbv), QK_HEAD)
            dqo.append(dx)
            dgq = dgq + dg
            dx, dg = _rms_bwd(kn[:, sl] + kr, gk, _rope_t(dkv[:, sl], cfv, sav, sbv), QK_HEAD)
            dko.append(dx)
            dkr = dkr + dx
            dgk = dgk + dg
        return jnp.concatenate(dqo, axis=1), jnp.concatenate(dko + [dvv], axis=1), dkr, dgq, dgk

    n = MLA_HEADS * HEAD_PAD
    return _rowwise(fn, [(dq, n, 0), (dk, n, 0), (dv, n, 0), (q_raw, n, 0), (kv_raw, n, 0),
                         (proj, HEAD_PAD, P_KR // HEAD_PAD), (cf, HEAD_PAD, 0), (sa, HEAD_PAD, 0), (sb, HEAD_PAD, 0)],
                    [p["qh_g"], p["kh_g"]], [(n, BF16), (2 * n, BF16), (HEAD_PAD, BF16)],
                    [(HEAD_PAD, F32), (HEAD_PAD, F32)], tm=256, name=f"head_prep_bwd_{li}")


def _layer_fwd(x, p, tabs, li, exchange=None):
    sv = {"x": x}
    h = _rowwise(lambda xv, g: _rms(xv, g, D_MODEL), [(x, D_MODEL, 0)], [p["mix_g"]], [(D_MODEL, BF16)],
                 name=f"mix_norm_{li}")[0]
    proj = _mm_nn(h, p["w_in_p"], name=f"in_proj_{li}")
    sv["h"], sv["proj"] = h, proj
    y_f, xr_f, xi_f = _s5_scan_fwd(proj, p["b_blk"][0], p["c_blk"], p["coef_fwd"][0], False, f"s5_fwd_f_{li}")
    y_b, xr_b, xi_b = _s5_scan_fwd(proj, p["b_blk"][1], p["c_blk"], p["coef_fwd"][1], True, f"s5_fwd_b_{li}")
    sv["states"] = [(xr_f, xi_f), (xr_b, xi_b)]
    y_raw, yg = _rowwise(lambda a, b, u, d: (a + b + d * u, _gelu(a + b + d * u)),
                         [(y_f, SSM_WIDTH, 0), (y_b, SSM_WIDTH, 0), (proj, SSM_WIDTH, P_U // SSM_WIDTH)], [p["d"]],
                         [(SSM_WIDTH, F32), (SSM_WIDTH, BF16)], name=f"s5_gelu_{li}")
    z = _mm_nn(yg, p["w_glu"], name=f"glu_proj_{li}")
    y_ssm = _rowwise(lambda yr, zv, b: _gelu(yr) * _sigmoid(zv + b), [(y_raw, SSM_WIDTH, 0), (z, SSM_WIDTH, 0)],
                     [p["b_glu"]], [(SSM_WIDTH, BF16)], name=f"glu_{li}")[0]
    sv.update(y_raw=y_raw, yg=yg, z=z, y_ssm=y_ssm)
    cqn, ckvn = _rowwise(lambda cq, ckv, gq, gkv: (_rms(cq, gq, Q_LORA), _rms(ckv, gkv, KV_LORA)),
                         [(proj, Q_LORA, P_CQ // Q_LORA), (proj, KV_LORA, P_CKV // KV_LORA)], [p["q_g"], p["kv_g"]],
                         [(Q_LORA, BF16), (KV_LORA, BF16)], name=f"lora_norm_{li}")
    q_raw = _mm_nn(cqn, p["w_q_p"], name=f"q_up_{li}")
    kv_raw = _mm_nn(ckvn, p["w_kv_p"], name=f"kv_up_{li}")
    q, k, v = _head_prep_fwd(q_raw, kv_raw, proj, tabs, p, li)
    o, lse, *gathered = _attn_fwd(q, k, v, f"attn_fwd_{li}", exchange)
    sv.update(cqn=cqn, ckvn=ckvn, q_raw=q_raw, kv_raw=kv_raw, q=q, k=k, v=v, o=o, lse=lse)
    t_ssm = _mm_nn(y_ssm, p["w_out_ssm"], name=f"out_ssm_{li}")
    t_mla = _mm_nn(o, p["w_out_mla_p"], name=f"out_mla_{li}")
    merged = _rowwise(lambda g0, g1, ts, tmv, b: _sigmoid(g0 + b[0:1]) * ts + _sigmoid(g1 + b[1:2]) * tmv,
                      [(proj, D_MODEL, 0), (proj, D_MODEL, 1), (t_ssm, D_MODEL, 0), (t_mla, D_MODEL, 0)],
                      [p["b_gate"]], [(D_MODEL, BF16)], name=f"merge_{li}")[0]
    x1 = _mm_nn(merged, p["w_o"], add=x, name=f"o_proj_{li}")
    sv.update(t_ssm=t_ssm, t_mla=t_mla, merged=merged, x1=x1)
    h2 = _rowwise(lambda xv, g: _rms(xv, g, D_MODEL), [(x1, D_MODEL, 0)], [p["ffn_g"]], [(D_MODEL, BF16)],
                  name=f"ffn_norm_{li}")[0]
    a, r = _mm_nn(h2, p["w_ff1"], epilogue=lambda acc: (acc, jnp.square(jnp.maximum(acc, 0.0))),
                  out_dtypes=(F32, BF16), name=f"ff1_{li}")
    x2 = _mm_nn(r, p["w_ff2"], add=x1, name=f"ff2_{li}")
    sv.update(h2=h2, a=a, r=r)
    return x2, sv, gathered


def _layer_bwd(dx2, dx2_b, sv, p, tabs, li, pending=None, send_early=None):
    g = {}
    da = _mm_nt(dx2_b, p["w_ff2"], extras=(sv["a"],), epilogue=lambda acc, av: acc * (2.0 * jnp.maximum(av, 0.0)),
                out_dtypes=(BF16,), name=f"d_ff2_x_{li}")
    g["w_ff2"] = _mm_tn(sv["r"], dx2_b, name=f"d_ff2_w_{li}")
    dh2 = _mm_nt(da, p["w_ff1"], name=f"d_ff1_x_{li}")
    g["w_ff1"] = _mm_tn(sv["h2"], da, name=f"d_ff1_w_{li}")

    def norm_bwd(xv, dyv, dres, gg):
        dx, dg = _rms_bwd(xv, gg, dyv, D_MODEL)
        return dres + dx, dres + dx, dg

    dx1, dx1_b, dg = _rowwise(norm_bwd, [(sv["x1"], D_MODEL, 0), (dh2, D_MODEL, 0), (dx2, D_MODEL, 0)], [p["ffn_g"]],
                              [(D_MODEL, F32), (D_MODEL, BF16)], [(D_MODEL, F32)], name=f"d_ffn_norm_{li}")
    g["ffn_norm_g"] = dg.reshape(D_MODEL)
    dmerged = _mm_nt(dx1_b, p["w_o"], name=f"d_o_x_{li}")
    g["w_o"] = _mm_tn(sv["merged"], dx1_b, name=f"d_o_w_{li}")

    def merge_bwd(dm, g0, g1, ts, tmv, b):
        s0, s1 = _sigmoid(g0 + b[0:1]), _sigmoid(g1 + b[1:2])
        dg0, dg1 = dm * ts * s0 * (1.0 - s0), dm * tmv * s1 * (1.0 - s1)
        return (dm * s0, dm * s1, jnp.concatenate([dg0, dg1], axis=1),
                jnp.sum(dg0, axis=0, keepdims=True), jnp.sum(dg1, axis=0, keepdims=True))

    proj = sv["proj"]
    dt_ssm, dt_mla, dgate, db0, db1 = _rowwise(
        merge_bwd, [(dmerged, D_MODEL, 0), (proj, D_MODEL, 0), (proj, D_MODEL, 1), (sv["t_ssm"], D_MODEL, 0),
                    (sv["t_mla"], D_MODEL, 0)], [p["b_gate"]],
        [(D_MODEL, BF16), (D_MODEL, BF16), (2 * D_MODEL, BF16)], [(D_MODEL, F32), (D_MODEL, F32)], tm=256,
        name=f"d_merge_{li}")
    g["b_gate"] = jnp.concatenate([db0, db1], axis=0)
    dy_ssm = _mm_nt(dt_ssm, p["w_out_ssm"], name=f"d_out_ssm_x_{li}")
    g["w_out_ssm"] = _mm_tn(sv["y_ssm"], dt_ssm, name=f"d_out_ssm_w_{li}")
    do = _mm_nt(dt_mla, p["w_out_mla_p"], name=f"d_out_mla_x_{li}")
    g["w_out_mla"] = _unpad_out_mla(_mm_tn(sv["o"], dt_mla, name=f"d_out_mla_w_{li}"))

    def glu_bwd(dyv, yr, zv, b):
        yg = _gelu(yr)
        sg = _sigmoid(zv + b)
        dz = dyv * yg * sg * (1.0 - sg)
        return dz, dyv * sg, jnp.sum(dz, axis=0, keepdims=True)

    dz, dyg_direct, dbglu = _rowwise(glu_bwd, [(dy_ssm, SSM_WIDTH, 0), (sv["y_raw"], SSM_WIDTH, 0), (sv["z"], SSM_WIDTH, 0)],
                                     [p["b_glu"]], [(SSM_WIDTH, BF16), (SSM_WIDTH, F32)], [(SSM_WIDTH, F32)],
                                     name=f"d_glu_{li}")
    g["b_glu"] = dbglu.reshape(SSM_WIDTH)
    dyg_mm = _mm_nt(dz, p["w_glu"], name=f"d_glu_x_{li}")
    g["w_glu"] = _mm_tn(sv["yg"], dz, name=f"d_glu_w_{li}")

    def gelu_bwd(d1, d2, yr, u, d):
        dyr = (d1 + d2) * _gelu_grad(yr)
        return dyr, dyr * d, jnp.sum(dyr * u, axis=0, keepdims=True)

    dy_raw, du_d, dd = _rowwise(gelu_bwd, [(dyg_direct, SSM_WIDTH, 0), (dyg_mm, SSM_WIDTH, 0), (sv["y_raw"], SSM_WIDTH, 0),
                                           (proj, SSM_WIDTH, P_U // SSM_WIDTH)], [p["d"]],
                                [(SSM_WIDTH, BF16), (SSM_WIDTH, F32)], [(SSM_WIDTH, F32)], name=f"d_gelu_{li}")
    g["ssm_d"] = dd.reshape(SSM_GROUPS, SSM_GROUP)
    du_parts, dc_sum = [du_d], None
    zoh_grads = []
    for dr_i in range(2):
        xr, xi = sv["states"][dr_i]
        du_i, da_i, db_i, dc_i = _s5_scan_bwd(dy_raw, proj, xr, xi, p["b_blk"][dr_i], p["c_blk"], p["coef_adj"][dr_i],
                                              dr_i == 0, f"s5_bwd_{'fb'[dr_i]}_{li}")
        du_parts.append(du_i)
        dc_sum = dc_i if dc_sum is None else dc_sum + dc_i
        da_i = jnp.sum(da_i, axis=1)
        dar = da_i[:, :CHUNK_STATE].reshape(SSM_GROUPS, SSM_STATE)
        dai = da_i[:, CHUNK_STATE:].reshape(SSM_GROUPS, SSM_STATE)
        dbr, dbi = _b_unblock(db_i)
        zoh_grads.append(p["zoh_vjp"][dr_i]((dar, dai, dbr, dbi)))
    for k_i, nm in enumerate(("ssm_lam_re", "ssm_lam_im", "ssm_log_step", "ssm_b_re", "ssm_b_im")):
        g[nm] = jnp.stack([zoh_grads[0][k_i], zoh_grads[1][k_i]], axis=0)
    g["ssm_c_re"], g["ssm_c_im"] = _c_unblock(dc_sum)
    du = _rowwise(lambda a, b, c: a + b + c, [(d_, SSM_WIDTH, 0) for d_ in du_parts], [], [(SSM_WIDTH, BF16)],
                  name=f"d_u_sum_{li}")[0]
    bufs, flags = pending if pending is not None else ([], [])
    n_pending = len(bufs)
    if send_early is not None:
        early_bufs, early_flags = send_early(g)
        bufs, flags = list(bufs) + list(early_bufs), list(flags) + list(early_flags)
    dq, dk, dv, *got = _attn_bwd(sv["q"], sv["k"], sv["v"], sv["o"], do, sv["lse"], f"attn_bwd_{li}",
                                 bufs or None, tuple(flags))
    dq_raw, dkv_raw, dkr, dgq, dgk = _head_prep_bwd(dq, dk, dv, sv["q_raw"], sv["kv_raw"], proj, tabs, p, li)
    g["q_head_g"] = dgq.reshape(HEAD_PAD)[:QK_HEAD]
    g["k_head_g"] = dgk.reshape(HEAD_PAD)[:QK_HEAD]
    dcqn = _mm_nt(dq_raw, p["w_q_p"], name=f"d_q_up_x_{li}")
    g["w_q_up"] = _unpad_heads_cols(_mm_tn(sv["cqn"], dq_raw, name=f"d_q_up_w_{li}"), QK_HEAD)
    dckvn = _mm_nt(dkv_raw, p["w_kv_p"], name=f"d_kv_up_x_{li}")
    g["w_kv_up"] = _unpad_kv(_mm_tn(sv["ckvn"], dkv_raw, name=f"d_kv_up_w_{li}"))

    def lora_bwd(cq, ckv, d1, d2, gq, gkv):
        dx1_, dg1 = _rms_bwd(cq, gq, d1, Q_LORA)
        dx2_, dg2 = _rms_bwd(ckv, gkv, d2, KV_LORA)
        return dx1_, dx2_, dg1, dg2

    dcq, dckv, dgqn, dgkvn = _rowwise(
        lora_bwd, [(proj, Q_LORA, P_CQ // Q_LORA), (proj, KV_LORA, P_CKV // KV_LORA), (dcqn, Q_LORA, 0), (dckvn, KV_LORA, 0)],
        [p["q_g"], p["kv_g"]], [(Q_LORA, BF16), (KV_LORA, BF16)], [(Q_LORA, F32), (KV_LORA, F32)], name=f"d_lora_norm_{li}")
    g["q_norm_g"], g["kv_norm_g"] = dgqn.reshape(Q_LORA), dgkvn.reshape(KV_LORA)
    gap = jnp.zeros((dx2.shape[0], P_CQ - P_KR - HEAD_PAD), BF16)
    tail = jnp.zeros((dx2.shape[0], P_COLS - P_CQ - Q_LORA), BF16)
    dproj = jnp.concatenate([dgate, du, dckv, dkr, gap, dcq, tail], axis=1)
    dh = _mm_nt(dproj, p["w_in_p"], name=f"d_in_x_{li}")
    g["w_in"] = _unpad_w_in(_mm_tn(sv["h"], dproj, name=f"d_in_w_{li}"))
    dx, dx_b, dg = _rowwise(norm_bwd, [(sv["x"], D_MODEL, 0), (dh, D_MODEL, 0), (dx1, D_MODEL, 0)], [p["mix_g"]],
                            [(D_MODEL, F32), (D_MODEL, BF16)], [(D_MODEL, F32)], name=f"d_mix_norm_{li}")
    g["mix_norm_g"] = dg.reshape(D_MODEL)
    return dx, dx_b, g, got[:n_pending], got[n_pending:]


def _local_step(x, target, layer_weights, send_weights=None, send_early=None, send_late=None):
    tabs = _rope_tables(x.shape[0])
    saved, preps = [], []
    gathered = None
    for li in range(DEPTH):
        p = _prep_layer(layer_weights(li, gathered))
        nxt = send_weights(li + 1) if send_weights is not None and li + 1 < DEPTH else None
        x, sv, gathered = _layer_fwd(x, p, tabs, li, nxt)
        saved.append(sv)
        preps.append(p)

    def loss_fn(y, t):
        err = y - t
        d = err * (1.0 / D_MODEL)
        return d, d, jnp.sum(jnp.sum(err * err, axis=1, keepdims=True), axis=0, keepdims=True) * jnp.ones((1, LANES), F32)

    dx, dx_b, lsum = _rowwise(loss_fn, [(x, D_MODEL, 0), (target, D_MODEL, 0)], [], [(D_MODEL, F32), (D_MODEL, BF16)],
                              [(LANES, F32)], name="loss")
    loss = 0.5 * lsum[0, 0] * (1.0 / D_MODEL)
    grads, early, late = [None] * DEPTH, [None] * DEPTH, [None] * DEPTH
    pending = None
    for li in reversed(range(DEPTH)):
        dx, dx_b, grads[li], got_late, early[li] = _layer_bwd(dx, dx_b, saved[li], preps[li], tabs, li, pending, send_early)
        if pending is not None:
            late[li + 1] = got_late
        pending = send_late(grads[li]) if send_late is not None else None
    return loss, dx, grads, early, late, pending


def _exchange(bufs, scatter, name):
    n = len(bufs)

    def body(*refs):
        copies = _exchange_copies(refs[:n], refs[n:2 * n], *refs[2 * n:], scatter)
        for cp in copies:
            cp.start()
        for cp in copies:
            cp.wait()

    shapes = [tuple(b.shape[1:]) if sc else tuple(b.shape) for b, sc in zip(bufs, scatter)]
    return pl.pallas_call(
        body, out_shape=[jax.ShapeDtypeStruct((N_DEV,) + s, b.dtype) for s, b in zip(shapes, bufs)],
        in_specs=[pl.BlockSpec(memory_space=pl.ANY)] * n, out_specs=[pl.BlockSpec(memory_space=pl.ANY)] * n,
        scratch_shapes=[pltpu.SemaphoreType.DMA((n, N_DEV)), pltpu.SemaphoreType.DMA((n, N_DEV)),
                        pltpu.SemaphoreType.DMA((n,))],
        name=name, compiler_params=pltpu.CompilerParams(has_side_effects=True),
    )(*bufs)


def _adamw(parts, w, m, v, name):
    shape = w.shape
    cols = shape[-1]
    r = math.prod(shape[:-1])
    parts, w, m, v = parts.reshape(N_DEV, r, cols), w.reshape(r, cols), m.reshape(r, cols), v.reshape(r, cols)
    tm = _pick_rows(r, cols)

    def body(p_ref, w_ref, m_ref, v_ref, g_ref, d_ref, nm_ref, nv_ref):
        g = p_ref[0].astype(F32)
        for j in range(1, N_DEV):
            g = g + p_ref[j].astype(F32)
        m_new = ADAM_B1 * m_ref[...] + (1.0 - ADAM_B1) * g
        v_new = ADAM_B2 * v_ref[...] + (1.0 - ADAM_B2) * (g * g)
        m_hat = m_new / (1.0 - ADAM_B1 ** ADAM_STEP)
        v_hat = v_new / (1.0 - ADAM_B2 ** ADAM_STEP)
        g_ref[...] = g
        d_ref[...] = -ADAM_LR * (m_hat / (jnp.sqrt(v_hat) + ADAM_EPS) + ADAM_WD * w_ref[...])
        nm_ref[...] = m_new
        nv_ref[...] = v_new

    spec = pl.BlockSpec((tm, cols), lambda i: (i, 0))
    res = pl.pallas_call(
        body, grid=(r // tm,), in_specs=[pl.BlockSpec((N_DEV, tm, cols), lambda i: (0, i, 0)), spec, spec, spec],
        out_specs=[spec] * 4, out_shape=[jax.ShapeDtypeStruct((r, cols), F32)] * 4, name=name,
        compiler_params=_cparams(("parallel",)),
    )(parts, w, m, v)
    return [a.reshape(shape) for a in res]


def _pick_rows(r, cols):
    for t in (512, 256, 128, 64, 32, 16):
        if r % t == 0 and t * cols <= 512 * 512:
            return t
    return r


def _pack(arrs, dtype, row_mult):
    flat = jnp.concatenate([a.reshape(-1).astype(dtype) for a in arrs])
    n = flat.shape[0]
    per = row_mult * D_MODEL
    total = -(-n // per) * per
    return jnp.pad(flat, (0, total - n)).reshape(total // D_MODEL, D_MODEL)


def _unpack(flat, shapes):
    lead = flat.shape[:-2]
    flat = flat.reshape(lead + (-1,))
    out, off = [], 0
    for shp in shapes:
        n = math.prod(shp)
        out.append(flat[..., off:off + n].reshape(lead + tuple(shp)))
        off += n
    return out


def _to_shards(gfull, axis):
    shp = gfull.shape
    gfull = gfull.reshape(shp[:axis] + (N_DEV, shp[axis] // N_DEV) + shp[axis + 1:])
    return jnp.moveaxis(gfull, axis, 0)


def _from_shards(parts, axis):
    parts = jnp.moveaxis(parts, 0, axis)
    shp = parts.shape
    return parts.reshape(shp[:axis] + (shp[axis] * shp[axis + 1],) + shp[axis + 2:])


def kernel(x, mix_norm_g, w_in, b_gate, ssm_lam_re, ssm_lam_im, ssm_log_step, ssm_b_re, ssm_b_im, ssm_c_re, ssm_c_im, ssm_d, w_glu, b_glu, w_out_ssm, q_norm_g, kv_norm_g, w_q_up, w_kv_up, q_head_g, k_head_g, w_out_mla, w_o, ffn_norm_g, w_ff1, w_ff2, loss_target, m_mix_norm_g, m_w_in, m_b_gate, m_ssm_lam_re, m_ssm_lam_im, m_ssm_log_step, m_ssm_b_re, m_ssm_b_im, m_ssm_c_re, m_ssm_c_im, m_ssm_d, m_w_glu, m_b_glu, m_w_out_ssm, m_q_norm_g, m_kv_norm_g, m_w_q_up, m_w_kv_up, m_q_head_g, m_k_head_g, m_w_out_mla, m_w_o, m_ffn_norm_g, m_w_ff1, m_w_ff2, v_mix_norm_g, v_w_in, v_b_gate, v_ssm_lam_re, v_ssm_lam_im, v_ssm_log_step, v_ssm_b_re, v_ssm_b_im, v_ssm_c_re, v_ssm_c_im, v_ssm_d, v_w_glu, v_b_glu, v_w_out_ssm, v_q_norm_g, v_kv_norm_g, v_w_q_up, v_w_kv_up, v_q_head_g, v_k_head_g, v_w_out_mla, v_w_o, v_ffn_norm_g, v_w_ff1, v_w_ff2):
    args = dict(locals())
    w = {n: args[n] for n in WEIGHTS}
    m = {n: args["m_" + n] for n in WEIGHTS}
    v = {n: args["v_" + n] for n in WEIGHTS}

    def send_weights(li):
        return [w[n][li] if n == "b_gate" else w[n][li].astype(BF16) for n in SHARDED]

    first = _exchange(send_weights(0), [False] * len(SHARDED), "weight_all_gather_0")

    def layer_weights(li, gathered):
        full = {n: _from_shards(pt, SHARD_AXIS[n] - 1) for n, pt in zip(SHARDED, first if li == 0 else gathered)}
        for n in REPLICATED:
            full[n] = w[n][li]
        return full

    def shards(g, names):
        return [_to_shards(g[n], SHARD_AXIS[n] - 1).astype(BF16) for n in names]

    def send_early(g):
        return shards(g, EARLY), [True] * len(EARLY)

    def send_late(g):
        return shards(g, LATE) + [_pack([g[n] for n in REPLICATED], F32, 8)], [True] * len(LATE) + [False]

    loss_part, dx, grads, early, late, pending = _local_step(x[0], loss_target[0], layer_weights, send_weights,
                                                             send_early, send_late)
    late[0] = _exchange(*pending, "grad_exchange_0")
    loss = lax.psum(loss_part, ("x", "y", "c"))

    outs = {}
    for n in SHARDED:
        src, t = (early, EARLY.index(n)) if n in EARLY else (late, LATE.index(n))
        parts = jnp.stack([src[li][t] for li in range(DEPTH)], axis=1)
        for kind, a in zip(("grad", "delta", "new_m", "new_v"), _adamw(parts, w[n], m[n], v[n], "adamw_" + n)):
            outs[kind + "_" + n] = a
    r_parts = jnp.stack([late[li][len(LATE)] for li in range(DEPTH)], axis=1)

    def pack_layers(d):
        return jnp.stack([_pack([d[n][li] for n in REPLICATED], F32, 8) for li in range(DEPTH)], axis=0)

    res = _adamw(r_parts, pack_layers(w), pack_layers(m), pack_layers(v), "adamw_replicated")
    rep_shapes = [w[n].shape[1:] for n in REPLICATED]
    for kind, flat in zip(("grad", "delta", "new_m", "new_v"), res):
        for n, a in zip(REPLICATED, _unpack(flat, rep_shapes)):
            outs[kind + "_" + n] = a
    return (loss, dx[None], *[outs[k + "_" + n] for k in ("grad", "delta", "new_m", "new_v") for n in WEIGHTS])
```

```python
import functools
import math

import jax
import jax.numpy as jnp
from jax import lax
from jax.experimental import pallas as pl
from jax.experimental.pallas import tpu as pltpu

F32 = jnp.float32
BF16 = jnp.bfloat16
_MXU = jnp.bfloat16

D_MODEL = 1024
DEPTH = 4
SSM_WIDTH = 512
SSM_GROUP = 16
SSM_GROUPS = 32
SSM_STATE = 64
MLA_HEADS = 8
QK_NOPE = 64
QK_ROPE = 32
QK_HEAD = 96
V_HEAD = 64
Q_LORA = 384
KV_LORA = 256
ROPE_THETA = 10000.0
D_FF = 4096
EPS = 1e-6
HEAD_PAD = 128
N_DEV = 8
LANES = 128
SUBLANES = 8
CHUNK_GROUPS = 8
N_CHUNKS = SSM_GROUPS // CHUNK_GROUPS
CHUNK_STATE = CHUNK_GROUPS * SSM_STATE
S5_T_BLK = 2048

P_GATE, P_U, P_CKV, P_KR, P_CQ = 0, 2048, 2560, 2816, 3072
P_COLS = 3584
IN_U, IN_CQ, IN_CKV, IN_KR, IN_GATE = 0, 512, 896, 1152, 1184
IN_COLS = 3232

ADAM_LR = 0.001
ADAM_B1 = 0.9
ADAM_B2 = 0.999
ADAM_EPS = 1e-08
ADAM_WD = 0.01
ADAM_STEP = 10

VMEM_LIMIT = 56 * 1024 * 1024
TN_OUT_BLOCK_BYTES = 8 * 1024 * 1024
MXU_WIDTH = 256
WIDE_BLOCK_MAX_K = 1024

SHARDED = ("w_in", "b_gate", "w_glu", "w_out_ssm", "w_q_up", "w_kv_up", "w_out_mla", "w_o", "w_ff1", "w_ff2")
SHARD_AXIS = {"w_in": 2, "b_gate": 2, "w_glu": 1, "w_out_ssm": 2, "w_q_up": 2, "w_kv_up": 2, "w_out_mla": 2,
              "w_o": 1, "w_ff1": 2, "w_ff2": 1}
EARLY = ("b_gate", "w_glu", "w_out_ssm", "w_out_mla", "w_o", "w_ff1", "w_ff2")
LATE = ("w_in", "w_q_up", "w_kv_up")
REPLICATED = ("mix_norm_g", "ssm_lam_re", "ssm_lam_im", "ssm_log_step", "ssm_b_re", "ssm_b_im", "ssm_c_re",
              "ssm_c_im", "ssm_d", "b_glu", "q_norm_g", "kv_norm_g", "q_head_g", "k_head_g", "ffn_norm_g")
WEIGHTS = ("mix_norm_g", "w_in", "b_gate", "ssm_lam_re", "ssm_lam_im", "ssm_log_step", "ssm_b_re", "ssm_b_im",
           "ssm_c_re", "ssm_c_im", "ssm_d", "w_glu", "b_glu", "w_out_ssm", "q_norm_g", "kv_norm_g", "w_q_up",
           "w_kv_up", "q_head_g", "k_head_g", "w_out_mla", "w_o", "ffn_norm_g", "w_ff1", "w_ff2")


def _cparams(sem):
    return pltpu.CompilerParams(dimension_semantics=sem, vmem_limit_bytes=VMEM_LIMIT)


def _dot(a, b, dims):
    return lax.dot_general(a.astype(_MXU), b.astype(_MXU), (dims, ((), ())), preferred_element_type=F32)


def _dot_nn(a, b):
    return _dot(a, b, ((1,), (0,)))


def _dot_nt(a, b):
    return _dot(a, b, ((1,), (1,)))


def _dot_tn(a, b):
    return _dot(a, b, ((0,), (0,)))


def _rowwise(fn, rows, consts, outs, accs=(), *, tm=512, name):
    n_rows = rows[0][0].shape[0]
    tm = min(tm, n_rows)
    n_in = len(rows) + len(consts)
    n_o, n_a = len(outs), len(accs)

    def body(*refs):
        res = fn(*[r[...] for r in refs[:n_in]])
        if not isinstance(res, (tuple, list)):
            res = (res,)
        orefs = refs[n_in:]
        for k in range(n_o):
            orefs[k][...] = res[k].astype(orefs[k].dtype)
        if n_a:
            @pl.when(pl.program_id(0) == 0)
            def _():
                for k in range(n_a):
                    orefs[n_o + k][...] = jnp.zeros_like(orefs[n_o + k])
            for k in range(n_a):
                orefs[n_o + k][...] += res[n_o + k]

    in_specs = [pl.BlockSpec((tm, w), functools.partial(lambda i, j: (i, j), j=j)) for (_, w, j) in rows]
    in_specs += [pl.BlockSpec(c.shape, functools.partial(lambda i, nd: (0,) * nd, nd=c.ndim)) for c in consts]
    out_specs = [pl.BlockSpec((tm, w), lambda i: (i, 0)) for (w, _) in outs]
    out_specs += [pl.BlockSpec((1, w), lambda i: (0, 0)) for (w, _) in accs]
    out_shape = [jax.ShapeDtypeStruct((n_rows, w), dt) for (w, dt) in outs]
    out_shape += [jax.ShapeDtypeStruct((1, w), dt) for (w, dt) in accs]
    res = pl.pallas_call(
        body, grid=(n_rows // tm,), in_specs=in_specs, out_specs=out_specs, out_shape=out_shape, name=name,
        compiler_params=_cparams(("arbitrary",) if n_a else ("parallel",)),
    )(*[r[0] for r in rows], *consts)
    return res


def _pick(n, cap):
    if n <= cap:
        return n
    for unit in (MXU_WIDTH, LANES):
        best = 0
        for t in range(unit, cap + 1, unit):
            if n % t == 0:
                best = t
        if best:
            return best
    return n


def _mm(a, b, transpose_b, extras, epilogue, out_dtypes, name):
    m, k = a.shape
    n = b.shape[0] if transpose_b else b.shape[1]
    wide = k <= WIDE_BLOCK_MAX_K
    tm, tn = min(1024 if wide else 512, m), _pick(n, 2048 if wide else 1024)
    n_in = 2 + len(extras)

    def body(*refs):
        acc = (_dot_nt if transpose_b else _dot_nn)(refs[0][...], refs[1][...])
        res = epilogue(acc, *[r[...] for r in refs[2:n_in]]) if epilogue is not None else acc
        if not isinstance(res, (tuple, list)):
            res = (res,)
        for o_ref, val in zip(refs[n_in:], res):
            o_ref[...] = val.astype(o_ref.dtype)

    blk = pl.BlockSpec((tm, tn), lambda j, i: (i, j))
    b_spec = pl.BlockSpec((tn, k), lambda j, i: (j, 0)) if transpose_b else pl.BlockSpec((k, tn), lambda j, i: (0, j))
    res = pl.pallas_call(
        body, grid=(n // tn, m // tm),
        in_specs=[pl.BlockSpec((tm, k), lambda j, i: (i, 0)), b_spec] + [blk] * len(extras),
        out_specs=[blk] * len(out_dtypes), out_shape=[jax.ShapeDtypeStruct((m, n), dt) for dt in out_dtypes],
        name=name, compiler_params=_cparams(("parallel", "parallel")),
    )(a, b, *extras)
    return res[0] if len(out_dtypes) == 1 else res


def _mm_nn(a, b, *, add=None, extras=(), epilogue=None, out_dtypes=(F32,), name):
    if add is not None:
        extras, epilogue = (add,), (lambda acc, r: acc + r)
    return _mm(a, b, False, tuple(extras), epilogue, out_dtypes, name)


def _mm_nt(a, b, *, extras=(), epilogue=None, out_dtypes=(F32,), name):
    return _mm(a, b, True, tuple(extras), epilogue, out_dtypes, name)


def _mm_tn(a, b, *, a_cols=None, name):
    s = a.shape[0]
    n = b.shape[1]
    mw, mj = (a.shape[1], 0) if a_cols is None else a_cols
    ts = min(1024, s)
    tm = _pick(mw, 1024)
    tn = _pick(n, max(1024, TN_OUT_BLOCK_BYTES // (4 * tm)))
    n_mb = mw // tm

    def body(a_ref, b_ref, o_ref):
        @pl.when(pl.program_id(2) == 0)
        def _():
            o_ref[...] = jnp.zeros_like(o_ref)
        o_ref[...] += _dot_tn(a_ref[...], b_ref[...])

    return pl.pallas_call(
        body, grid=(n_mb, n // tn, s // ts),
        in_specs=[pl.BlockSpec((ts, tm), lambda i, j, t: (t, mj * n_mb + i)), pl.BlockSpec((ts, tn), lambda i, j, t: (t, j))],
        out_specs=pl.BlockSpec((tm, tn), lambda i, j, t: (i, j)),
        out_shape=jax.ShapeDtypeStruct((mw, n), F32), name=name,
        compiler_params=_cparams(("parallel", "parallel", "arbitrary")),
    )(a, b)


def _rms(x, g, n):
    r = lax.rsqrt(jnp.sum(x * x, axis=-1, keepdims=True) * (1.0 / n) + EPS)
    return x * r * g


def _rms_bwd(x, g, dy, n):
    r = lax.rsqrt(jnp.sum(x * x, axis=-1, keepdims=True) * (1.0 / n) + EPS)
    xr = x * r
    dyg = dy * g
    dx = r * dyg - xr * (r * r) * (jnp.sum(dyg * x, axis=-1, keepdims=True) * (1.0 / n))
    return dx, jnp.sum(dy * xr, axis=0, keepdims=True)


def _gelu(x):
    c = math.sqrt(2.0 / math.pi)
    return 0.5 * x * (1.0 + jnp.tanh(c * (x + 0.044715 * (x * x * x))))


def _gelu_grad(x):
    c = math.sqrt(2.0 / math.pi)
    t = jnp.tanh(c * (x + 0.044715 * (x * x * x)))
    return 0.5 * (1.0 + t) + 0.5 * x * (1.0 - t * t) * (c * (1.0 + 3.0 * 0.044715 * (x * x)))


def _sigmoid(x):
    return 1.0 / (1.0 + jnp.exp(-x))


def _rope(x, cf, sa, sb):
    return x * cf + pltpu.roll(x, HEAD_PAD - QK_ROPE // 2, 1) * sa + pltpu.roll(x, QK_ROPE // 2, 1) * sb


def _rope_t(d, cf, sa, sb):
    return d * cf + pltpu.roll(d * sa, QK_ROPE // 2, 1) + pltpu.roll(d * sb, HEAD_PAD - QK_ROPE // 2, 1)


def _scan_tables(ar, ai, reverse):
    ar = ar.reshape(N_CHUNKS, CHUNK_STATE)
    ai = ai.reshape(N_CHUNKS, CHUNK_STATE)
    pr, pi = [ar], [ai]
    for _ in range(SUBLANES - 1):
        pr, pi = pr + [pr[-1] * ar - pi[-1] * ai], pi + [pr[-1] * ai + pi[-1] * ar]
    row = jnp.arange(SUBLANES)[None, :, None]
    tiles = []
    for k in (1, 2, 4):
        mask = (row <= SUBLANES - 1 - k) if reverse else (row >= k)
        tiles.append(jnp.where(mask, pr[k - 1][:, None, :], 0.0))
        tiles.append(jnp.where(mask, pi[k - 1][:, None, :], 0.0))
    order = list(range(SUBLANES))[::-1] if reverse else list(range(SUBLANES))
    tiles.append(jnp.stack([pr[j] for j in order], axis=1))
    tiles.append(jnp.stack([pi[j] for j in order], axis=1))
    return jnp.stack(tiles, axis=1).astype(F32)


def _slab_scan(xr, xi, coef, carry_r, carry_i, reverse):
    for idx, k in enumerate((1, 2, 4)):
        sh = SUBLANES - k if reverse else k
        sr, si = pltpu.roll(xr, sh, 0), pltpu.roll(xi, sh, 0)
        cr, ci = coef[2 * idx], coef[2 * idx + 1]
        xr, xi = xr + cr * sr - ci * si, xi + cr * si + ci * sr
    pr, pi = coef[6], coef[7]
    xr = xr + pr * carry_r - pi * carry_i
    xi = xi + pr * carry_i + pi * carry_r
    return xr, xi


def _s5_scan_fwd(proj, b_blk, c_blk, coef, reverse, name):
    s = proj.shape[0]
    t_blk = min(S5_T_BLK, s)
    n_t = s // t_blk
    n_slab = t_blk // SUBLANES
    last = 0 if reverse else SUBLANES - 1

    def tmap(t):
        return n_t - 1 - t if reverse else t

    def body(u_ref, b_ref, c_ref, coef_ref, y_ref, xr_ref, xi_ref, carry_ref):
        @pl.when(pl.program_id(1) == 0)
        def _():
            carry_ref[...] = jnp.zeros_like(carry_ref)
        bu = _dot_nn(u_ref[...], b_ref[0])
        xr_ref[...] = bu[:, :CHUNK_STATE]
        xi_ref[...] = bu[:, CHUNK_STATE:]
        coef_v = [coef_ref[0, k] for k in range(8)]

        def slab(i, carry):
            sl = (n_slab - 1 - i) if reverse else i
            rows = pl.ds(pl.multiple_of(sl * SUBLANES, SUBLANES), SUBLANES)
            xr, xi = _slab_scan(xr_ref[rows, :], xi_ref[rows, :], coef_v, carry[0], carry[1], reverse)
            xr_ref[rows, :] = xr
            xi_ref[rows, :] = xi
            return (jnp.broadcast_to(xr[last:last + 1, :], xr.shape), jnp.broadcast_to(xi[last:last + 1, :], xi.shape))

        cr, ci = lax.fori_loop(0, n_slab, slab, (carry_ref[0], carry_ref[1]))
        carry_ref[0] = cr
        carry_ref[1] = ci
        y_ref[...] = _dot_nn(xr_ref[...], c_ref[0, :CHUNK_STATE, :]) + _dot_nn(xi_ref[...], c_ref[0, CHUNK_STATE:, :])

    u_blk0 = P_U // LANES
    return pl.pallas_call(
        body, grid=(N_CHUNKS, n_t),
        in_specs=[pl.BlockSpec((t_blk, LANES), lambda c, t: (tmap(t), u_blk0 + c)),
                  pl.BlockSpec((1, LANES, 2 * CHUNK_STATE), lambda c, t: (c, 0, 0)),
                  pl.BlockSpec((1, 2 * CHUNK_STATE, LANES), lambda c, t: (c, 0, 0)),
                  pl.BlockSpec((1, 8, SUBLANES, CHUNK_STATE), lambda c, t: (c, 0, 0, 0))],
        out_specs=[pl.BlockSpec((t_blk, LANES), lambda c, t: (tmap(t), c)),
                   pl.BlockSpec((t_blk, CHUNK_STATE), lambda c, t: (tmap(t), c)),
                   pl.BlockSpec((t_blk, CHUNK_STATE), lambda c, t: (tmap(t), c))],
        out_shape=[jax.ShapeDtypeStruct((s, SSM_WIDTH), F32),
                   jax.ShapeDtypeStruct((s, N_CHUNKS * CHUNK_STATE), F32),
                   jax.ShapeDtypeStruct((s, N_CHUNKS * CHUNK_STATE), F32)],
        scratch_shapes=[pltpu.VMEM((2, SUBLANES, CHUNK_STATE), F32)],
        name=name, compiler_params=_cparams(("parallel", "arbitrary")),
    )(proj, b_blk, c_blk, coef)


def _s5_scan_bwd(dy, proj, x_re, x_im, b_blk, c_blk, coef, reverse, name):
    s = dy.shape[0]
    t_blk = min(S5_T_BLK, s)
    n_t = s // t_blk
    n_slab = t_blk // SUBLANES
    last = 0 if reverse else SUBLANES - 1
    first = SUBLANES - 1 if reverse else 0

    def tmap(t):
        return n_t - 1 - t if reverse else t

    def body(dy_ref, u_ref, xr_ref, xi_ref, b_ref, c_ref, coef_ref, du_ref, da_ref, db_ref, dc_ref,
             carry_ref, lr_ref, li_ref):
        @pl.when(pl.program_id(1) == 0)
        def _():
            carry_ref[...] = jnp.zeros_like(carry_ref)
            da_ref[...] = jnp.zeros_like(da_ref)
            db_ref[...] = jnp.zeros_like(db_ref)
            dc_ref[...] = jnp.zeros_like(dc_ref)
        g = _dot_nt(dy_ref[...], c_ref[0])
        lr_ref[...] = g[:, :CHUNK_STATE]
        li_ref[...] = g[:, CHUNK_STATE:]
        coef_v = [coef_ref[0, k] for k in range(8)]
        row = lax.broadcasted_iota(jnp.int32, (SUBLANES, CHUNK_STATE), 0)
        sh_prev = SUBLANES - 1 if reverse else 1

        def slab(i, carry):
            cr, ci, ar_acc, ai_acc = carry
            sl = (n_slab - 1 - i) if reverse else i
            rows = pl.ds(pl.multiple_of(sl * SUBLANES, SUBLANES), SUBLANES)
            lr, li = _slab_scan(lr_ref[rows, :], li_ref[rows, :], coef_v, cr, ci, reverse)
            lr_ref[rows, :] = lr
            li_ref[rows, :] = li
            pr = jnp.where(row == first, cr, pltpu.roll(lr, sh_prev, 0))
            pi = jnp.where(row == first, ci, pltpu.roll(li, sh_prev, 0))
            xr, xi = xr_ref[rows, :], xi_ref[rows, :]
            ar_acc = ar_acc + xr * pr + xi * pi
            ai_acc = ai_acc + xr * pi - xi * pr
            return (jnp.broadcast_to(lr[last:last + 1, :], lr.shape), jnp.broadcast_to(li[last:last + 1, :], li.shape),
                    ar_acc, ai_acc)

        zero = jnp.zeros((SUBLANES, CHUNK_STATE), F32)
        cr, ci, ar_acc, ai_acc = lax.fori_loop(0, n_slab, slab, (carry_ref[0], carry_ref[1], zero, zero))
        carry_ref[0] = cr
        carry_ref[1] = ci
        da_ref[0, :, :CHUNK_STATE] += ar_acc
        da_ref[0, :, CHUNK_STATE:] += ai_acc
        lam_r, lam_i = lr_ref[...], li_ref[...]
        u = u_ref[...]
        du_ref[...] = _dot_nt(lam_r, b_ref[0, :, :CHUNK_STATE]) + _dot_nt(lam_i, b_ref[0, :, CHUNK_STATE:])
        db_ref[0, :, :CHUNK_STATE] += _dot_tn(u, lam_r)
        db_ref[0, :, CHUNK_STATE:] += _dot_tn(u, lam_i)
        dyv = dy_ref[...]
        dc_ref[0, :CHUNK_STATE, :] += _dot_tn(xr_ref[...], dyv)
        dc_ref[0, CHUNK_STATE:, :] += _dot_tn(xi_ref[...], dyv)

    u_blk0 = P_U // LANES
    return pl.pallas_call(
        body, grid=(N_CHUNKS, n_t),
        in_specs=[pl.BlockSpec((t_blk, LANES), lambda c, t: (tmap(t), c)),
                  pl.BlockSpec((t_blk, LANES), lambda c, t: (tmap(t), u_blk0 + c)),
                  pl.BlockSpec((t_blk, CHUNK_STATE), lambda c, t: (tmap(t), c)),
                  pl.BlockSpec((t_blk, CHUNK_STATE), lambda c, t: (tmap(t), c)),
                  pl.BlockSpec((1, LANES, 2 * CHUNK_STATE), lambda c, t: (c, 0, 0)),
                  pl.BlockSpec((1, 2 * CHUNK_STATE, LANES), lambda c, t: (c, 0, 0)),
                  pl.BlockSpec((1, 8, SUBLANES, CHUNK_STATE), lambda c, t: (c, 0, 0, 0))],
        out_specs=[pl.BlockSpec((t_blk, LANES), lambda c, t: (tmap(t), c)),
                   pl.BlockSpec((1, SUBLANES, 2 * CHUNK_STATE), lambda c, t: (c, 0, 0)),
                   pl.BlockSpec((1, LANES, 2 * CHUNK_STATE), lambda c, t: (c, 0, 0)),
                   pl.BlockSpec((1, 2 * CHUNK_STATE, LANES), lambda c, t: (c, 0, 0))],
        out_shape=[jax.ShapeDtypeStruct((s, SSM_WIDTH), F32),
                   jax.ShapeDtypeStruct((N_CHUNKS, SUBLANES, 2 * CHUNK_STATE), F32),
                   jax.ShapeDtypeStruct((N_CHUNKS, LANES, 2 * CHUNK_STATE), F32),
                   jax.ShapeDtypeStruct((N_CHUNKS, 2 * CHUNK_STATE, LANES), F32)],
        scratch_shapes=[pltpu.VMEM((2, SUBLANES, CHUNK_STATE), F32), pltpu.VMEM((t_blk, CHUNK_STATE), F32),
                        pltpu.VMEM((t_blk, CHUNK_STATE), F32)],
        name=name, compiler_params=_cparams(("parallel", "arbitrary")),
    )(dy, proj, x_re, x_im, b_blk, c_blk, coef)


def _zoh(lam_re, lam_im, log_step, b_re, b_im):
    step = jnp.exp(log_step)[:, None]
    mag = jnp.exp(lam_re * step)
    abar_r = mag * jnp.cos(lam_im * step)
    abar_i = mag * jnp.sin(lam_im * step)
    nr = abar_r - 1.0
    ni = abar_i
    den = lam_re * lam_re + lam_im * lam_im
    fr = (nr * lam_re + ni * lam_im) / den
    fi = (ni * lam_re - nr * lam_im) / den
    bbar_r = fr[..., None] * b_re - fi[..., None] * b_im
    bbar_i = fr[..., None] * b_im + fi[..., None] * b_re
    return abar_r, abar_i, bbar_r, bbar_i


def _b_block(bbar_r, bbar_i):
    eye = jnp.eye(CHUNK_GROUPS, dtype=F32)

    def one(b):
        b = b.reshape(N_CHUNKS, CHUNK_GROUPS, SSM_STATE, SSM_GROUP)
        return jnp.einsum("cgnp,gh->cgphn", b, eye).reshape(N_CHUNKS, LANES, CHUNK_STATE)

    return jnp.concatenate([one(bbar_r), one(bbar_i)], axis=2)


def _b_unblock(db):
    eye = jnp.eye(CHUNK_GROUPS, dtype=F32)

    def one(d):
        d = d.reshape(N_CHUNKS, CHUNK_GROUPS, SSM_GROUP, CHUNK_GROUPS, SSM_STATE)
        return jnp.einsum("cgphn,gh->cgnp", d, eye).reshape(SSM_GROUPS, SSM_STATE, SSM_GROUP)

    return one(db[:, :, :CHUNK_STATE]), one(db[:, :, CHUNK_STATE:])


def _c_block(c_re, c_im):
    eye = jnp.eye(CHUNK_GROUPS, dtype=F32)

    def one(c):
        c = c.reshape(N_CHUNKS, CHUNK_GROUPS, SSM_GROUP, SSM_STATE)
        return jnp.einsum("cgpn,gh->cgnhp", c, eye).reshape(N_CHUNKS, CHUNK_STATE, LANES)

    return jnp.concatenate([one(c_re), -one(c_im)], axis=1)


def _c_unblock(dc):
    eye = jnp.eye(CHUNK_GROUPS, dtype=F32)

    def one(d):
        d = d.reshape(N_CHUNKS, CHUNK_GROUPS, SSM_STATE, CHUNK_GROUPS, SSM_GROUP)
        return jnp.einsum("cgnhp,gh->cgpn", d, eye).reshape(SSM_GROUPS, SSM_GROUP, SSM_STATE)

    return one(dc[:, :CHUNK_STATE, :]), -one(dc[:, CHUNK_STATE:, :])


def _exchange_copies(ins, outs, send_sems, recv_sems, local_sems, scatter):
    x, y, c = lax.axis_index("x"), lax.axis_index("y"), lax.axis_index("c")
    me = 4 * x + 2 * y + c
    copies = [pltpu.make_async_copy(ins[t].at[me] if scatter[t] else ins[t], outs[t].at[me], local_sems.at[t])
              for t in range(len(ins))]
    for k in range(1, N_DEV):
        peer = (x ^ ((k >> 2) & 1), y ^ ((k >> 1) & 1), c ^ (k & 1))
        peer_idx = 4 * peer[0] + 2 * peer[1] + peer[2]
        for t in range(len(ins)):
            copies.append(pltpu.make_async_remote_copy(
                src_ref=ins[t].at[peer_idx] if scatter[t] else ins[t], dst_ref=outs[t].at[me],
                send_sem=send_sems.at[t, k], recv_sem=recv_sems.at[t, k], device_id=peer,
                device_id_type=pl.DeviceIdType.MESH))
    return copies


def _call_with_exchange(body, *, grid, in_specs, out_specs, out_shape, scratch_shapes, args, semantics, name,
                        exchange=None, scatter=()):
    if exchange is None:
        return pl.pallas_call(body, grid=grid, in_specs=in_specs, out_specs=out_specs, out_shape=out_shape,
                              scratch_shapes=scratch_shapes, name=name, compiler_params=_cparams(semantics))(*args)
    n, n_in, n_out, n_scr = len(exchange), len(in_specs), len(out_specs), len(scratch_shapes)

    def wrapped(*refs):
        ins, refs = refs[:n_in], refs[n_in:]
        c_ins, refs = refs[:n], refs[n:]
        outs, refs = refs[:n_out], refs[n_out:]
        c_outs, refs = refs[:n], refs[n:]
        scr, sems = refs[:n_scr], refs[n_scr:]
        ids = [pl.program_id(a) for a in range(len(grid))]
        first = functools.reduce(jnp.logical_and, [i == 0 for i in ids])
        last = functools.reduce(jnp.logical_and, [i == g - 1 for i, g in zip(ids, grid)])

        @pl.when(first)
        def _():
            for cp in _exchange_copies(c_ins, c_outs, *sems, scatter):
                cp.start()

        body(*ins, *outs, *scr)

        @pl.when(last)
        def _():
            for cp in _exchange_copies(c_ins, c_outs, *sems, scatter):
                cp.wait()

    shapes = [tuple(b.shape[1:]) if sc else tuple(b.shape) for b, sc in zip(exchange, scatter)]
    hbm = pl.BlockSpec(memory_space=pl.ANY)
    res = pl.pallas_call(
        wrapped, grid=grid, in_specs=list(in_specs) + [hbm] * n, out_specs=list(out_specs) + [hbm] * n,
        out_shape=list(out_shape) + [jax.ShapeDtypeStruct((N_DEV,) + s, b.dtype) for s, b in zip(shapes, exchange)],
        scratch_shapes=list(scratch_shapes) + [pltpu.SemaphoreType.DMA((n, N_DEV)), pltpu.SemaphoreType.DMA((n, N_DEV)),
                                              pltpu.SemaphoreType.DMA((n,))],
        name=name, compiler_params=pltpu.CompilerParams(dimension_semantics=("arbitrary",) * len(grid),
                                                        vmem_limit_bytes=VMEM_LIMIT, has_side_effects=True),
    )(*args, *exchange)
    return res


def _attn_fwd(q, k, v, name, exchange=None):
    s = q.shape[0]
    tq = min(2048, s)
    tk = min(1024, s)
    n_k = s // tk

    def body(q_ref, k_ref, v_ref, o_ref, lse_ref, m_ref, acc_ref):
        m_ref[...] = jnp.full_like(m_ref, -jnp.inf)
        acc_ref[...] = jnp.zeros_like(acc_ref)
        qv = q_ref[...]

        def step(j, _):
            rows = pl.ds(pl.multiple_of(j * tk, tk), tk)
            sc = _dot_nt(qv, k_ref[rows, :])
            m_old = m_ref[...]
            m_new = jnp.maximum(m_old, jnp.max(sc, axis=1, keepdims=True))
            p = jnp.exp(sc - m_new)
            alpha = jnp.exp(m_old - m_new)
            acc_ref[...] = alpha * acc_ref[...] + _dot_nn(p, v_ref[rows, :])
            m_ref[...] = m_new
            return 0

        lax.fori_loop(0, n_k, step, 0, unroll=min(8, n_k))
        acc = acc_ref[...]
        l = acc[:, V_HEAD:V_HEAD + 1]
        o_ref[...] = acc / l
        lse = m_ref[...] + jnp.log(l)
        lse_ref[0] = jnp.broadcast_to(lse, (tq, LANES)).T[:SUBLANES, :]

    return _call_with_exchange(
        body, grid=(MLA_HEADS, s // tq),
        in_specs=[pl.BlockSpec((tq, HEAD_PAD), lambda h, i: (i, h)),
                  pl.BlockSpec((s, HEAD_PAD), lambda h, i: (0, h)),
                  pl.BlockSpec((s, HEAD_PAD), lambda h, i: (0, h))],
        out_specs=[pl.BlockSpec((tq, HEAD_PAD), lambda h, i: (i, h)),
                   pl.BlockSpec((1, SUBLANES, tq), lambda h, i: (h, 0, i))],
        out_shape=[jax.ShapeDtypeStruct((s, MLA_HEADS * HEAD_PAD), F32),
                   jax.ShapeDtypeStruct((MLA_HEADS, SUBLANES, s), F32)],
        scratch_shapes=[pltpu.VMEM((tq, 1), F32), pltpu.VMEM((tq, HEAD_PAD), F32)],
        args=(q, k, v), semantics=("parallel", "parallel"), name=name, exchange=exchange,
        scatter=(False,) * len(exchange or ()))


def _attn_bwd(q, k, v, o, do, lse, name, exchange=None, scatter=()):
    s = q.shape[0]
    tq = min(2048, s)
    tk = min(2048, s)

    def body(q_ref, k_ref, v_ref, o_ref, do_ref, lse_ref, dq_ref, dk_ref, dv_ref):
        j, i = pl.program_id(1), pl.program_id(2)

        @pl.when(jnp.logical_and(j == 0, i == 0))
        def _():
            dq_ref[...] = jnp.zeros_like(dq_ref)

        @pl.when(i == 0)
        def _():
            dk_ref[...] = jnp.zeros_like(dk_ref)
            dv_ref[...] = jnp.zeros_like(dv_ref)

        qv, kv, vv, dov = q_ref[...], k_ref[...], v_ref[...], do_ref[...]
        delta_col = jnp.sum(dov * o_ref[...], axis=1, keepdims=True)
        delta = jnp.broadcast_to(delta_col, (tq, LANES)).T[:1, :]
        st = _dot_nt(kv, qv)
        pt = jnp.exp(st - lse_ref[0, :1, :])
        dv_ref[...] += _dot_nn(pt, dov)
        dpt = _dot_nt(vv, dov)
        dst = pt * (dpt - delta)
        dk_ref[...] += _dot_nn(dst, qv)
        rows = pl.ds(pl.multiple_of(i * tq, tq), tq)
        dq_ref[rows, :] += _dot_tn(dst, kv)

    return _call_with_exchange(
        body, grid=(MLA_HEADS, s // tk, s // tq),
        in_specs=[pl.BlockSpec((tq, HEAD_PAD), lambda h, j, i: (i, h)),
                  pl.BlockSpec((tk, HEAD_PAD), lambda h, j, i: (j, h)),
                  pl.BlockSpec((tk, HEAD_PAD), lambda h, j, i: (j, h)),
                  pl.BlockSpec((tq, HEAD_PAD), lambda h, j, i: (i, h)),
                  pl.BlockSpec((tq, HEAD_PAD), lambda h, j, i: (i, h)),
                  pl.BlockSpec((1, SUBLANES, tq), lambda h, j, i: (h, 0, i))],
        out_specs=[pl.BlockSpec((s, HEAD_PAD), lambda h, j, i: (0, h)),
                   pl.BlockSpec((tk, HEAD_PAD), lambda h, j, i: (j, h)),
                   pl.BlockSpec((tk, HEAD_PAD), lambda h, j, i: (j, h))],
        out_shape=[jax.ShapeDtypeStruct((s, MLA_HEADS * HEAD_PAD), F32)] * 3, scratch_shapes=[],
        args=(q, k, v, o, do, lse), semantics=("parallel", "arbitrary", "arbitrary"), name=name,
        exchange=exchange, scatter=scatter)


def _pad_w_in(w):
    z = functools.partial(jnp.zeros, dtype=w.dtype)
    return jnp.concatenate([
        w[:, IN_GATE:IN_COLS], w[:, IN_U:IN_CQ], w[:, IN_CKV:IN_KR],
        z((D_MODEL, QK_NOPE)), w[:, IN_KR:IN_GATE], z((D_MODEL, HEAD_PAD - QK_HEAD)),
        z((D_MODEL, P_CQ - P_KR - HEAD_PAD)), w[:, IN_CQ:IN_CKV], z((D_MODEL, P_COLS - P_CQ - Q_LORA))], axis=1)


def _unpad_w_in(d):
    return jnp.concatenate([d[:, P_U:P_CKV], d[:, P_CQ:P_CQ + Q_LORA], d[:, P_CKV:P_KR],
                            d[:, P_KR + QK_NOPE:P_KR + QK_HEAD], d[:, P_GATE:P_U]], axis=1)


def _pad_heads_cols(w, real):
    k = w.shape[0]
    w = w.reshape(k, MLA_HEADS, real)
    return jnp.pad(w, ((0, 0), (0, 0), (0, HEAD_PAD - real))).reshape(k, MLA_HEADS * HEAD_PAD)


def _unpad_heads_cols(d, real):
    k = d.shape[0]
    return d.reshape(k, MLA_HEADS, HEAD_PAD)[:, :, :real].reshape(k, MLA_HEADS * real)


def _pad_kv(w):
    w = w.reshape(KV_LORA, MLA_HEADS, QK_NOPE + V_HEAD)
    kn = jnp.pad(w[:, :, :QK_NOPE], ((0, 0), (0, 0), (0, HEAD_PAD - QK_NOPE)))
    vv = jnp.pad(w[:, :, QK_NOPE:], ((0, 0), (0, 0), (0, HEAD_PAD - V_HEAD)))
    return jnp.concatenate([kn.reshape(KV_LORA, -1), vv.reshape(KV_LORA, -1)], axis=1)


def _unpad_kv(d):
    n = MLA_HEADS * HEAD_PAD
    kn = d[:, :n].reshape(KV_LORA, MLA_HEADS, HEAD_PAD)[:, :, :QK_NOPE]
    vv = d[:, n:].reshape(KV_LORA, MLA_HEADS, HEAD_PAD)[:, :, :V_HEAD]
    return jnp.concatenate([kn, vv], axis=2).reshape(KV_LORA, MLA_HEADS * (QK_NOPE + V_HEAD))


def _pad_out_mla(w):
    w = w.reshape(MLA_HEADS, V_HEAD, D_MODEL)
    return jnp.pad(w, ((0, 0), (0, HEAD_PAD - V_HEAD), (0, 0))).reshape(MLA_HEADS * HEAD_PAD, D_MODEL)


def _unpad_out_mla(d):
    return d.reshape(MLA_HEADS, HEAD_PAD, D_MODEL)[:, :V_HEAD, :].reshape(MLA_HEADS * V_HEAD, D_MODEL)


def _rope_tables(seq):
    half = QK_ROPE // 2
    inv_freq = ROPE_THETA ** (-jnp.arange(half, dtype=F32) / half)
    ang = jnp.arange(seq, dtype=F32)[:, None] * inv_freq[None, :]
    cos, sin = jnp.cos(ang), jnp.sin(ang)
    one, zero = jnp.ones((seq, QK_NOPE), F32), jnp.zeros((seq, half), F32)
    tail1, tail0 = jnp.ones((seq, HEAD_PAD - QK_HEAD), F32), jnp.zeros((seq, HEAD_PAD - QK_HEAD), F32)
    cf = jnp.concatenate([one, cos, cos, tail1], axis=1)
    sa = jnp.concatenate([0.0 * one, -sin, zero, tail0], axis=1)
    sb = jnp.concatenate([0.0 * one, zero, sin, tail0], axis=1)
    return cf, sa, sb


def _prep_layer(w):
    p = {}
    p["w_in_p"] = _pad_w_in(w["w_in"])
    p["w_glu"] = w["w_glu"]
    p["w_out_ssm"] = w["w_out_ssm"]
    p["w_q_p"] = _pad_heads_cols(w["w_q_up"], QK_HEAD)
    p["w_kv_p"] = _pad_kv(w["w_kv_up"])
    p["w_out_mla_p"] = _pad_out_mla(w["w_out_mla"])
    p["w_o"] = w["w_o"]
    p["w_ff1"] = w["w_ff1"]
    p["w_ff2"] = w["w_ff2"]
    p["mix_g"] = w["mix_norm_g"].reshape(1, D_MODEL)
    p["ffn_g"] = w["ffn_norm_g"].reshape(1, D_MODEL)
    p["b_gate"] = w["b_gate"]
    p["b_glu"] = w["b_glu"].reshape(1, SSM_WIDTH)
    p["d"] = w["ssm_d"].reshape(1, SSM_WIDTH)
    p["q_g"] = w["q_norm_g"].reshape(1, Q_LORA)
    p["kv_g"] = w["kv_norm_g"].reshape(1, KV_LORA)
    p["qh_g"] = jnp.pad(w["q_head_g"], (0, HEAD_PAD - QK_HEAD)).reshape(1, HEAD_PAD)
    p["kh_g"] = jnp.pad(w["k_head_g"], (0, HEAD_PAD - QK_HEAD)).reshape(1, HEAD_PAD)
    p["c_blk"] = _c_block(w["ssm_c_re"], w["ssm_c_im"]).astype(BF16)
    zoh, p["zoh_vjp"] = [], []
    for dr in range(2):
        out, vjp = jax.vjp(_zoh, w["ssm_lam_re"][dr], w["ssm_lam_im"][dr], w["ssm_log_step"][dr],
                           w["ssm_b_re"][dr], w["ssm_b_im"][dr])
        zoh.append(out)
        p["zoh_vjp"].append(vjp)
    p["b_blk"] = [_b_block(z[2], z[3]).astype(BF16) for z in zoh]
    p["coef_fwd"] = [_scan_tables(zoh[0][0], zoh[0][1], False), _scan_tables(zoh[1][0], zoh[1][1], True)]
    p["coef_adj"] = [_scan_tables(zoh[0][0], -zoh[0][1], True), _scan_tables(zoh[1][0], -zoh[1][1], False)]
    return p


def _head_prep_fwd(q_raw, kv_raw, proj, tabs, p, li):
    cf, sa, sb = tabs
    scale = QK_HEAD ** -0.5

    def fn(qr, kn, vv, kr, cfv, sav, sbv, gq, gk):
        qo, ko = [], []
        for h in range(MLA_HEADS):
            sl = slice(h * HEAD_PAD, (h + 1) * HEAD_PAD)
            qo.append(_rope(_rms(qr[:, sl], gq, QK_HEAD), cfv, sav, sbv) * scale)
            ko.append(_rope(_rms(kn[:, sl] + kr, gk, QK_HEAD), cfv, sav, sbv))
        lane = lax.broadcasted_iota(jnp.int32, vv.shape, 1)
        return jnp.concatenate(qo, axis=1), jnp.concatenate(ko, axis=1), jnp.where(lane % HEAD_PAD == V_HEAD, 1.0, vv)

    n = MLA_HEADS * HEAD_PAD
    return _rowwise(fn, [(q_raw, n, 0), (kv_raw, n, 0), (kv_raw, n, 1), (proj, HEAD_PAD, P_KR // HEAD_PAD),
                         (cf, HEAD_PAD, 0), (sa, HEAD_PAD, 0), (sb, HEAD_PAD, 0)], [p["qh_g"], p["kh_g"]],
                    [(n, BF16), (n, BF16), (n, BF16)], tm=256, name=f"head_prep_fwd_{li}")


def _head_prep_bwd(dq, dk, dv, q_raw, kv_raw, proj, tabs, p, li):
    cf, sa, sb = tabs
    scale = QK_HEAD ** -0.5

    def fn(dqv, dkv, dvv, qr, kn, kr, cfv, sav, sbv, gq, gk):
        dqo, dko = [], []
        dkr = jnp.zeros_like(kr)
        dgq = jnp.zeros((1, HEAD_PAD), F32)
        dgk = jnp.zeros((1, HEAD_PAD), F32)
        for h in range(MLA_HEADS):
            sl = slice(h * HEAD_PAD, (h + 1) * HEAD_PAD)
            dx, dg = _rms_bwd(qr[:, sl], gq, _rope_t(dqv[:, sl] * scale, cfv, sav, sbv), QK_HEAD)
            dqo.append(dx)
            dgq = dgq + dg
            dx, dg = _rms_bwd(kn[:, sl] + kr, gk, _rope_t(dkv[:, sl], cfv, sav, sbv), QK_HEAD)
            dko.append(dx)
            dkr = dkr + dx
            dgk = dgk + dg
        return jnp.concatenate(dqo, axis=1), jnp.concatenate(dko + [dvv], axis=1), dkr, dgq, dgk

    n = MLA_HEADS * HEAD_PAD
    return _rowwise(fn, [(dq, n, 0), (dk, n, 0), (dv, n, 0), (q_raw, n, 0), (kv_raw, n, 0),
                         (proj, HEAD_PAD, P_KR // HEAD_PAD), (cf, HEAD_PAD, 0), (sa, HEAD_PAD, 0), (sb, HEAD_PAD, 0)],
                    [p["qh_g"], p["kh_g"]], [(n, BF16), (2 * n, BF16), (HEAD_PAD, BF16)],
                    [(HEAD_PAD, F32), (HEAD_PAD, F32)], tm=256, name=f"head_prep_bwd_{li}")


def _layer_fwd(x, p, tabs, li, exchange=None):
    sv = {"x": x}
    h = _rowwise(lambda xv, g: _rms(xv, g, D_MODEL), [(x, D_MODEL, 0)], [p["mix_g"]], [(D_MODEL, BF16)],
                 name=f"mix_norm_{li}")[0]
    proj = _mm_nn(h, p["w_in_p"], name=f"in_proj_{li}")
    sv["h"], sv["proj"] = h, proj
    y_f, xr_f, xi_f = _s5_scan_fwd(proj, p["b_blk"][0], p["c_blk"], p["coef_fwd"][0], False, f"s5_fwd_f_{li}")
    y_b, xr_b, xi_b = _s5_scan_fwd(proj, p["b_blk"][1], p["c_blk"], p["coef_fwd"][1], True, f"s5_fwd_b_{li}")
    sv["states"] = [(xr_f, xi_f), (xr_b, xi_b)]
    y_raw, yg = _rowwise(lambda a, b, u, d: (a + b + d * u, _gelu(a + b + d * u)),
                         [(y_f, SSM_WIDTH, 0), (y_b, SSM_WIDTH, 0), (proj, SSM_WIDTH, P_U // SSM_WIDTH)], [p["d"]],
                         [(SSM_WIDTH, F32), (SSM_WIDTH, BF16)], name=f"s5_gelu_{li}")
    z = _mm_nn(yg, p["w_glu"], name=f"glu_proj_{li}")
    y_ssm = _rowwise(lambda yr, zv, b: _gelu(yr) * _sigmoid(zv + b), [(y_raw, SSM_WIDTH, 0), (z, SSM_WIDTH, 0)],
                     [p["b_glu"]], [(SSM_WIDTH, BF16)], name=f"glu_{li}")[0]
    sv.update(y_raw=y_raw, yg=yg, z=z, y_ssm=y_ssm)
    cqn, ckvn = _rowwise(lambda cq, ckv, gq, gkv: (_rms(cq, gq, Q_LORA), _rms(ckv, gkv, KV_LORA)),
                         [(proj, Q_LORA, P_CQ // Q_LORA), (proj, KV_LORA, P_CKV // KV_LORA)], [p["q_g"], p["kv_g"]],
                         [(Q_LORA, BF16), (KV_LORA, BF16)], name=f"lora_norm_{li}")
    q_raw = _mm_nn(cqn, p["w_q_p"], name=f"q_up_{li}")
    kv_raw = _mm_nn(ckvn, p["w_kv_p"], name=f"kv_up_{li}")
    q, k, v = _head_prep_fwd(q_raw, kv_raw, proj, tabs, p, li)
    o, lse, *gathered = _attn_fwd(q, k, v, f"attn_fwd_{li}", exchange)
    sv.update(cqn=cqn, ckvn=ckvn, q_raw=q_raw, kv_raw=kv_raw, q=q, k=k, v=v, o=o, lse=lse)
    t_ssm = _mm_nn(y_ssm, p["w_out_ssm"], name=f"out_ssm_{li}")
    t_mla = _mm_nn(o, p["w_out_mla_p"], name=f"out_mla_{li}")
    merged = _rowwise(lambda g0, g1, ts, tmv, b: _sigmoid(g0 + b[0:1]) * ts + _sigmoid(g1 + b[1:2]) * tmv,
                      [(proj, D_MODEL, 0), (proj, D_MODEL, 1), (t_ssm, D_MODEL, 0), (t_mla, D_MODEL, 0)],
                      [p["b_gate"]], [(D_MODEL, BF16)], name=f"merge_{li}")[0]
    x1 = _mm_nn(merged, p["w_o"], add=x, name=f"o_proj_{li}")
    sv.update(t_ssm=t_ssm, t_mla=t_mla, merged=merged, x1=x1)
    h2 = _rowwise(lambda xv, g: _rms(xv, g, D_MODEL), [(x1, D_MODEL, 0)], [p["ffn_g"]], [(D_MODEL, BF16)],
                  name=f"ffn_norm_{li}")[0]
    a, r = _mm_nn(h2, p["w_ff1"], epilogue=lambda acc: (acc, jnp.square(jnp.maximum(acc, 0.0))),
                  out_dtypes=(F32, BF16), name=f"ff1_{li}")
    x2 = _mm_nn(r, p["w_ff2"], add=x1, name=f"ff2_{li}")
    sv.update(h2=h2, a=a, r=r)
    return x2, sv, gathered


def _layer_bwd(dx2, dx2_b, sv, p, tabs, li, pending=None, send_early=None):
    g = {}
    da = _mm_nt(dx2_b, p["w_ff2"], extras=(sv["a"],), epilogue=lambda acc, av: acc * (2.0 * jnp.maximum(av, 0.0)),
                out_dtypes=(BF16,), name=f"d_ff2_x_{li}")
    g["w_ff2"] = _mm_tn(sv["r"], dx2_b, name=f"d_ff2_w_{li}")
    dh2 = _mm_nt(da, p["w_ff1"], name=f"d_ff1_x_{li}")
    g["w_ff1"] = _mm_tn(sv["h2"], da, name=f"d_ff1_w_{li}")

    def norm_bwd(xv, dyv, dres, gg):
        dx, dg = _rms_bwd(xv, gg, dyv, D_MODEL)
        return dres + dx, dres + dx, dg

    dx1, dx1_b, dg = _rowwise(norm_bwd, [(sv["x1"], D_MODEL, 0), (dh2, D_MODEL, 0), (dx2, D_MODEL, 0)], [p["ffn_g"]],
                              [(D_MODEL, F32), (D_MODEL, BF16)], [(D_MODEL, F32)], name=f"d_ffn_norm_{li}")
    g["ffn_norm_g"] = dg.reshape(D_MODEL)
    dmerged = _mm_nt(dx1_b, p["w_o"], name=f"d_o_x_{li}")
    g["w_o"] = _mm_tn(sv["merged"], dx1_b, name=f"d_o_w_{li}")

    def merge_bwd(dm, g0, g1, ts, tmv, b):
        s0, s1 = _sigmoid(g0 + b[0:1]), _sigmoid(g1 + b[1:2])
        dg0, dg1 = dm * ts * s0 * (1.0 - s0), dm * tmv * s1 * (1.0 - s1)
        return (dm * s0, dm * s1, jnp.concatenate([dg0, dg1], axis=1),
                jnp.sum(dg0, axis=0, keepdims=True), jnp.sum(dg1, axis=0, keepdims=True))

    proj = sv["proj"]
    dt_ssm, dt_mla, dgate, db0, db1 = _rowwise(
        merge_bwd, [(dmerged, D_MODEL, 0), (proj, D_MODEL, 0), (proj, D_MODEL, 1), (sv["t_ssm"], D_MODEL, 0),
                    (sv["t_mla"], D_MODEL, 0)], [p["b_gate"]],
        [(D_MODEL, BF16), (D_MODEL, BF16), (2 * D_MODEL, BF16)], [(D_MODEL, F32), (D_MODEL, F32)], tm=256,
        name=f"d_merge_{li}")
    g["b_gate"] = jnp.concatenate([db0, db1], axis=0)
    dy_ssm = _mm_nt(dt_ssm, p["w_out_ssm"], name=f"d_out_ssm_x_{li}")
    g["w_out_ssm"] = _mm_tn(sv["y_ssm"], dt_ssm, name=f"d_out_ssm_w_{li}")
    do = _mm_nt(dt_mla, p["w_out_mla_p"], name=f"d_out_mla_x_{li}")
    g["w_out_mla"] = _unpad_out_mla(_mm_tn(sv["o"], dt_mla, name=f"d_out_mla_w_{li}"))

    def glu_bwd(dyv, yr, zv, b):
        yg = _gelu(yr)
        sg = _sigmoid(zv + b)
        dz = dyv * yg * sg * (1.0 - sg)
        return dz, dyv * sg, jnp.sum(dz, axis=0, keepdims=True)

    dz, dyg_direct, dbglu = _rowwise(glu_bwd, [(dy_ssm, SSM_WIDTH, 0), (sv["y_raw"], SSM_WIDTH, 0), (sv["z"], SSM_WIDTH, 0)],
                                     [p["b_glu"]], [(SSM_WIDTH, BF16), (SSM_WIDTH, F32)], [(SSM_WIDTH, F32)],
                                     name=f"d_glu_{li}")
    g["b_glu"] = dbglu.reshape(SSM_WIDTH)
    dyg_mm = _mm_nt(dz, p["w_glu"], name=f"d_glu_x_{li}")
    g["w_glu"] = _mm_tn(sv["yg"], dz, name=f"d_glu_w_{li}")

    def gelu_bwd(d1, d2, yr, u, d):
        dyr = (d1 + d2) * _gelu_grad(yr)
        return dyr, dyr * d, jnp.sum(dyr * u, axis=0, keepdims=True)

    dy_raw, du_d, dd = _rowwise(gelu_bwd, [(dyg_direct, SSM_WIDTH, 0), (dyg_mm, SSM_WIDTH, 0), (sv["y_raw"], SSM_WIDTH, 0),
                                           (proj, SSM_WIDTH, P_U // SSM_WIDTH)], [p["d"]],
                                [(SSM_WIDTH, BF16), (SSM_WIDTH, F32)], [(SSM_WIDTH, F32)], name=f"d_gelu_{li}")
    g["ssm_d"] = dd.reshape(SSM_GROUPS, SSM_GROUP)
    du_parts, dc_sum = [du_d], None
    zoh_grads = []
    for dr_i in range(2):
        xr, xi = sv["states"][dr_i]
        du_i, da_i, db_i, dc_i = _s5_scan_bwd(dy_raw, proj, xr, xi, p["b_blk"][dr_i], p["c_blk"], p["coef_adj"][dr_i],
                                              dr_i == 0, f"s5_bwd_{'fb'[dr_i]}_{li}")
        du_parts.append(du_i)
        dc_sum = dc_i if dc_sum is None else dc_sum + dc_i
        da_i = jnp.sum(da_i, axis=1)
        dar = da_i[:, :CHUNK_STATE].reshape(SSM_GROUPS, SSM_STATE)
        dai = da_i[:, CHUNK_STATE:].reshape(SSM_GROUPS, SSM_STATE)
        dbr, dbi = _b_unblock(db_i)
        zoh_grads.append(p["zoh_vjp"][dr_i]((dar, dai, dbr, dbi)))
    for k_i, nm in enumerate(("ssm_lam_re", "ssm_lam_im", "ssm_log_step", "ssm_b_re", "ssm_b_im")):
        g[nm] = jnp.stack([zoh_grads[0][k_i], zoh_grads[1][k_i]], axis=0)
    g["ssm_c_re"], g["ssm_c_im"] = _c_unblock(dc_sum)
    du = _rowwise(lambda a, b, c: a + b + c, [(d_, SSM_WIDTH, 0) for d_ in du_parts], [], [(SSM_WIDTH, BF16)],
                  name=f"d_u_sum_{li}")[0]
    bufs, flags = pending if pending is not None else ([], [])
    n_pending = len(bufs)
    if send_early is not None:
        early_bufs, early_flags = send_early(g)
        bufs, flags = list(bufs) + list(early_bufs), list(flags) + list(early_flags)
    dq, dk, dv, *got = _attn_bwd(sv["q"], sv["k"], sv["v"], sv["o"], do, sv["lse"], f"attn_bwd_{li}",
                                 bufs or None, tuple(flags))
    dq_raw, dkv_raw, dkr, dgq, dgk = _head_prep_bwd(dq, dk, dv, sv["q_raw"], sv["kv_raw"], proj, tabs, p, li)
    g["q_head_g"] = dgq.reshape(HEAD_PAD)[:QK_HEAD]
    g["k_head_g"] = dgk.reshape(HEAD_PAD)[:QK_HEAD]
    dcqn = _mm_nt(dq_raw, p["w_q_p"], name=f"d_q_up_x_{li}")
    g["w_q_up"] = _unpad_heads_cols(_mm_tn(sv["cqn"], dq_raw, name=f"d_q_up_w_{li}"), QK_HEAD)
    dckvn = _mm_nt(dkv_raw, p["w_kv_p"], name=f"d_kv_up_x_{li}")
    g["w_kv_up"] = _unpad_kv(_mm_tn(sv["ckvn"], dkv_raw, name=f"d_kv_up_w_{li}"))

    def lora_bwd(cq, ckv, d1, d2, gq, gkv):
        dx1_, dg1 = _rms_bwd(cq, gq, d1, Q_LORA)
        dx2_, dg2 = _rms_bwd(ckv, gkv, d2, KV_LORA)
        return dx1_, dx2_, dg1, dg2

    dcq, dckv, dgqn, dgkvn = _rowwise(
        lora_bwd, [(proj, Q_LORA, P_CQ // Q_LORA), (proj, KV_LORA, P_CKV // KV_LORA), (dcqn, Q_LORA, 0), (dckvn, KV_LORA, 0)],
        [p["q_g"], p["kv_g"]], [(Q_LORA, BF16), (KV_LORA, BF16)], [(Q_LORA, F32), (KV_LORA, F32)], name=f"d_lora_norm_{li}")
    g["q_norm_g"], g["kv_norm_g"] = dgqn.reshape(Q_LORA), dgkvn.reshape(KV_LORA)
    gap = jnp.zeros((dx2.shape[0], P_CQ - P_KR - HEAD_PAD), BF16)
    tail = jnp.zeros((dx2.shape[0], P_COLS - P_CQ - Q_LORA), BF16)
    dproj = jnp.concatenate([dgate, du, dckv, dkr, gap, dcq, tail], axis=1)
    dh = _mm_nt(dproj, p["w_in_p"], name=f"d_in_x_{li}")
    g["w_in"] = _unpad_w_in(_mm_tn(sv["h"], dproj, name=f"d_in_w_{li}"))
    dx, dx_b, dg = _rowwise(norm_bwd, [(sv["x"], D_MODEL, 0), (dh, D_MODEL, 0), (dx1, D_MODEL, 0)], [p["mix_g"]],
                            [(D_MODEL, F32), (D_MODEL, BF16)], [(D_MODEL, F32)], name=f"d_mix_norm_{li}")
    g["mix_norm_g"] = dg.reshape(D_MODEL)
    return dx, dx_b, g, got[:n_pending], got[n_pending:]


def _local_step(x, target, layer_weights, send_weights=None, send_early=None, send_late=None):
    tabs = _rope_tables(x.shape[0])
    saved, preps = [], []
    gathered = None
    for li in range(DEPTH):
        p = _prep_layer(layer_weights(li, gathered))
        nxt = send_weights(li + 1) if send_weights is not None and li + 1 < DEPTH else None
        x, sv, gathered = _layer_fwd(x, p, tabs, li, nxt)
        saved.append(sv)
        preps.append(p)

    def loss_fn(y, t):
        err = y - t
        d = err * (1.0 / D_MODEL)
        return d, d, jnp.sum(jnp.sum(err * err, axis=1, keepdims=True), axis=0, keepdims=True) * jnp.ones((1, LANES), F32)

    dx, dx_b, lsum = _rowwise(loss_fn, [(x, D_MODEL, 0), (target, D_MODEL, 0)], [], [(D_MODEL, F32), (D_MODEL, BF16)],
                              [(LANES, F32)], name="loss")
    loss = 0.5 * lsum[0, 0] * (1.0 / D_MODEL)
    grads, early, late = [None] * DEPTH, [None] * DEPTH, [None] * DEPTH
    pending = None
    for li in reversed(range(DEPTH)):
        dx, dx_b, grads[li], got_late, early[li] = _layer_bwd(dx, dx_b, saved[li], preps[li], tabs, li, pending, send_early)
        if pending is not None:
            late[li + 1] = got_late
        pending = send_late(grads[li]) if send_late is not None else None
    return loss, dx, grads, early, late, pending


def _exchange(bufs, scatter, name):
    n = len(bufs)

    def body(*refs):
        copies = _exchange_copies(refs[:n], refs[n:2 * n], *refs[2 * n:], scatter)
        for cp in copies:
            cp.start()
        for cp in copies:
            cp.wait()

    shapes = [tuple(b.shape[1:]) if sc else tuple(b.shape) for b, sc in zip(bufs, scatter)]
    return pl.pallas_call(
        body, out_shape=[jax.ShapeDtypeStruct((N_DEV,) + s, b.dtype) for s, b in zip(shapes, bufs)],
        in_specs=[pl.BlockSpec(memory_space=pl.ANY)] * n, out_specs=[pl.BlockSpec(memory_space=pl.ANY)] * n,
        scratch_shapes=[pltpu.SemaphoreType.DMA((n, N_DEV)), pltpu.SemaphoreType.DMA((n, N_DEV)),
                        pltpu.SemaphoreType.DMA((n,))],
        name=name, compiler_params=pltpu.CompilerParams(has_side_effects=True),
    )(*bufs)


def _adamw(parts, w, m, v, name):
    shape = w.shape
    cols = shape[-1]
    r = math.prod(shape[:-1])
    parts, w, m, v = parts.reshape(N_DEV, r, cols), w.reshape(r, cols), m.reshape(r, cols), v.reshape(r, cols)
    tm = _pick_rows(r, cols)

    def body(p_ref, w_ref, m_ref, v_ref, g_ref, d_ref, nm_ref, nv_ref):
        g = p_ref[0].astype(F32)
        for j in range(1, N_DEV):
            g = g + p_ref[j].astype(F32)
        m_new = ADAM_B1 * m_ref[...] + (1.0 - ADAM_B1) * g
        v_new = ADAM_B2 * v_ref[...] + (1.0 - ADAM_B2) * (g * g)
        m_hat = m_new / (1.0 - ADAM_B1 ** ADAM_STEP)
        v_hat = v_new / (1.0 - ADAM_B2 ** ADAM_STEP)
        g_ref[...] = g
        d_ref[...] = -ADAM_LR * (m_hat / (jnp.sqrt(v_hat) + ADAM_EPS) + ADAM_WD * w_ref[...])
        nm_ref[...] = m_new
        nv_ref[...] = v_new

    spec = pl.BlockSpec((tm, cols), lambda i: (i, 0))
    res = pl.pallas_call(
        body, grid=(r // tm,), in_specs=[pl.BlockSpec((N_DEV, tm, cols), lambda i: (0, i, 0)), spec, spec, spec],
        out_specs=[spec] * 4, out_shape=[jax.ShapeDtypeStruct((r, cols), F32)] * 4, name=name,
        compiler_params=_cparams(("parallel",)),
    )(parts, w, m, v)
    return [a.reshape(shape) for a in res]


def _pick_rows(r, cols):
    for t in (512, 256, 128, 64, 32, 16):
        if r % t == 0 and t * cols <= 512 * 512:
            return t
    return r


def _pack(arrs, dtype, row_mult):
    flat = jnp.concatenate([a.reshape(-1).astype(dtype) for a in arrs])
    n = flat.shape[0]
    per = row_mult * D_MODEL
    total = -(-n // per) * per
    return jnp.pad(flat, (0, total - n)).reshape(total // D_MODEL, D_MODEL)


def _unpack(flat, shapes):
    lead = flat.shape[:-2]
    flat = flat.reshape(lead + (-1,))
    out, off = [], 0
    for shp in shapes:
        n = math.prod(shp)
        out.append(flat[..., off:off + n].reshape(lead + tuple(shp)))
        off += n
    return out


def _to_shards(gfull, axis):
    shp = gfull.shape
    gfull = gfull.reshape(shp[:axis] + (N_DEV, shp[axis] // N_DEV) + shp[axis + 1:])
    return jnp.moveaxis(gfull, axis, 0)


def _from_shards(parts, axis):
    parts = jnp.moveaxis(parts, 0, axis)
    shp = parts.shape
    return parts.reshape(shp[:axis] + (shp[axis] * shp[axis + 1],) + shp[axis + 2:])


def kernel(x, mix_norm_g, w_in, b_gate, ssm_lam_re, ssm_lam_im, ssm_log_step, ssm_b_re, ssm_b_im, ssm_c_re, ssm_c_im, ssm_d, w_glu, b_glu, w_out_ssm, q_norm_g, kv_norm_g, w_q_up, w_kv_up, q_head_g, k_head_g, w_out_mla, w_o, ffn_norm_g, w_ff1, w_ff2, loss_target, m_mix_norm_g, m_w_in, m_b_gate, m_ssm_lam_re, m_ssm_lam_im, m_ssm_log_step, m_ssm_b_re, m_ssm_b_im, m_ssm_c_re, m_ssm_c_im, m_ssm_d, m_w_glu, m_b_glu, m_w_out_ssm, m_q_norm_g, m_kv_norm_g, m_w_q_up, m_w_kv_up, m_q_head_g, m_k_head_g, m_w_out_mla, m_w_o, m_ffn_norm_g, m_w_ff1, m_w_ff2, v_mix_norm_g, v_w_in, v_b_gate, v_ssm_lam_re, v_ssm_lam_im, v_ssm_log_step, v_ssm_b_re, v_ssm_b_im, v_ssm_c_re, v_ssm_c_im, v_ssm_d, v_w_glu, v_b_glu, v_w_out_ssm, v_q_norm_g, v_kv_norm_g, v_w_q_up, v_w_kv_up, v_q_head_g, v_k_head_g, v_w_out_mla, v_w_o, v_ffn_norm_g, v_w_ff1, v_w_ff2):
    args = dict(locals())
    w = {n: args[n] for n in WEIGHTS}
    m = {n: args["m_" + n] for n in WEIGHTS}
    v = {n: args["v_" + n] for n in WEIGHTS}

    def send_weights(li):
        return [w[n][li] if n == "b_gate" else w[n][li].astype(BF16) for n in SHARDED]

    first = _exchange(send_weights(0), [False] * len(SHARDED), "weight_all_gather_0")

    def layer_weights(li, gathered):
        full = {n: _from_shards(pt, SHARD_AXIS[n] - 1) for n, pt in zip(SHARDED, first if li == 0 else gathered)}
        for n in REPLICATED:
            full[n] = w[n][li]
        return full

    def shards(g, names):
        return [_to_shards(g[n], SHARD_AXIS[n] - 1).astype(BF16) for n in names]

    def send_early(g):
        return shards(g, EARLY), [True] * len(EARLY)

    def send_late(g):
        return shards(g, LATE) + [_pack([g[n] for n in REPLICATED], F32, 8)], [True] * len(LATE) + [False]

    loss_part, dx, grads, early, late, pending = _local_step(x[0], loss_target[0], layer_weights, send_weights,
                                                             send_early, send_late)
    late[0] = _exchange(*pending, "grad_exchange_0")
    loss = lax.psum(loss_part, ("x", "y", "c"))

    outs = {}
    for n in SHARDED:
        src, t = (early, EARLY.index(n)) if n in EARLY else (late, LATE.index(n))
        parts = jnp.stack([src[li][t] for li in range(DEPTH)], axis=1)
        for kind, a in zip(("grad", "delta", "new_m", "new_v"), _adamw(parts, w[n], m[n], v[n], "adamw_" + n)):
            outs[kind + "_" + n] = a
    r_parts = jnp.stack([late[li][len(LATE)] for li in range(DEPTH)], axis=1)

    def pack_layers(d):
        return jnp.stack([_pack([d[n][li] for n in REPLICATED], F32, 8) for li in range(DEPTH)], axis=0)

    res = _adamw(r_parts, pack_layers(w), pack_layers(m), pack_layers(v), "adamw_replicated")
    rep_shapes = [w[n].shape[1:] for n in REPLICATED]
    for kind, flat in zip(("grad", "delta", "new_m", "new_v"), res):
        for n, a in zip(REPLICATED, _unpack(flat, rep_shapes)):
            outs[kind + "_" + n] = a
    return (loss, dx[None], *[outs[k + "_" + n] for k in ("grad", "delta", "new_m", "new_v") for n in WEIGHTS])
```

```python
import functools
import math

import jax
import jax.numpy as jnp
from jax import lax
from jax.experimental import pallas as pl
from jax.experimental.pallas import tpu as pltpu

F32 = jnp.float32
BF16 = jnp.bfloat16
_MXU = jnp.bfloat16

D_MODEL = 1024
DEPTH = 4
SSM_WIDTH = 512
SSM_GROUP = 16
SSM_GROUPS = 32
SSM_STATE = 64
MLA_HEADS = 8
QK_NOPE = 64
QK_ROPE = 32
QK_HEAD = 96
V_HEAD = 64
Q_LORA = 384
KV_LORA = 256
ROPE_THETA = 10000.0
D_FF = 4096
EPS = 1e-6
HEAD_PAD = 128
N_DEV = 8
LANES = 128
SUBLANES = 8
CHUNK_GROUPS = 8
N_CHUNKS = SSM_GROUPS // CHUNK_GROUPS
CHUNK_STATE = CHUNK_GROUPS * SSM_STATE
S5_T_BLK = 2048

P_GATE, P_U, P_CKV, P_KR, P_CQ = 0, 2048, 2560, 2816, 3072
P_COLS = 3584
IN_U, IN_CQ, IN_CKV, IN_KR, IN_GATE = 0, 512, 896, 1152, 1184
IN_COLS = 3232

ADAM_LR = 0.001
ADAM_B1 = 0.9
ADAM_B2 = 0.999
ADAM_EPS = 1e-08
ADAM_WD = 0.01
ADAM_STEP = 10

VMEM_LIMIT = 56 * 1024 * 1024
TN_OUT_BLOCK_BYTES = 8 * 1024 * 1024
MXU_WIDTH = 256
WIDE_BLOCK_MAX_K = 1024

SHARDED = ("w_in", "b_gate", "w_glu", "w_out_ssm", "w_q_up", "w_kv_up", "w_out_mla", "w_o", "w_ff1", "w_ff2")
SHARD_AXIS = {"w_in": 2, "b_gate": 2, "w_glu": 1, "w_out_ssm": 2, "w_q_up": 2, "w_kv_up": 2, "w_out_mla": 2,
              "w_o": 1, "w_ff1": 2, "w_ff2": 1}
EARLY = ("b_gate", "w_glu", "w_out_ssm", "w_out_mla", "w_o", "w_ff1", "w_ff2")
LATE = ("w_in", "w_q_up", "w_kv_up")
REPLICATED = ("mix_norm_g", "ssm_lam_re", "ssm_lam_im", "ssm_log_step", "ssm_b_re", "ssm_b_im", "ssm_c_re",
              "ssm_c_im", "ssm_d", "b_glu", "q_norm_g", "kv_norm_g", "q_head_g", "k_head_g", "ffn_norm_g")
WEIGHTS = ("mix_norm_g", "w_in", "b_gate", "ssm_lam_re", "ssm_lam_im", "ssm_log_step", "ssm_b_re", "ssm_b_im",
           "ssm_c_re", "ssm_c_im", "ssm_d", "w_glu", "b_glu", "w_out_ssm", "q_norm_g", "kv_norm_g", "w_q_up",
           "w_kv_up", "q_head_g", "k_head_g", "w_out_mla", "w_o", "ffn_norm_g", "w_ff1", "w_ff2")


def _cparams(sem):
    return pltpu.CompilerParams(dimension_semantics=sem, vmem_limit_bytes=VMEM_LIMIT)


def _dot(a, b, dims):
    return lax.dot_general(a.astype(_MXU), b.astype(_MXU), (dims, ((), ())), preferred_element_type=F32)


def _dot_nn(a, b):
    return _dot(a, b, ((1,), (0,)))


def _dot_nt(a, b):
    return _dot(a, b, ((1,), (1,)))


def _dot_tn(a, b):
    return _dot(a, b, ((0,), (0,)))


def _rowwise(fn, rows, consts, outs, accs=(), *, tm=1024, name):
    n_rows = rows[0][0].shape[0]
    tm = min(tm, n_rows)
    n_in = len(rows) + len(consts)
    n_o, n_a = len(outs), len(accs)

    def body(*refs):
        res = fn(*[r[...] for r in refs[:n_in]])
        if not isinstance(res, (tuple, list)):
            res = (res,)
        orefs = refs[n_in:]
        for k in range(n_o):
            orefs[k][...] = res[k].astype(orefs[k].dtype)
        if n_a:
            @pl.when(pl.program_id(0) == 0)
            def _():
                for k in range(n_a):
                    orefs[n_o + k][...] = jnp.zeros_like(orefs[n_o + k])
            for k in range(n_a):
                orefs[n_o + k][...] += res[n_o + k]

    in_specs = [pl.BlockSpec((tm, w), functools.partial(lambda i, j: (i, j), j=j)) for (_, w, j) in rows]
    in_specs += [pl.BlockSpec(c.shape, functools.partial(lambda i, nd: (0,) * nd, nd=c.ndim)) for c in consts]
    out_specs = [pl.BlockSpec((tm, w), lambda i: (i, 0)) for (w, _) in outs]
    out_specs += [pl.BlockSpec((1, w), lambda i: (0, 0)) for (w, _) in accs]
    out_shape = [jax.ShapeDtypeStruct((n_rows, w), dt) for (w, dt) in outs]
    out_shape += [jax.ShapeDtypeStruct((1, w), dt) for (w, dt) in accs]
    res = pl.pallas_call(
        body, grid=(n_rows // tm,), in_specs=in_specs, out_specs=out_specs, out_shape=out_shape, name=name,
        compiler_params=_cparams(("arbitrary",) if n_a else ("parallel",)),
    )(*[r[0] for r in rows], *consts)
    return res


def _pick(n, cap):
    if n <= cap:
        return n
    for unit in (MXU_WIDTH, LANES):
        best = 0
        for t in range(unit, cap + 1, unit):
            if n % t == 0:
                best = t
        if best:
            return best
    return n


def _mm(a, b, transpose_b, extras, epilogue, out_dtypes, name):
    m, k = a.shape
    n = b.shape[0] if transpose_b else b.shape[1]
    wide = k <= WIDE_BLOCK_MAX_K
    tm, tn = min(1024 if wide else 512, m), _pick(n, 2048 if wide else 1024)
    n_in = 2 + len(extras)

    def body(*refs):
        acc = (_dot_nt if transpose_b else _dot_nn)(refs[0][...], refs[1][...])
        res = epilogue(acc, *[r[...] for r in refs[2:n_in]]) if epilogue is not None else acc
        if not isinstance(res, (tuple, list)):
            res = (res,)
        for o_ref, val in zip(refs[n_in:], res):
            o_ref[...] = val.astype(o_ref.dtype)

    blk = pl.BlockSpec((tm, tn), lambda j, i: (i, j))
    b_spec = pl.BlockSpec((tn, k), lambda j, i: (j, 0)) if transpose_b else pl.BlockSpec((k, tn), lambda j, i: (0, j))
    res = pl.pallas_call(
        body, grid=(n // tn, m // tm),
        in_specs=[pl.BlockSpec((tm, k), lambda j, i: (i, 0)), b_spec] + [blk] * len(extras),
        out_specs=[blk] * len(out_dtypes), out_shape=[jax.ShapeDtypeStruct((m, n), dt) for dt in out_dtypes],
        name=name, compiler_params=_cparams(("parallel", "parallel")),
    )(a, b, *extras)
    return res[0] if len(out_dtypes) == 1 else res


def _mm_nn(a, b, *, add=None, extras=(), epilogue=None, out_dtypes=(F32,), name):
    if add is not None:
        extras, epilogue = (add,), (lambda acc, r: acc + r)
    return _mm(a, b, False, tuple(extras), epilogue, out_dtypes, name)


def _mm_nt(a, b, *, extras=(), epilogue=None, out_dtypes=(F32,), name):
    return _mm(a, b, True, tuple(extras), epilogue, out_dtypes, name)


def _mm_tn(a, b, *, a_cols=None, name):
    s = a.shape[0]
    n = b.shape[1]
    mw, mj = (a.shape[1], 0) if a_cols is None else a_cols
    ts = min(1024, s)
    tm = _pick(mw, 1024)
    tn = _pick(n, max(1024, TN_OUT_BLOCK_BYTES // (4 * tm)))
    n_mb = mw // tm

    def body(a_ref, b_ref, o_ref):
        @pl.when(pl.program_id(2) == 0)
        def _():
            o_ref[...] = jnp.zeros_like(o_ref)
        o_ref[...] += _dot_tn(a_ref[...], b_ref[...])

    return pl.pallas_call(
        body, grid=(n_mb, n // tn, s // ts),
        in_specs=[pl.BlockSpec((ts, tm), lambda i, j, t: (t, mj * n_mb + i)), pl.BlockSpec((ts, tn), lambda i, j, t: (t, j))],
        out_specs=pl.BlockSpec((tm, tn), lambda i, j, t: (i, j)),
        out_shape=jax.ShapeDtypeStruct((mw, n), F32), name=name,
        compiler_params=_cparams(("parallel", "parallel", "arbitrary")),
    )(a, b)


def _rms(x, g, n):
    r = lax.rsqrt(jnp.sum(x * x, axis=-1, keepdims=True) * (1.0 / n) + EPS)
    return x * r * g


def _rms_bwd(x, g, dy, n):
    r = lax.rsqrt(jnp.sum(x * x, axis=-1, keepdims=True) * (1.0 / n) + EPS)
    xr = x * r
    dyg = dy * g
    dx = r * dyg - xr * (r * r) * (jnp.sum(dyg * x, axis=-1, keepdims=True) * (1.0 / n))
    return dx, jnp.sum(dy * xr, axis=0, keepdims=True)


def _gelu(x):
    c = math.sqrt(2.0 / math.pi)
    return 0.5 * x * (1.0 + jnp.tanh(c * (x + 0.044715 * (x * x * x))))


def _gelu_grad(x):
    c = math.sqrt(2.0 / math.pi)
    t = jnp.tanh(c * (x + 0.044715 * (x * x * x)))
    return 0.5 * (1.0 + t) + 0.5 * x * (1.0 - t * t) * (c * (1.0 + 3.0 * 0.044715 * (x * x)))


def _sigmoid(x):
    return 1.0 / (1.0 + jnp.exp(-x))


def _rope(x, cf, sa, sb):
    return x * cf + pltpu.roll(x, HEAD_PAD - QK_ROPE // 2, 1) * sa + pltpu.roll(x, QK_ROPE // 2, 1) * sb


def _rope_t(d, cf, sa, sb):
    return d * cf + pltpu.roll(d * sa, QK_ROPE // 2, 1) + pltpu.roll(d * sb, HEAD_PAD - QK_ROPE // 2, 1)


def _scan_tables(ar, ai, reverse):
    ar = ar.reshape(N_CHUNKS, CHUNK_STATE)
    ai = ai.reshape(N_CHUNKS, CHUNK_STATE)
    pr, pi = [ar], [ai]
    for _ in range(SUBLANES - 1):
        pr, pi = pr + [pr[-1] * ar - pi[-1] * ai], pi + [pr[-1] * ai + pi[-1] * ar]
    row = jnp.arange(SUBLANES)[None, :, None]
    tiles = []
    for k in (1, 2, 4):
        mask = (row <= SUBLANES - 1 - k) if reverse else (row >= k)
        tiles.append(jnp.where(mask, pr[k - 1][:, None, :], 0.0))
        tiles.append(jnp.where(mask, pi[k - 1][:, None, :], 0.0))
    order = list(range(SUBLANES))[::-1] if reverse else list(range(SUBLANES))
    tiles.append(jnp.stack([pr[j] for j in order], axis=1))
    tiles.append(jnp.stack([pi[j] for j in order], axis=1))
    return jnp.stack(tiles, axis=1).astype(F32)


def _slab_scan(xr, xi, coef, carry_r, carry_i, reverse):
    for idx, k in enumerate((1, 2, 4)):
        sh = SUBLANES - k if reverse else k
        sr, si = pltpu.roll(xr, sh, 0), pltpu.roll(xi, sh, 0)
        cr, ci = coef[2 * idx], coef[2 * idx + 1]
        xr, xi = xr + cr * sr - ci * si, xi + cr * si + ci * sr
    pr, pi = coef[6], coef[7]
    xr = xr + pr * carry_r - pi * carry_i
    xi = xi + pr * carry_i + pi * carry_r
    return xr, xi


def _s5_scan_fwd(proj, b_blk, c_blk, coef, reverse, name):
    s = proj.shape[0]
    t_blk = min(S5_T_BLK, s)
    n_t = s // t_blk
    n_slab = t_blk // SUBLANES
    last = 0 if reverse else SUBLANES - 1

    def tmap(t):
        return n_t - 1 - t if reverse else t

    def body(u_ref, b_ref, c_ref, coef_ref, y_ref, xr_ref, xi_ref, carry_ref):
        @pl.when(pl.program_id(1) == 0)
        def _():
            carry_ref[...] = jnp.zeros_like(carry_ref)
        bu = _dot_nn(u_ref[...], b_ref[0])
        xr_ref[...] = bu[:, :CHUNK_STATE]
        xi_ref[...] = bu[:, CHUNK_STATE:]
        coef_v = [coef_ref[0, k] for k in range(8)]

        def slab(i, carry):
            sl = (n_slab - 1 - i) if reverse else i
            rows = pl.ds(pl.multiple_of(sl * SUBLANES, SUBLANES), SUBLANES)
            xr, xi = _slab_scan(xr_ref[rows, :], xi_ref[rows, :], coef_v, carry[0], carry[1], reverse)
            xr_ref[rows, :] = xr
            xi_ref[rows, :] = xi
            return (jnp.broadcast_to(xr[last:last + 1, :], xr.shape), jnp.broadcast_to(xi[last:last + 1, :], xi.shape))

        cr, ci = lax.fori_loop(0, n_slab, slab, (carry_ref[0], carry_ref[1]))
        carry_ref[0] = cr
        carry_ref[1] = ci
        y_ref[...] = _dot_nn(xr_ref[...], c_ref[0, :CHUNK_STATE, :]) + _dot_nn(xi_ref[...], c_ref[0, CHUNK_STATE:, :])

    u_blk0 = P_U // LANES
    return pl.pallas_call(
        body, grid=(N_CHUNKS, n_t),
        in_specs=[pl.BlockSpec((t_blk, LANES), lambda c, t: (tmap(t), u_blk0 + c)),
                  pl.BlockSpec((1, LANES, 2 * CHUNK_STATE), lambda c, t: (c, 0, 0)),
                  pl.BlockSpec((1, 2 * CHUNK_STATE, LANES), lambda c, t: (c, 0, 0)),
                  pl.BlockSpec((1, 8, SUBLANES, CHUNK_STATE), lambda c, t: (c, 0, 0, 0))],
        out_specs=[pl.BlockSpec((t_blk, LANES), lambda c, t: (tmap(t), c)),
                   pl.BlockSpec((t_blk, CHUNK_STATE), lambda c, t: (tmap(t), c)),
                   pl.BlockSpec((t_blk, CHUNK_STATE), lambda c, t: (tmap(t), c))],
        out_shape=[jax.ShapeDtypeStruct((s, SSM_WIDTH), F32),
                   jax.ShapeDtypeStruct((s, N_CHUNKS * CHUNK_STATE), F32),
                   jax.ShapeDtypeStruct((s, N_CHUNKS * CHUNK_STATE), F32)],
        scratch_shapes=[pltpu.VMEM((2, SUBLANES, CHUNK_STATE), F32)],
        name=name, compiler_params=_cparams(("parallel", "arbitrary")),
    )(proj, b_blk, c_blk, coef)


def _s5_scan_bwd(dy, proj, x_re, x_im, b_blk, c_blk, coef, reverse, name):
    s = dy.shape[0]
    t_blk = min(S5_T_BLK, s)
    n_t = s // t_blk
    n_slab = t_blk // SUBLANES
    last = 0 if reverse else SUBLANES - 1
    first = SUBLANES - 1 if reverse else 0

    def tmap(t):
        return n_t - 1 - t if reverse else t

    def body(dy_ref, u_ref, xr_ref, xi_ref, b_ref, c_ref, coef_ref, du_ref, da_ref, db_ref, dc_ref,
             carry_ref, lr_ref, li_ref):
        @pl.when(pl.program_id(1) == 0)
        def _():
            carry_ref[...] = jnp.zeros_like(carry_ref)
            da_ref[...] = jnp.zeros_like(da_ref)
            db_ref[...] = jnp.zeros_like(db_ref)
            dc_ref[...] = jnp.zeros_like(dc_ref)
        g = _dot_nt(dy_ref[...], c_ref[0])
        lr_ref[...] = g[:, :CHUNK_STATE]
        li_ref[...] = g[:, CHUNK_STATE:]
        coef_v = [coef_ref[0, k] for k in range(8)]
        row = lax.broadcasted_iota(jnp.int32, (SUBLANES, CHUNK_STATE), 0)
        sh_prev = SUBLANES - 1 if reverse else 1

        def slab(i, carry):
            cr, ci, ar_acc, ai_acc = carry
            sl = (n_slab - 1 - i) if reverse else i
            rows = pl.ds(pl.multiple_of(sl * SUBLANES, SUBLANES), SUBLANES)
            lr, li = _slab_scan(lr_ref[rows, :], li_ref[rows, :], coef_v, cr, ci, reverse)
            lr_ref[rows, :] = lr
            li_ref[rows, :] = li
            pr = jnp.where(row == first, cr, pltpu.roll(lr, sh_prev, 0))
            pi = jnp.where(row == first, ci, pltpu.roll(li, sh_prev, 0))
            xr, xi = xr_ref[rows, :], xi_ref[rows, :]
            ar_acc = ar_acc + xr * pr + xi * pi
            ai_acc = ai_acc + xr * pi - xi * pr
            return (jnp.broadcast_to(lr[last:last + 1, :], lr.shape), jnp.broadcast_to(li[last:last + 1, :], li.shape),
                    ar_acc, ai_acc)

        zero = jnp.zeros((SUBLANES, CHUNK_STATE), F32)
        cr, ci, ar_acc, ai_acc = lax.fori_loop(0, n_slab, slab, (carry_ref[0], carry_ref[1], zero, zero))
        carry_ref[0] = cr
        carry_ref[1] = ci
        da_ref[0, :, :CHUNK_STATE] += ar_acc
        da_ref[0, :, CHUNK_STATE:] += ai_acc
        lam_r, lam_i = lr_ref[...], li_ref[...]
        u = u_ref[...]
        du_ref[...] = _dot_nt(lam_r, b_ref[0, :, :CHUNK_STATE]) + _dot_nt(lam_i, b_ref[0, :, CHUNK_STATE:])
        db_ref[0, :, :CHUNK_STATE] += _dot_tn(u, lam_r)
        db_ref[0, :, CHUNK_STATE:] += _dot_tn(u, lam_i)
        dyv = dy_ref[...]
        dc_ref[0, :CHUNK_STATE, :] += _dot_tn(xr_ref[...], dyv)
        dc_ref[0, CHUNK_STATE:, :] += _dot_tn(xi_ref[...], dyv)

    u_blk0 = P_U // LANES
    return pl.pallas_call(
        body, grid=(N_CHUNKS, n_t),
        in_specs=[pl.BlockSpec((t_blk, LANES), lambda c, t: (tmap(t), c)),
                  pl.BlockSpec((t_blk, LANES), lambda c, t: (tmap(t), u_blk0 + c)),
                  pl.BlockSpec((t_blk, CHUNK_STATE), lambda c, t: (tmap(t), c)),
                  pl.BlockSpec((t_blk, CHUNK_STATE), lambda c, t: (tmap(t), c)),
                  pl.BlockSpec((1, LANES, 2 * CHUNK_STATE), lambda c, t: (c, 0, 0)),
                  pl.BlockSpec((1, 2 * CHUNK_STATE, LANES), lambda c, t: (c, 0, 0)),
                  pl.BlockSpec((1, 8, SUBLANES, CHUNK_STATE), lambda c, t: (c, 0, 0, 0))],
        out_specs=[pl.BlockSpec((t_blk, LANES), lambda c, t: (tmap(t), c)),
                   pl.BlockSpec((1, SUBLANES, 2 * CHUNK_STATE), lambda c, t: (c, 0, 0)),
                   pl.BlockSpec((1, LANES, 2 * CHUNK_STATE), lambda c, t: (c, 0, 0)),
                   pl.BlockSpec((1, 2 * CHUNK_STATE, LANES), lambda c, t: (c, 0, 0))],
        out_shape=[jax.ShapeDtypeStruct((s, SSM_WIDTH), F32),
                   jax.ShapeDtypeStruct((N_CHUNKS, SUBLANES, 2 * CHUNK_STATE), F32),
                   jax.ShapeDtypeStruct((N_CHUNKS, LANES, 2 * CHUNK_STATE), F32),
                   jax.ShapeDtypeStruct((N_CHUNKS, 2 * CHUNK_STATE, LANES), F32)],
        scratch_shapes=[pltpu.VMEM((2, SUBLANES, CHUNK_STATE), F32), pltpu.VMEM((t_blk, CHUNK_STATE), F32),
                        pltpu.VMEM((t_blk, CHUNK_STATE), F32)],
        name=name, compiler_params=_cparams(("parallel", "arbitrary")),
    )(dy, proj, x_re, x_im, b_blk, c_blk, coef)


def _zoh(lam_re, lam_im, log_step, b_re, b_im):
    step = jnp.exp(log_step)[:, None]
    mag = jnp.exp(lam_re * step)
    abar_r = mag * jnp.cos(lam_im * step)
    abar_i = mag * jnp.sin(lam_im * step)
    nr = abar_r - 1.0
    ni = abar_i
    den = lam_re * lam_re + lam_im * lam_im
    fr = (nr * lam_re + ni * lam_im) / den
    fi = (ni * lam_re - nr * lam_im) / den
    bbar_r = fr[..., None] * b_re - fi[..., None] * b_im
    bbar_i = fr[..., None] * b_im + fi[..., None] * b_re
    return abar_r, abar_i, bbar_r, bbar_i


def _b_block(bbar_r, bbar_i):
    eye = jnp.eye(CHUNK_GROUPS, dtype=F32)

    def one(b):
        b = b.reshape(N_CHUNKS, CHUNK_GROUPS, SSM_STATE, SSM_GROUP)
        return jnp.einsum("cgnp,gh->cgphn", b, eye).reshape(N_CHUNKS, LANES, CHUNK_STATE)

    return jnp.concatenate([one(bbar_r), one(bbar_i)], axis=2)


def _b_unblock(db):
    eye = jnp.eye(CHUNK_GROUPS, dtype=F32)

    def one(d):
        d = d.reshape(N_CHUNKS, CHUNK_GROUPS, SSM_GROUP, CHUNK_GROUPS, SSM_STATE)
        return jnp.einsum("cgphn,gh->cgnp", d, eye).reshape(SSM_GROUPS, SSM_STATE, SSM_GROUP)

    return one(db[:, :, :CHUNK_STATE]), one(db[:, :, CHUNK_STATE:])


def _c_block(c_re, c_im):
    eye = jnp.eye(CHUNK_GROUPS, dtype=F32)

    def one(c):
        c = c.reshape(N_CHUNKS, CHUNK_GROUPS, SSM_GROUP, SSM_STATE)
        return jnp.einsum("cgpn,gh->cgnhp", c, eye).reshape(N_CHUNKS, CHUNK_STATE, LANES)

    return jnp.concatenate([one(c_re), -one(c_im)], axis=1)


def _c_unblock(dc):
    eye = jnp.eye(CHUNK_GROUPS, dtype=F32)

    def one(d):
        d = d.reshape(N_CHUNKS, CHUNK_GROUPS, SSM_STATE, CHUNK_GROUPS, SSM_GROUP)
        return jnp.einsum("cgnhp,gh->cgpn", d, eye).reshape(SSM_GROUPS, SSM_GROUP, SSM_STATE)

    return one(dc[:, :CHUNK_STATE, :]), -one(dc[:, CHUNK_STATE:, :])


def _exchange_copies(ins, outs, send_sems, recv_sems, local_sems, scatter):
    x, y, c = lax.axis_index("x"), lax.axis_index("y"), lax.axis_index("c")
    me = 4 * x + 2 * y + c
    copies = [pltpu.make_async_copy(ins[t].at[me] if scatter[t] else ins[t], outs[t].at[me], local_sems.at[t])
              for t in range(len(ins))]
    for k in range(1, N_DEV):
        peer = (x ^ ((k >> 2) & 1), y ^ ((k >> 1) & 1), c ^ (k & 1))
        peer_idx = 4 * peer[0] + 2 * peer[1] + peer[2]
        for t in range(len(ins)):
            copies.append(pltpu.make_async_remote_copy(
                src_ref=ins[t].at[peer_idx] if scatter[t] else ins[t], dst_ref=outs[t].at[me],
                send_sem=send_sems.at[t, k], recv_sem=recv_sems.at[t, k], device_id=peer,
                device_id_type=pl.DeviceIdType.MESH))
    return copies


def _call_with_exchange(body, *, grid, in_specs, out_specs, out_shape, scratch_shapes, args, semantics, name,
                        exchange=None, scatter=()):
    if exchange is None:
        return pl.pallas_call(body, grid=grid, in_specs=in_specs, out_specs=out_specs, out_shape=out_shape,
                              scratch_shapes=scratch_shapes, name=name, compiler_params=_cparams(semantics))(*args)
    n, n_in, n_out, n_scr = len(exchange), len(in_specs), len(out_specs), len(scratch_shapes)

    def wrapped(*refs):
        ins, refs = refs[:n_in], refs[n_in:]
        c_ins, refs = refs[:n], refs[n:]
        outs, refs = refs[:n_out], refs[n_out:]
        c_outs, refs = refs[:n], refs[n:]
        scr, sems = refs[:n_scr], refs[n_scr:]
        ids = [pl.program_id(a) for a in range(len(grid))]
        first = functools.reduce(jnp.logical_and, [i == 0 for i in ids])
        last = functools.reduce(jnp.logical_and, [i == g - 1 for i, g in zip(ids, grid)])

        @pl.when(first)
        def _():
            for cp in _exchange_copies(c_ins, c_outs, *sems, scatter):
                cp.start()

        body(*ins, *outs, *scr)

        @pl.when(last)
        def _():
            for cp in _exchange_copies(c_ins, c_outs, *sems, scatter):
                cp.wait()

    shapes = [tuple(b.shape[1:]) if sc else tuple(b.shape) for b, sc in zip(exchange, scatter)]
    hbm = pl.BlockSpec(memory_space=pl.ANY)
    res = pl.pallas_call(
        wrapped, grid=grid, in_specs=list(in_specs) + [hbm] * n, out_specs=list(out_specs) + [hbm] * n,
        out_shape=list(out_shape) + [jax.ShapeDtypeStruct((N_DEV,) + s, b.dtype) for s, b in zip(shapes, exchange)],
        scratch_shapes=list(scratch_shapes) + [pltpu.SemaphoreType.DMA((n, N_DEV)), pltpu.SemaphoreType.DMA((n, N_DEV)),
                                              pltpu.SemaphoreType.DMA((n,))],
        name=name, compiler_params=pltpu.CompilerParams(dimension_semantics=("arbitrary",) * len(grid),
                                                        vmem_limit_bytes=VMEM_LIMIT, has_side_effects=True),
    )(*args, *exchange)
    return res


def _attn_fwd(q, k, v, name, exchange=None):
    s = q.shape[0]
    tq = min(2048, s)
    tk = min(1024, s)
    n_k = s // tk

    def body(q_ref, k_ref, v_ref, o_ref, lse_ref, m_ref, acc_ref):
        m_ref[...] = jnp.full_like(m_ref, -jnp.inf)
        acc_ref[...] = jnp.zeros_like(acc_ref)
        qv = q_ref[...]

        def step(j, _):
            rows = pl.ds(pl.multiple_of(j * tk, tk), tk)
            sc = _dot_nt(qv, k_ref[rows, :])
            m_old = m_ref[...]
            m_new = jnp.maximum(m_old, jnp.max(sc, axis=1, keepdims=True))
            p = jnp.exp(sc - m_new)
            alpha = jnp.exp(m_old - m_new)
            acc_ref[...] = alpha * acc_ref[...] + _dot_nn(p, v_ref[rows, :])
            m_ref[...] = m_new
            return 0

        lax.fori_loop(0, n_k, step, 0, unroll=min(8, n_k))
        acc = acc_ref[...]
        l = acc[:, V_HEAD:V_HEAD + 1]
        o_ref[...] = acc / l
        lse = m_ref[...] + jnp.log(l)
        lse_ref[0] = jnp.broadcast_to(lse, (tq, LANES)).T[:SUBLANES, :]

    return _call_with_exchange(
        body, grid=(MLA_HEADS, s // tq),
        in_specs=[pl.BlockSpec((tq, HEAD_PAD), lambda h, i: (i, h)),
                  pl.BlockSpec((s, HEAD_PAD), lambda h, i: (0, h)),
                  pl.BlockSpec((s, HEAD_PAD), lambda h, i: (0, h))],
        out_specs=[pl.BlockSpec((tq, HEAD_PAD), lambda h, i: (i, h)),
                   pl.BlockSpec((1, SUBLANES, tq), lambda h, i: (h, 0, i))],
        out_shape=[jax.ShapeDtypeStruct((s, MLA_HEADS * HEAD_PAD), F32),
                   jax.ShapeDtypeStruct((MLA_HEADS, SUBLANES, s), F32)],
        scratch_shapes=[pltpu.VMEM((tq, 1), F32), pltpu.VMEM((tq, HEAD_PAD), F32)],
        args=(q, k, v), semantics=("parallel", "parallel"), name=name, exchange=exchange,
        scatter=(False,) * len(exchange or ()))


def _attn_bwd(q, k, v, o, do, lse, name, exchange=None, scatter=()):
    s = q.shape[0]
    tq = min(2048, s)
    tk = min(2048, s)

    def body(q_ref, k_ref, v_ref, o_ref, do_ref, lse_ref, dq_ref, dk_ref, dv_ref):
        j, i = pl.program_id(1), pl.program_id(2)

        @pl.when(jnp.logical_and(j == 0, i == 0))
        def _():
            dq_ref[...] = jnp.zeros_like(dq_ref)

        @pl.when(i == 0)
        def _():
            dk_ref[...] = jnp.zeros_like(dk_ref)
            dv_ref[...] = jnp.zeros_like(dv_ref)

        qv, kv, vv, dov = q_ref[...], k_ref[...], v_ref[...], do_ref[...]
        delta_col = jnp.sum(dov * o_ref[...], axis=1, keepdims=True)
        delta = jnp.broadcast_to(delta_col, (tq, LANES)).T[:1, :]
        st = _dot_nt(kv, qv)
        pt = jnp.exp(st - lse_ref[0, :1, :])
        dv_ref[...] += _dot_nn(pt, dov)
        dpt = _dot_nt(vv, dov)
        dst = pt * (dpt - delta)
        dk_ref[...] += _dot_nn(dst, qv)
        rows = pl.ds(pl.multiple_of(i * tq, tq), tq)
        dq_ref[rows, :] += _dot_tn(dst, kv)

    return _call_with_exchange(
        body, grid=(MLA_HEADS, s // tk, s // tq),
        in_specs=[pl.BlockSpec((tq, HEAD_PAD), lambda h, j, i: (i, h)),
                  pl.BlockSpec((tk, HEAD_PAD), lambda h, j, i: (j, h)),
                  pl.BlockSpec((tk, HEAD_PAD), lambda h, j, i: (j, h)),
                  pl.BlockSpec((tq, HEAD_PAD), lambda h, j, i: (i, h)),
                  pl.BlockSpec((tq, HEAD_PAD), lambda h, j, i: (i, h)),
                  pl.BlockSpec((1, SUBLANES, tq), lambda h, j, i: (h, 0, i))],
        out_specs=[pl.BlockSpec((s, HEAD_PAD), lambda h, j, i: (0, h)),
                   pl.BlockSpec((tk, HEAD_PAD), lambda h, j, i: (j, h)),
                   pl.BlockSpec((tk, HEAD_PAD), lambda h, j, i: (j, h))],
        out_shape=[jax.ShapeDtypeStruct((s, MLA_HEADS * HEAD_PAD), F32)] * 3, scratch_shapes=[],
        args=(q, k, v, o, do, lse), semantics=("parallel", "arbitrary", "arbitrary"), name=name,
        exchange=exchange, scatter=scatter)


def _pad_w_in(w):
    z = functools.partial(jnp.zeros, dtype=w.dtype)
    return jnp.concatenate([
        w[:, IN_GATE:IN_COLS], w[:, IN_U:IN_CQ], w[:, IN_CKV:IN_KR],
        z((D_MODEL, QK_NOPE)), w[:, IN_KR:IN_GATE], z((D_MODEL, HEAD_PAD - QK_HEAD)),
        z((D_MODEL, P_CQ - P_KR - HEAD_PAD)), w[:, IN_CQ:IN_CKV], z((D_MODEL, P_COLS - P_CQ - Q_LORA))], axis=1)


def _unpad_w_in(d):
    return jnp.concatenate([d[:, P_U:P_CKV], d[:, P_CQ:P_CQ + Q_LORA], d[:, P_CKV:P_KR],
                            d[:, P_KR + QK_NOPE:P_KR + QK_HEAD], d[:, P_GATE:P_U]], axis=1)


def _pad_heads_cols(w, real):
    k = w.shape[0]
    w = w.reshape(k, MLA_HEADS, real)
    return jnp.pad(w, ((0, 0), (0, 0), (0, HEAD_PAD - real))).reshape(k, MLA_HEADS * HEAD_PAD)


def _unpad_heads_cols(d, real):
    k = d.shape[0]
    return d.reshape(k, MLA_HEADS, HEAD_PAD)[:, :, :real].reshape(k, MLA_HEADS * real)


def _pad_kv(w):
    w = w.reshape(KV_LORA, MLA_HEADS, QK_NOPE + V_HEAD)
    kn = jnp.pad(w[:, :, :QK_NOPE], ((0, 0), (0, 0), (0, HEAD_PAD - QK_NOPE)))
    vv = jnp.pad(w[:, :, QK_NOPE:], ((0, 0), (0, 0), (0, HEAD_PAD - V_HEAD)))
    return jnp.concatenate([kn.reshape(KV_LORA, -1), vv.reshape(KV_LORA, -1)], axis=1)


def _unpad_kv(d):
    n = MLA_HEADS * HEAD_PAD
    kn = d[:, :n].reshape(KV_LORA, MLA_HEADS, HEAD_PAD)[:, :, :QK_NOPE]
    vv = d[:, n:].reshape(KV_LORA, MLA_HEADS, HEAD_PAD)[:, :, :V_HEAD]
    return jnp.concatenate([kn, vv], axis=2).reshape(KV_LORA, MLA_HEADS * (QK_NOPE + V_HEAD))


def _pad_out_mla(w):
    w = w.reshape(MLA_HEADS, V_HEAD, D_MODEL)
    return jnp.pad(w, ((0, 0), (0, HEAD_PAD - V_HEAD), (0, 0))).reshape(MLA_HEADS * HEAD_PAD, D_MODEL)


def _unpad_out_mla(d):
    return d.reshape(MLA_HEADS, HEAD_PAD, D_MODEL)[:, :V_HEAD, :].reshape(MLA_HEADS * V_HEAD, D_MODEL)


def _rope_tables(seq):
    half = QK_ROPE // 2
    inv_freq = ROPE_THETA ** (-jnp.arange(half, dtype=F32) / half)
    ang = jnp.arange(seq, dtype=F32)[:, None] * inv_freq[None, :]
    cos, sin = jnp.cos(ang), jnp.sin(ang)
    one, zero = jnp.ones((seq, QK_NOPE), F32), jnp.zeros((seq, half), F32)
    tail1, tail0 = jnp.ones((seq, HEAD_PAD - QK_HEAD), F32), jnp.zeros((seq, HEAD_PAD - QK_HEAD), F32)
    cf = jnp.concatenate([one, cos, cos, tail1], axis=1)
    sa = jnp.concatenate([0.0 * one, -sin, zero, tail0], axis=1)
    sb = jnp.concatenate([0.0 * one, zero, sin, tail0], axis=1)
    return cf, sa, sb


def _prep_layer(w):
    p = {}
    p["w_in_p"] = _pad_w_in(w["w_in"])
    p["w_glu"] = w["w_glu"]
    p["w_out_ssm"] = w["w_out_ssm"]
    p["w_q_p"] = _pad_heads_cols(w["w_q_up"], QK_HEAD)
    p["w_kv_p"] = _pad_kv(w["w_kv_up"])
    p["w_out_mla_p"] = _pad_out_mla(w["w_out_mla"])
    p["w_o"] = w["w_o"]
    p["w_ff1"] = w["w_ff1"]
    p["w_ff2"] = w["w_ff2"]
    p["mix_g"] = w["mix_norm_g"].reshape(1, D_MODEL)
    p["ffn_g"] = w["ffn_norm_g"].reshape(1, D_MODEL)
    p["b_gate"] = w["b_gate"]
    p["b_glu"] = w["b_glu"].reshape(1, SSM_WIDTH)
    p["d"] = w["ssm_d"].reshape(1, SSM_WIDTH)
    p["q_g"] = w["q_norm_g"].reshape(1, Q_LORA)
    p["kv_g"] = w["kv_norm_g"].reshape(1, KV_LORA)
    p["qh_g"] = jnp.pad(w["q_head_g"], (0, HEAD_PAD - QK_HEAD)).reshape(1, HEAD_PAD)
    p["kh_g"] = jnp.pad(w["k_head_g"], (0, HEAD_PAD - QK_HEAD)).reshape(1, HEAD_PAD)
    p["c_blk"] = _c_block(w["ssm_c_re"], w["ssm_c_im"]).astype(BF16)
    zoh, p["zoh_vjp"] = [], []
    for dr in range(2):
        out, vjp = jax.vjp(_zoh, w["ssm_lam_re"][dr], w["ssm_lam_im"][dr], w["ssm_log_step"][dr],
                           w["ssm_b_re"][dr], w["ssm_b_im"][dr])
        zoh.append(out)
        p["zoh_vjp"].append(vjp)
    p["b_blk"] = [_b_block(z[2], z[3]).astype(BF16) for z in zoh]
    p["coef_fwd"] = [_scan_tables(zoh[0][0], zoh[0][1], False), _scan_tables(zoh[1][0], zoh[1][1], True)]
    p["coef_adj"] = [_scan_tables(zoh[0][0], -zoh[0][1], True), _scan_tables(zoh[1][0], -zoh[1][1], False)]
    return p


def _head_prep_fwd(q_raw, kv_raw, proj, tabs, p, li):
    cf, sa, sb = tabs
    scale = QK_HEAD ** -0.5

    def fn(qr, kn, vv, kr, cfv, sav, sbv, gq, gk):
        qo, ko = [], []
        for h in range(MLA_HEADS):
            sl = slice(h * HEAD_PAD, (h + 1) * HEAD_PAD)
            qo.append(_rope(_rms(qr[:, sl], gq, QK_HEAD), cfv, sav, sbv) * scale)
            ko.append(_rope(_rms(kn[:, sl] + kr, gk, QK_HEAD), cfv, sav, sbv))
        lane = lax.broadcasted_iota(jnp.int32, vv.shape, 1)
        return jnp.concatenate(qo, axis=1), jnp.concatenate(ko, axis=1), jnp.where(lane % HEAD_PAD == V_HEAD, 1.0, vv)

    n = MLA_HEADS * HEAD_PAD
    return _rowwise(fn, [(q_raw, n, 0), (kv_raw, n, 0), (kv_raw, n, 1), (proj, HEAD_PAD, P_KR // HEAD_PAD),
                         (cf, HEAD_PAD, 0), (sa, HEAD_PAD, 0), (sb, HEAD_PAD, 0)], [p["qh_g"], p["kh_g"]],
                    [(n, BF16), (n, BF16), (n, BF16)], tm=256, name=f"head_prep_fwd_{li}")


def _head_prep_bwd(dq, dk, dv, q_raw, kv_raw, proj, tabs, p, li):
    cf, sa, sb = tabs
    scale = QK_HEAD ** -0.5

    def fn(dqv, dkv, dvv, qr, kn, kr, cfv, sav, sbv, gq, gk):
        dqo, dko = [], []
        dkr = jnp.zeros_like(kr)
        dgq = jnp.zeros((1, HEAD_PAD), F32)
        dgk = jnp.zeros((1, HEAD_PAD), F32)
        for h in range(MLA_HEADS):
            sl = slice(h * HEAD_PAD, (h + 1) * HEAD_PAD)
            dx, dg = _rms_bwd(qr[:, sl], gq, _rope_t(dqv[:, sl] * scale, cfv, sav, sbv), QK_HEAD)
            dqo.append(dx)
            dgq = dgq + dg
            dx, dg = _rms_bwd(kn[:, sl] + kr, gk, _rope_t(dkv[:, sl], cfv, sav, sbv), QK_HEAD)
            dko.append(dx)
            dkr = dkr + dx
            dgk = dgk + dg
        return jnp.concatenate(dqo, axis=1), jnp.concatenate(dko + [dvv], axis=1), dkr, dgq, dgk

    n = MLA_HEADS * HEAD_PAD
    return _rowwise(fn, [(dq, n, 0), (dk, n, 0), (dv, n, 0), (q_raw, n, 0), (kv_raw, n, 0),
                         (proj, HEAD_PAD, P_KR // HEAD_PAD), (cf, HEAD_PAD, 0), (sa, HEAD_PAD, 0), (sb, HEAD_PAD, 0)],
                    [p["qh_g"], p["kh_g"]], [(n, BF16), (2 * n, BF16), (HEAD_PAD, BF16)],
                    [(HEAD_PAD, F32), (HEAD_PAD, F32)], tm=256, name=f"head_prep_bwd_{li}")


def _layer_fwd(x, p, tabs, li, exchange=None):
    sv = {"x": x}
    h = _rowwise(lambda xv, g: _rms(xv, g, D_MODEL), [(x, D_MODEL, 0)], [p["mix_g"]], [(D_MODEL, BF16)],
                 name=f"mix_norm_{li}")[0]
    proj = _mm_nn(h, p["w_in_p"], name=f"in_proj_{li}")
    sv["h"], sv["proj"] = h, proj
    y_f, xr_f, xi_f = _s5_scan_fwd(proj, p["b_blk"][0], p["c_blk"], p["coef_fwd"][0], False, f"s5_fwd_f_{li}")
    y_b, xr_b, xi_b = _s5_scan_fwd(proj, p["b_blk"][1], p["c_blk"], p["coef_fwd"][1], True, f"s5_fwd_b_{li}")
    sv["states"] = [(xr_f, xi_f), (xr_b, xi_b)]
    y_raw, yg = _rowwise(lambda a, b, u, d: (a + b + d * u, _gelu(a + b + d * u)),
                         [(y_f, SSM_WIDTH, 0), (y_b, SSM_WIDTH, 0), (proj, SSM_WIDTH, P_U // SSM_WIDTH)], [p["d"]],
                         [(SSM_WIDTH, F32), (SSM_WIDTH, BF16)], name=f"s5_gelu_{li}")
    z = _mm_nn(yg, p["w_glu"], name=f"glu_proj_{li}")
    y_ssm = _rowwise(lambda yr, zv, b: _gelu(yr) * _sigmoid(zv + b), [(y_raw, SSM_WIDTH, 0), (z, SSM_WIDTH, 0)],
                     [p["b_glu"]], [(SSM_WIDTH, BF16)], name=f"glu_{li}")[0]
    sv.update(y_raw=y_raw, yg=yg, z=z, y_ssm=y_ssm)
    cqn, ckvn = _rowwise(lambda cq, ckv, gq, gkv: (_rms(cq, gq, Q_LORA), _rms(ckv, gkv, KV_LORA)),
                         [(proj, Q_LORA, P_CQ // Q_LORA), (proj, KV_LORA, P_CKV // KV_LORA)], [p["q_g"], p["kv_g"]],
                         [(Q_LORA, BF16), (KV_LORA, BF16)], name=f"lora_norm_{li}")
    q_raw = _mm_nn(cqn, p["w_q_p"], name=f"q_up_{li}")
    kv_raw = _mm_nn(ckvn, p["w_kv_p"], name=f"kv_up_{li}")
    q, k, v = _head_prep_fwd(q_raw, kv_raw, proj, tabs, p, li)
    o, lse, *gathered = _attn_fwd(q, k, v, f"attn_fwd_{li}", exchange)
    sv.update(cqn=cqn, ckvn=ckvn, q_raw=q_raw, kv_raw=kv_raw, q=q, k=k, v=v, o=o, lse=lse)
    t_ssm = _mm_nn(y_ssm, p["w_out_ssm"], name=f"out_ssm_{li}")
    t_mla = _mm_nn(o, p["w_out_mla_p"], name=f"out_mla_{li}")
    merged = _rowwise(lambda g0, g1, ts, tmv, b: _sigmoid(g0 + b[0:1]) * ts + _sigmoid(g1 + b[1:2]) * tmv,
                      [(proj, D_MODEL, 0), (proj, D_MODEL, 1), (t_ssm, D_MODEL, 0), (t_mla, D_MODEL, 0)],
                      [p["b_gate"]], [(D_MODEL, BF16)], name=f"merge_{li}")[0]
    x1 = _mm_nn(merged, p["w_o"], add=x, name=f"o_proj_{li}")
    sv.update(t_ssm=t_ssm, t_mla=t_mla, merged=merged, x1=x1)
    h2 = _rowwise(lambda xv, g: _rms(xv, g, D_MODEL), [(x1, D_MODEL, 0)], [p["ffn_g"]], [(D_MODEL, BF16)],
                  name=f"ffn_norm_{li}")[0]
    a, r = _mm_nn(h2, p["w_ff1"], epilogue=lambda acc: (acc, jnp.square(jnp.maximum(acc, 0.0))),
                  out_dtypes=(F32, BF16), name=f"ff1_{li}")
    x2 = _mm_nn(r, p["w_ff2"], add=x1, name=f"ff2_{li}")
    sv.update(h2=h2, a=a, r=r)
    return x2, sv, gathered


def _layer_bwd(dx2, dx2_b, sv, p, tabs, li, pending=None, send_early=None):
    g = {}
    da = _mm_nt(dx2_b, p["w_ff2"], extras=(sv["a"],), epilogue=lambda acc, av: acc * (2.0 * jnp.maximum(av, 0.0)),
                out_dtypes=(BF16,), name=f"d_ff2_x_{li}")
    g["w_ff2"] = _mm_tn(sv["r"], dx2_b, name=f"d_ff2_w_{li}")
    dh2 = _mm_nt(da, p["w_ff1"], name=f"d_ff1_x_{li}")
    g["w_ff1"] = _mm_tn(sv["h2"], da, name=f"d_ff1_w_{li}")

    def norm_bwd(xv, dyv, dres, gg):
        dx, dg = _rms_bwd(xv, gg, dyv, D_MODEL)
        return dres + dx, dres + dx, dg

    dx1, dx1_b, dg = _rowwise(norm_bwd, [(sv["x1"], D_MODEL, 0), (dh2, D_MODEL, 0), (dx2, D_MODEL, 0)], [p["ffn_g"]],
                              [(D_MODEL, F32), (D_MODEL, BF16)], [(D_MODEL, F32)], name=f"d_ffn_norm_{li}")
    g["ffn_norm_g"] = dg.reshape(D_MODEL)
    dmerged = _mm_nt(dx1_b, p["w_o"], name=f"d_o_x_{li}")
    g["w_o"] = _mm_tn(sv["merged"], dx1_b, name=f"d_o_w_{li}")

    def merge_bwd(dm, g0, g1, ts, tmv, b):
        s0, s1 = _sigmoid(g0 + b[0:1]), _sigmoid(g1 + b[1:2])
        dg0, dg1 = dm * ts * s0 * (1.0 - s0), dm * tmv * s1 * (1.0 - s1)
        return (dm * s0, dm * s1, jnp.concatenate([dg0, dg1], axis=1),
                jnp.sum(dg0, axis=0, keepdims=True), jnp.sum(dg1, axis=0, keepdims=True))

    proj = sv["proj"]
    dt_ssm, dt_mla, dgate, db0, db1 = _rowwise(
        merge_bwd, [(dmerged, D_MODEL, 0), (proj, D_MODEL, 0), (proj, D_MODEL, 1), (sv["t_ssm"], D_MODEL, 0),
                    (sv["t_mla"], D_MODEL, 0)], [p["b_gate"]],
        [(D_MODEL, BF16), (D_MODEL, BF16), (2 * D_MODEL, BF16)], [(D_MODEL, F32), (D_MODEL, F32)], tm=256,
        name=f"d_merge_{li}")
    g["b_gate"] = jnp.concatenate([db0, db1], axis=0)
    dy_ssm = _mm_nt(dt_ssm, p["w_out_ssm"], name=f"d_out_ssm_x_{li}")
    g["w_out_ssm"] = _mm_tn(sv["y_ssm"], dt_ssm, name=f"d_out_ssm_w_{li}")
    do = _mm_nt(dt_mla, p["w_out_mla_p"], name=f"d_out_mla_x_{li}")
    g["w_out_mla"] = _unpad_out_mla(_mm_tn(sv["o"], dt_mla, name=f"d_out_mla_w_{li}"))

    def glu_bwd(dyv, yr, zv, b):
        yg = _gelu(yr)
        sg = _sigmoid(zv + b)
        dz = dyv * yg * sg * (1.0 - sg)
        return dz, dyv * sg, jnp.sum(dz, axis=0, keepdims=True)

    dz, dyg_direct, dbglu = _rowwise(glu_bwd, [(dy_ssm, SSM_WIDTH, 0), (sv["y_raw"], SSM_WIDTH, 0), (sv["z"], SSM_WIDTH, 0)],
                                     [p["b_glu"]], [(SSM_WIDTH, BF16), (SSM_WIDTH, F32)], [(SSM_WIDTH, F32)],
                                     name=f"d_glu_{li}")
    g["b_glu"] = dbglu.reshape(SSM_WIDTH)
    dyg_mm = _mm_nt(dz, p["w_glu"], name=f"d_glu_x_{li}")
    g["w_glu"] = _mm_tn(sv["yg"], dz, name=f"d_glu_w_{li}")

    def gelu_bwd(d1, d2, yr, u, d):
        dyr = (d1 + d2) * _gelu_grad(yr)
        return dyr, dyr * d, jnp.sum(dyr * u, axis=0, keepdims=True)

    dy_raw, du_d, dd = _rowwise(gelu_bwd, [(dyg_direct, SSM_WIDTH, 0), (dyg_mm, SSM_WIDTH, 0), (sv["y_raw"], SSM_WIDTH, 0),
                                           (proj, SSM_WIDTH, P_U // SSM_WIDTH)], [p["d"]],
                                [(SSM_WIDTH, BF16), (SSM_WIDTH, F32)], [(SSM_WIDTH, F32)], name=f"d_gelu_{li}")
    g["ssm_d"] = dd.reshape(SSM_GROUPS, SSM_GROUP)
    du_parts, dc_sum = [du_d], None
    zoh_grads = []
    for dr_i in range(2):
        xr, xi = sv["states"][dr_i]
        du_i, da_i, db_i, dc_i = _s5_scan_bwd(dy_raw, proj, xr, xi, p["b_blk"][dr_i], p["c_blk"], p["coef_adj"][dr_i],
                                              dr_i == 0, f"s5_bwd_{'fb'[dr_i]}_{li}")
        du_parts.append(du_i)
        dc_sum = dc_i if dc_sum is None else dc_sum + dc_i
        da_i = jnp.sum(da_i, axis=1)
        dar = da_i[:, :CHUNK_STATE].reshape(SSM_GROUPS, SSM_STATE)
        dai = da_i[:, CHUNK_STATE:].reshape(SSM_GROUPS, SSM_STATE)
        dbr, dbi = _b_unblock(db_i)
        zoh_grads.append(p["zoh_vjp"][dr_i]((dar, dai, dbr, dbi)))
    for k_i, nm in enumerate(("ssm_lam_re", "ssm_lam_im", "ssm_log_step", "ssm_b_re", "ssm_b_im")):
        g[nm] = jnp.stack([zoh_grads[0][k_i], zoh_grads[1][k_i]], axis=0)
    g["ssm_c_re"], g["ssm_c_im"] = _c_unblock(dc_sum)
    du = _rowwise(lambda a, b, c: a + b + c, [(d_, SSM_WIDTH, 0) for d_ in du_parts], [], [(SSM_WIDTH, BF16)],
                  name=f"d_u_sum_{li}")[0]
    bufs, flags = pending if pending is not None else ([], [])
    n_pending = len(bufs)
    if send_early is not None:
        early_bufs, early_flags = send_early(g)
        bufs, flags = list(bufs) + list(early_bufs), list(flags) + list(early_flags)
    dq, dk, dv, *got = _attn_bwd(sv["q"], sv["k"], sv["v"], sv["o"], do, sv["lse"], f"attn_bwd_{li}",
                                 bufs or None, tuple(flags))
    dq_raw, dkv_raw, dkr, dgq, dgk = _head_prep_bwd(dq, dk, dv, sv["q_raw"], sv["kv_raw"], proj, tabs, p, li)
    g["q_head_g"] = dgq.reshape(HEAD_PAD)[:QK_HEAD]
    g["k_head_g"] = dgk.reshape(HEAD_PAD)[:QK_HEAD]
    dcqn = _mm_nt(dq_raw, p["w_q_p"], name=f"d_q_up_x_{li}")
    g["w_q_up"] = _unpad_heads_cols(_mm_tn(sv["cqn"], dq_raw, name=f"d_q_up_w_{li}"), QK_HEAD)
    dckvn = _mm_nt(dkv_raw, p["w_kv_p"], name=f"d_kv_up_x_{li}")
    g["w_kv_up"] = _unpad_kv(_mm_tn(sv["ckvn"], dkv_raw, name=f"d_kv_up_w_{li}"))

    def lora_bwd(cq, ckv, d1, d2, gq, gkv):
        dx1_, dg1 = _rms_bwd(cq, gq, d1, Q_LORA)
        dx2_, dg2 = _rms_bwd(ckv, gkv, d2, KV_LORA)
        return dx1_, dx2_, dg1, dg2

    dcq, dckv, dgqn, dgkvn = _rowwise(
        lora_bwd, [(proj, Q_LORA, P_CQ // Q_LORA), (proj, KV_LORA, P_CKV // KV_LORA), (dcqn, Q_LORA, 0), (dckvn, KV_LORA, 0)],
        [p["q_g"], p["kv_g"]], [(Q_LORA, BF16), (KV_LORA, BF16)], [(Q_LORA, F32), (KV_LORA, F32)], name=f"d_lora_norm_{li}")
    g["q_norm_g"], g["kv_norm_g"] = dgqn.reshape(Q_LORA), dgkvn.reshape(KV_LORA)
    gap = jnp.zeros((dx2.shape[0], P_CQ - P_KR - HEAD_PAD), BF16)
    tail = jnp.zeros((dx2.shape[0], P_COLS - P_CQ - Q_LORA), BF16)
    dproj = jnp.concatenate([dgate, du, dckv, dkr, gap, dcq, tail], axis=1)
    dh = _mm_nt(dproj, p["w_in_p"], name=f"d_in_x_{li}")
    g["w_in"] = _unpad_w_in(_mm_tn(sv["h"], dproj, name=f"d_in_w_{li}"))
    dx, dx_b, dg = _rowwise(norm_bwd, [(sv["x"], D_MODEL, 0), (dh, D_MODEL, 0), (dx1, D_MODEL, 0)], [p["mix_g"]],
                            [(D_MODEL, F32), (D_MODEL, BF16)], [(D_MODEL, F32)], name=f"d_mix_norm_{li}")
    g["mix_norm_g"] = dg.reshape(D_MODEL)
    return dx, dx_b, g, got[:n_pending], got[n_pending:]


def _local_step(x, target, layer_weights, send_weights=None, send_early=None, send_late=None):
    tabs = _rope_tables(x.shape[0])
    saved, preps = [], []
    gathered = None
    for li in range(DEPTH):
        p = _prep_layer(layer_weights(li, gathered))
        nxt = send_weights(li + 1) if send_weights is not None and li + 1 < DEPTH else None
        x, sv, gathered = _layer_fwd(x, p, tabs, li, nxt)
        saved.append(sv)
        preps.append(p)

    def loss_fn(y, t):
        err = y - t
        d = err * (1.0 / D_MODEL)
        return d, d, jnp.sum(jnp.sum(err * err, axis=1, keepdims=True), axis=0, keepdims=True) * jnp.ones((1, LANES), F32)

    dx, dx_b, lsum = _rowwise(loss_fn, [(x, D_MODEL, 0), (target, D_MODEL, 0)], [], [(D_MODEL, F32), (D_MODEL, BF16)],
                              [(LANES, F32)], name="loss")
    loss = 0.5 * lsum[0, 0] * (1.0 / D_MODEL)
    grads, early, late = [None] * DEPTH, [None] * DEPTH, [None] * DEPTH
    pending = None
    for li in reversed(range(DEPTH)):
        dx, dx_b, grads[li], got_late, early[li] = _layer_bwd(dx, dx_b, saved[li], preps[li], tabs, li, pending, send_early)
        if pending is not None:
            late[li + 1] = got_late
        pending = send_late(grads[li]) if send_late is not None else None
    return loss, dx, grads, early, late, pending


def _exchange(bufs, scatter, name):
    n = len(bufs)

    def body(*refs):
        copies = _exchange_copies(refs[:n], refs[n:2 * n], *refs[2 * n:], scatter)
        for cp in copies:
            cp.start()
        for cp in copies:
            cp.wait()

    shapes = [tuple(b.shape[1:]) if sc else tuple(b.shape) for b, sc in zip(bufs, scatter)]
    return pl.pallas_call(
        body, out_shape=[jax.ShapeDtypeStruct((N_DEV,) + s, b.dtype) for s, b in zip(shapes, bufs)],
        in_specs=[pl.BlockSpec(memory_space=pl.ANY)] * n, out_specs=[pl.BlockSpec(memory_space=pl.ANY)] * n,
        scratch_shapes=[pltpu.SemaphoreType.DMA((n, N_DEV)), pltpu.SemaphoreType.DMA((n, N_DEV)),
                        pltpu.SemaphoreType.DMA((n,))],
        name=name, compiler_params=pltpu.CompilerParams(has_side_effects=True),
    )(*bufs)


def _adamw(parts, w, m, v, name):
    shape = w.shape
    cols = shape[-1]
    r = math.prod(shape[:-1])
    parts, w, m, v = parts.reshape(N_DEV, r, cols), w.reshape(r, cols), m.reshape(r, cols), v.reshape(r, cols)
    tm = _pick_rows(r, cols)

    def body(p_ref, w_ref, m_ref, v_ref, g_ref, d_ref, nm_ref, nv_ref):
        g = p_ref[0].astype(F32)
        for j in range(1, N_DEV):
            g = g + p_ref[j].astype(F32)
        m_new = ADAM_B1 * m_ref[...] + (1.0 - ADAM_B1) * g
        v_new = ADAM_B2 * v_ref[...] + (1.0 - ADAM_B2) * (g * g)
        m_hat = m_new / (1.0 - ADAM_B1 ** ADAM_STEP)
        v_hat = v_new / (1.0 - ADAM_B2 ** ADAM_STEP)
        g_ref[...] = g
        d_ref[...] = -ADAM_LR * (m_hat / (jnp.sqrt(v_hat) + ADAM_EPS) + ADAM_WD * w_ref[...])
        nm_ref[...] = m_new
        nv_ref[...] = v_new

    spec = pl.BlockSpec((tm, cols), lambda i: (i, 0))
    res = pl.pallas_call(
        body, grid=(r // tm,), in_specs=[pl.BlockSpec((N_DEV, tm, cols), lambda i: (0, i, 0)), spec, spec, spec],
        out_specs=[spec] * 4, out_shape=[jax.ShapeDtypeStruct((r, cols), F32)] * 4, name=name,
        compiler_params=_cparams(("parallel",)),
    )(parts, w, m, v)
    return [a.reshape(shape) for a in res]


def _pick_rows(r, cols):
    for t in (512, 256, 128, 64, 32, 16):
        if r % t == 0 and t * cols <= 512 * 512:
            return t
    return r


def _pack(arrs, dtype, row_mult):
    flat = jnp.concatenate([a.reshape(-1).astype(dtype) for a in arrs])
    n = flat.shape[0]
    per = row_mult * D_MODEL
    total = -(-n // per) * per
    return jnp.pad(flat, (0, total - n)).reshape(total // D_MODEL, D_MODEL)


def _unpack(flat, shapes):
    lead = flat.shape[:-2]
    flat = flat.reshape(lead + (-1,))
    out, off = [], 0
    for shp in shapes:
        n = math.prod(shp)
        out.append(flat[..., off:off + n].reshape(lead + tuple(shp)))
        off += n
    return out


def _to_shards(gfull, axis):
    shp = gfull.shape
    gfull = gfull.reshape(shp[:axis] + (N_DEV, shp[axis] // N_DEV) + shp[axis + 1:])
    return jnp.moveaxis(gfull, axis, 0)


def _from_shards(parts, axis):
    parts = jnp.moveaxis(parts, 0, axis)
    shp = parts.shape
    return parts.reshape(shp[:axis] + (shp[axis] * shp[axis + 1],) + shp[axis + 2:])


def kernel(x, mix_norm_g, w_in, b_gate, ssm_lam_re, ssm_lam_im, ssm_log_step, ssm_b_re, ssm_b_im, ssm_c_re, ssm_c_im, ssm_d, w_glu, b_glu, w_out_ssm, q_norm_g, kv_norm_g, w_q_up, w_kv_up, q_head_g, k_head_g, w_out_mla, w_o, ffn_norm_g, w_ff1, w_ff2, loss_target, m_mix_norm_g, m_w_in, m_b_gate, m_ssm_lam_re, m_ssm_lam_im, m_ssm_log_step, m_ssm_b_re, m_ssm_b_im, m_ssm_c_re, m_ssm_c_im, m_ssm_d, m_w_glu, m_b_glu, m_w_out_ssm, m_q_norm_g, m_kv_norm_g, m_w_q_up, m_w_kv_up, m_q_head_g, m_k_head_g, m_w_out_mla, m_w_o, m_ffn_norm_g, m_w_ff1, m_w_ff2, v_mix_norm_g, v_w_in, v_b_gate, v_ssm_lam_re, v_ssm_lam_im, v_ssm_log_step, v_ssm_b_re, v_ssm_b_im, v_ssm_c_re, v_ssm_c_im, v_ssm_d, v_w_glu, v_b_glu, v_w_out_ssm, v_q_norm_g, v_kv_norm_g, v_w_q_up, v_w_kv_up, v_q_head_g, v_k_head_g, v_w_out_mla, v_w_o, v_ffn_norm_g, v_w_ff1, v_w_ff2):
    args = dict(locals())
    w = {n: args[n] for n in WEIGHTS}
    m = {n: args["m_" + n] for n in WEIGHTS}
    v = {n: args["v_" + n] for n in WEIGHTS}

    def send_weights(li):
        return [w[n][li] if n == "b_gate" else w[n][li].astype(BF16) for n in SHARDED]

    first = _exchange(send_weights(0), [False] * len(SHARDED), "weight_all_gather_0")

    def layer_weights(li, gathered):
        full = {n: _from_shards(pt, SHARD_AXIS[n] - 1) for n, pt in zip(SHARDED, first if li == 0 else gathered)}
        for n in REPLICATED:
            full[n] = w[n][li]
        return full

    def shards(g, names):
        return [_to_shards(g[n], SHARD_AXIS[n] - 1).astype(BF16) for n in names]

    def send_early(g):
        return shards(g, EARLY), [True] * len(EARLY)

    def send_late(g):
        return shards(g, LATE) + [_pack([g[n] for n in REPLICATED], F32, 8)], [True] * len(LATE) + [False]

    loss_part, dx, grads, early, late, pending = _local_step(x[0], loss_target[0], layer_weights, send_weights,
                                                             send_early, send_late)
    late[0] = _exchange(*pending, "grad_exchange_0")
    loss = lax.psum(loss_part, ("x", "y", "c"))

    outs = {}
    for n in SHARDED:
        src, t = (early, EARLY.index(n)) if n in EARLY else (late, LATE.index(n))
        parts = jnp.stack([src[li][t] for li in range(DEPTH)], axis=1)
        for kind, a in zip(("grad", "delta", "new_m", "new_v"), _adamw(parts, w[n], m[n], v[n], "adamw_" + n)):
            outs[kind + "_" + n] = a
    r_parts = jnp.stack([late[li][len(LATE)] for li in range(DEPTH)], axis=1)

    def pack_layers(d):
        return jnp.stack([_pack([d[n][li] for n in REPLICATED], F32, 8) for li in range(DEPTH)], axis=0)

    res = _adamw(r_parts, pack_layers(w), pack_layers(m), pack_layers(v), "adamw_replicated")
    rep_shapes = [w[n].shape[1:] for n in REPLICATED]
    for kind, flat in zip(("grad", "delta", "new_m", "new_v"), res):
        for n, a in zip(REPLICATED, _unpack(flat, rep_shapes)):
            outs[kind + "_" + n] = a
    return (loss, dx[None], *[outs[k + "_" + n] for k in ("grad", "delta", "new_m", "new_v") for n in WEIGHTS])
```
